```python
import math
import jax
import jax.numpy as jnp
from jax import lax
import numpy as np

D_MODEL = 1024
BATCH = 2
SEQ = 8192
DEPTH = 2

GRID_W = 64
CTX_LEN = 256
HEAD_DIM = 64
ROPE_PAIRS_PER_AXIS = HEAD_DIM // 4
ROPE_THETA = 10000.0
EPS = 1e-6
Q_BLOCK = 128

NA_HEADS = 8
WIN_ROWS = 8
WIN_COLS = 16
GQA_HEADS = 8
GQA_KV_HEADS = 2
NA_W = NA_HEADS * HEAD_DIM
GQA_Q_W = GQA_HEADS * HEAD_DIM
GQA_KV_W = GQA_KV_HEADS * HEAD_DIM
AB_SPLITS = [NA_W, 2 * NA_W, 3 * NA_W, 3 * NA_W + GQA_Q_W, 3 * NA_W + GQA_Q_W + GQA_KV_W]
AB_IN = 3 * NA_W + GQA_Q_W + 2 * GQA_KV_W
AB_OUT = NA_W + GQA_Q_W

DIFF_HEADS = D_MODEL // (2 * HEAD_DIM)
DIFF_W = DIFF_HEADS * 2 * HEAD_DIM
DIFF_IN = 3 * DIFF_W

N_EXPERTS = 64
TOP_K = 8
D_EXPERT = D_MODEL // 4
ROUTED_SCALE = 2.5
MOE_BLOCK = 128

N_EVEN = (DEPTH + 1) // 2
N_ODD = DEPTH // 2

kernel_name = 'hybrid_na_gqa_diffattn_moe_dit'


def rms_norm(x, g):
    xf = x.astype(jnp.float32)
    y = xf * lax.rsqrt(jnp.mean(xf * xf, axis=-1, keepdims=True) + EPS)
    return (y * g.astype(jnp.float32)).astype(x.dtype)


def modulate(h, shift, scale):
    return h * (1.0 + scale) + shift


def to_heads(t, n_heads):
    b, s, _ = t.shape
    return t.reshape(b, s, n_heads, -1).transpose(0, 2, 1, 3)


def from_heads(t):
    b, h, s, d = t.shape
    return t.transpose(0, 2, 1, 3).reshape(b, s, h * d)


def axial_rope(n):
    t = jnp.arange(n, dtype=jnp.int32)
    row = (t // GRID_W).astype(jnp.float32)
    col = (t % GRID_W).astype(jnp.float32)
    inv_freq = ROPE_THETA ** (-jnp.arange(ROPE_PAIRS_PER_AXIS, dtype=jnp.float32) / ROPE_PAIRS_PER_AXIS)
    ang = jnp.concatenate([row[:, None] * inv_freq, col[:, None] * inv_freq], axis=-1)
    return jnp.cos(ang), jnp.sin(ang)


def apply_rope(x, cos, sin):
    xf = x.astype(jnp.float32)
    x1, x2 = xf[..., 0::2], xf[..., 1::2]
    out = jnp.stack([x1 * cos - x2 * sin, x1 * sin + x2 * cos], axis=-1).reshape(x.shape)
    return out.astype(x.dtype)


def gqa_attention(q, k, v):
    b, g, r, nq, d = q.shape
    blk = min(Q_BLOCK, nq)
    nb = nq // blk
    qb = jnp.moveaxis(q.reshape(b, g, r, nb, blk, d), 3, 0)
    scale = d ** -0.5

    def one_block(qblk):
        s = jnp.einsum('bgrqd,bgsd->bgrqs', qblk, k).astype(jnp.float32) * scale
        p = jax.nn.softmax(s, axis=-1).astype(v.dtype)
        return jnp.einsum('bgrqs,bgsd->bgrqd', p, v)

    o = lax.map(one_block, qb)
    return jnp.moveaxis(o, 0, 3).reshape(b, g * r, nq, d)


def neighbourhood_attention(q, k, v, k_ctx, v_ctx, rpb):
    b, h, n, d = q.shape
    rows = n // GRID_W
    kr = min(WIN_ROWS, rows)
    kc = WIN_COLS
    scale = d ** -0.5
    qg = q.reshape(b, h, rows, GRID_W, d)
    kg = k.reshape(b, h, rows, GRID_W, d)
    vg = v.reshape(b, h, rows, GRID_W, d)
    cols = jnp.arange(GRID_W, dtype=jnp.int32)
    col_start = jnp.clip(cols - kc // 2, 0, GRID_W - kc)
    col_idx = col_start[:, None] + jnp.arange(kc, dtype=jnp.int32)[None, :]
    col_bias_idx = col_idx - cols[:, None] + WIN_COLS - 1
    rpb_cols = rpb[:, :, col_bias_idx]

    def one_row(r):
        rs = jnp.clip(r - kr // 2, 0, rows - kr)
        k_rows = lax.dynamic_slice_in_dim(kg, rs, kr, axis=2)
        v_rows = lax.dynamic_slice_in_dim(vg, rs, kr, axis=2)
        k_win = k_rows[:, :, :, col_idx]
        v_win = v_rows[:, :, :, col_idx]
        q_row = lax.dynamic_index_in_dim(qg, r, axis=2, keepdims=False)
        row_bias_idx = rs + jnp.arange(kr, dtype=jnp.int32) - r + WIN_ROWS - 1
        bias = rpb_cols[:, row_bias_idx].transpose(0, 2, 1, 3).astype(jnp.float32)
        s_loc = jnp.einsum('bhqd,bhiqjd->bhqij', q_row, k_win).astype(jnp.float32) * scale + bias[None]
        s_ctx = jnp.einsum('bhqd,bhsd->bhqs', q_row, k_ctx).astype(jnp.float32) * scale
        s = jnp.concatenate([s_loc.reshape(b, h, GRID_W, kr * kc), s_ctx], axis=-1)
        p = jax.nn.softmax(s, axis=-1).astype(v.dtype)
        p_loc = p[..., :kr * kc].reshape(b, h, GRID_W, kr, kc)
        p_ctx = p[..., kr * kc:]
        return (jnp.einsum('bhqij,bhiqjd->bhqd', p_loc, v_win)
                + jnp.einsum('bhqs,bhsd->bhqd', p_ctx, v_ctx))

    o = lax.map(one_row, jnp.arange(rows, dtype=jnp.int32))
    return o.transpose(1, 2, 0, 3, 4).reshape(b, h, n, d)


def diff_attention(q, k, v, lam):
    b, h, _, nq, d = q.shape
    blk = min(Q_BLOCK, nq)
    nb = nq // blk
    qb = jnp.moveaxis(q.reshape(b, h, 2, nb, blk, d), 3, 0)
    scale = d ** -0.5

    def one_block(qblk):
        s = jnp.einsum('bhcqd,bhcsd->bhcqs', qblk, k).astype(jnp.float32) * scale
        p = jax.nn.softmax(s, axis=-1)
        pd = (p[:, :, 0] - lam * p[:, :, 1]).astype(v.dtype)
        return jnp.einsum('bhqs,bhse->bhqe', pd, v)

    o = lax.map(one_block, qb)
    return jnp.moveaxis(o, 0, 2).reshape(b, h, nq, 2 * d)


def group_q(q):
    b, h, s, d = q.shape
    return q.reshape(b, GQA_KV_HEADS, h // GQA_KV_HEADS, s, d)


def na_gqa_mixer(h_ctx, h_lat, w_in, w_out, rpb, q_gain, k_gain, cos, sin, need_ctx):
    qa_c, ka_c, va_c, qb_c, kb_c, vb_c = jnp.split(h_ctx @ w_in, AB_SPLITS, axis=-1)
    qa, ka, va, qb, kb, vb = jnp.split(h_lat @ w_in, AB_SPLITS, axis=-1)
    ka_c, va_c = to_heads(ka_c, NA_HEADS), to_heads(va_c, NA_HEADS)
    kb_c = rms_norm(to_heads(kb_c, GQA_KV_HEADS), k_gain)
    vb_c = to_heads(vb_c, GQA_KV_HEADS)
    qa, ka, va = to_heads(qa, NA_HEADS), to_heads(ka, NA_HEADS), to_heads(va, NA_HEADS)
    qb = apply_rope(rms_norm(to_heads(qb, GQA_HEADS), q_gain), cos, sin)
    kb = apply_rope(rms_norm(to_heads(kb, GQA_KV_HEADS), k_gain), cos, sin)
    vb = to_heads(vb, GQA_KV_HEADS)
    o_a = neighbourhood_attention(qa, ka, va, ka_c, va_c, rpb)
    o_b = gqa_attention(group_q(qb), jnp.concatenate([kb_c, kb], axis=2),
                        jnp.concatenate([vb_c, vb], axis=2))
    o_lat = from_heads(jnp.concatenate([o_a, o_b], axis=1)) @ w_out
    o_ctx = None
    if need_ctx:
        o_a_c = gqa_attention(to_heads(qa_c, NA_HEADS)[:, :, None], ka_c, va_c)
        qb_c = rms_norm(to_heads(qb_c, GQA_HEADS), q_gain)
        o_b_c = gqa_attention(group_q(qb_c), kb_c, vb_c)
        o_ctx = from_heads(jnp.concatenate([o_a_c, o_b_c], axis=1)) @ w_out
    return o_ctx, o_lat


def diff_mixer(h_ctx, h_lat, w_in, w_out, lq1, lk1, lq2, lk2, sub_gain, lam_init, cos, sin, need_ctx):
    f32 = jnp.float32
    lam = (jnp.exp(jnp.sum(lq1.astype(f32) * lk1.astype(f32)))
           - jnp.exp(jnp.sum(lq2.astype(f32) * lk2.astype(f32))) + lam_init)

    def project(hs):
        b, s, _ = hs.shape
        q, k, v = jnp.split(hs @ w_in, 3, axis=-1)
        q = q.reshape(b, s, DIFF_HEADS, 2, HEAD_DIM).transpose(0, 2, 3, 1, 4)
        k = k.reshape(b, s, DIFF_HEADS, 2, HEAD_DIM).transpose(0, 2, 3, 1, 4)
        return q, k, to_heads(v, DIFF_HEADS)

    def finish(o):
        return from_heads(rms_norm(o, sub_gain) * (1.0 - lam_init)) @ w_out

    q_c, k_c, v_c = project(h_ctx)
    q, k, v = project(h_lat)
    q, k = apply_rope(q, cos, sin), apply_rope(k, cos, sin)
    k_all = jnp.concatenate([k_c, k], axis=3)
    v_all = jnp.concatenate([v_c, v], axis=2)
    o_lat = finish(diff_attention(q, k_all, v_all, lam))
    o_ctx = finish(diff_attention(q_c, k_c, v_c, lam)) if need_ctx else None
    return o_ctx, o_lat


def moe_ffn(h, router_w, router_bias, w_gate, w_up, w_down, sh_gate, sh_up, sh_down):
    t, d = h.shape
    scores = jax.nn.sigmoid(h.astype(jnp.float32) @ router_w.astype(jnp.float32))
    _, idx = lax.top_k(scores + router_bias.astype(jnp.float32), TOP_K)
    s_sel = jnp.take_along_axis(scores, idx, axis=-1)
    gates = s_sel / jnp.sum(s_sel, axis=-1, keepdims=True) * ROUTED_SCALE
    a = t * TOP_K
    e_flat = idx.reshape(a).astype(jnp.int32)
    tok_flat = jnp.arange(a, dtype=jnp.int32) // TOP_K
    g_flat = gates.reshape(a)
    order = jnp.argsort(e_flat)
    e_sorted, tok_sorted, g_sorted = e_flat[order], tok_flat[order], g_flat[order]
    counts = jnp.bincount(e_flat, length=N_EXPERTS).astype(jnp.int32)
    starts = jnp.cumsum(counts) - counts
    padded = (counts + MOE_BLOCK - 1) // MOE_BLOCK * MOE_BLOCK
    pad_ends = jnp.cumsum(padded)
    pad_starts = pad_ends - padded
    dest = pad_starts[e_sorted] + jnp.arange(a, dtype=jnp.int32) - starts[e_sorted]
    n_blocks = (a + N_EXPERTS * (MOE_BLOCK - 1) + MOE_BLOCK - 1) // MOE_BLOCK
    n_pad = n_blocks * MOE_BLOCK
    buf_tok = jnp.full((n_pad,), t, dtype=jnp.int32).at[dest].set(tok_sorted)
    buf_gate = jnp.zeros((n_pad,), jnp.float32).at[dest].set(g_sorted)
    block_starts = jnp.arange(n_blocks, dtype=jnp.int32) * MOE_BLOCK
    blk_expert = jnp.minimum(jnp.searchsorted(pad_ends, block_starts, side='right'), N_EXPERTS - 1)
    h_pad = jnp.concatenate([h, jnp.zeros((1, d), h.dtype)], axis=0)

    def expert_block(args):
        tok, e = args
        xb = h_pad[tok]
        hid = jax.nn.silu(xb @ w_gate[e]) * (xb @ w_up[e])
        return hid @ w_down[e]

    y = lax.map(expert_block, (buf_tok.reshape(n_blocks, MOE_BLOCK), blk_expert))
    y = y.reshape(n_pad, d) * buf_gate[:, None].astype(y.dtype)
    routed = jax.ops.segment_sum(y, buf_tok, num_segments=t + 1)[:t]
    shared = (jax.nn.silu(h @ sh_gate) * (h @ sh_up)) @ sh_down
    return routed + shared


def setup_inputs(seed: int = 0) -> dict:
    key = jax.random.key(seed)
    keys = jax.random.split(key, 29)
    D, E, F = D_MODEL, N_EXPERTS, D_EXPERT

    def normal(i, shape, scale):
        return jax.random.normal(keys[i], shape, jnp.float32) * scale

    def gain(i, shape):
        return 1.0 + normal(i, shape, 0.05)

    return {
        'x': normal(0, (BATCH, SEQ, D), 1.0),
        'c': normal(1, (BATCH, D), 1.0),
        'ctx': normal(2, (BATCH, CTX_LEN, D), 1.0),
        'c_ctx': normal(3, (D,), 1.0),
        'ada_w': normal(4, (DEPTH, D, 6 * D), 0.5 * D ** -0.5),
        'ada_b': normal(5, (DEPTH, 6 * D), 0.02),
        'norm_mix': gain(6, (DEPTH, D)),
        'norm_ffn': gain(7, (DEPTH, D)),
        'ab_w_in': normal(8, (N_EVEN, D, AB_IN), D ** -0.5),
        'ab_w_out': normal(9, (N_EVEN, AB_OUT, D), AB_OUT ** -0.5),
        'na_rpb': normal(10, (N_EVEN, NA_HEADS, 2 * WIN_ROWS - 1, 2 * WIN_COLS - 1), 0.05),
        'gqa_q_gain': gain(11, (N_EVEN, HEAD_DIM)),
        'gqa_k_gain': gain(12, (N_EVEN, HEAD_DIM)),
        'diff_w_in': normal(13, (N_ODD, D, DIFF_IN), D ** -0.5),
        'diff_w_out': normal(14, (N_ODD, DIFF_W, D), DIFF_W ** -0.5),
        'diff_lq1': normal(15, (N_ODD, HEAD_DIM), 0.1),
        'diff_lk1': normal(16, (N_ODD, HEAD_DIM), 0.1),
        'diff_lq2': normal(17, (N_ODD, HEAD_DIM), 0.1),
        'diff_lk2': normal(18, (N_ODD, HEAD_DIM), 0.1),
        'diff_sub_gain': gain(19, (N_ODD, 2 * HEAD_DIM)),
        'router_w': normal(20, (DEPTH, D, E), D ** -0.5),
        'router_bias': normal(21, (DEPTH, E), 0.01),
        'expert_w_gate': normal(22, (DEPTH, E, D, F), D ** -0.5),
        'expert_w_up': normal(23, (DEPTH, E, D, F), D ** -0.5),
        'expert_w_down': normal(24, (DEPTH, E, F, D), F ** -0.5),
        'shared_w_gate': normal(25, (DEPTH, D, F), D ** -0.5),
        'shared_w_up': normal(26, (DEPTH, D, F), D ** -0.5),
        'shared_w_down': normal(27, (DEPTH, F, D), F ** -0.5),
        'final_norm': gain(28, (D,)),
    }


def reference(x, c, ctx, c_ctx, ada_w, ada_b, norm_mix, norm_ffn, ab_w_in, ab_w_out, na_rpb,
              gqa_q_gain, gqa_k_gain, diff_w_in, diff_w_out, diff_lq1, diff_lk1, diff_lq2, diff_lk2,
              diff_sub_gain, router_w, router_bias, expert_w_gate, expert_w_up, expert_w_down,
              shared_w_gate, shared_w_up, shared_w_down, final_norm):
    b, n, d = x.shape
    L = ctx.shape[1]
    cos, sin = axial_rope(n)
    silu_c = jax.nn.silu(c)
    silu_cc = jax.nn.silu(c_ctx)
    xc = ctx
    for l in range(DEPTH):
        last = l == DEPTH - 1
        i = l // 2
        mod = silu_c @ ada_w[l] + ada_b[l]
        mod_c = silu_cc @ ada_w[l] + ada_b[l]
        sh_m, sc_m, g_m, sh_f, sc_f, g_f = jnp.split(mod[:, None, :], 6, axis=-1)
        sh_mc, sc_mc, g_mc, sh_fc, sc_fc, g_fc = jnp.split(mod_c, 6, axis=-1)
        h = modulate(rms_norm(x, norm_mix[l]), sh_m, sc_m)
        hc = modulate(rms_norm(xc, norm_mix[l]), sh_mc, sc_mc)
        if l % 2 == 0:
            o_c, o = na_gqa_mixer(hc, h, ab_w_in[i], ab_w_out[i], na_rpb[i], gqa_q_gain[i],
                                  gqa_k_gain[i], cos, sin, not last)
        else:
            lam_init = 0.8 - 0.6 * math.exp(-0.3 * l)
            o_c, o = diff_mixer(hc, h, diff_w_in[i], diff_w_out[i], diff_lq1[i], diff_lk1[i],
                                diff_lq2[i], diff_lk2[i], diff_sub_gain[i], lam_init, cos, sin, not last)
        x = x + g_m * o
        moe_args = (router_w[l], router_bias[l], expert_w_gate[l], expert_w_up[l], expert_w_down[l],
                    shared_w_gate[l], shared_w_up[l], shared_w_down[l])
        if last:
            h = modulate(rms_norm(x, norm_ffn[l]), sh_f, sc_f)
            x = x + g_f * moe_ffn(h.reshape(b * n, d), *moe_args).reshape(b, n, d)
        else:
            xc = xc + g_mc * o_c
            h = modulate(rms_norm(x, norm_ffn[l]), sh_f, sc_f)
            hc = modulate(rms_norm(xc, norm_ffn[l]), sh_fc, sc_fc)
            tokens = jnp.concatenate([hc, h], axis=1).reshape(b * (L + n), d)
            y = moe_ffn(tokens, *moe_args).reshape(b, L + n, d)
            xc = xc + g_fc * y[:, :L]
            x = x + g_f * y[:, L:]
    return rms_norm(x, final_norm)
```

```python
import functools
import math

import jax
import jax.numpy as jnp
import numpy as np
from jax import lax
from jax.experimental import pallas as pl
from jax.experimental.pallas import tpu as pltpu

F32 = jnp.float32
BF16 = jnp.bfloat16

LANES = 128
HEAD_DIM = 64
GRID_W = 64
WIN_ROWS = 8
WIN_COLS = 16
ROPE_THETA = 10000.0
EPS = 1e-6
N_EXPERTS = 64
TOP_K = 8
ROUTED_SCALE = 2.5
NEG = -1e30
VMEM_LIMIT = 56 * 1024 * 1024

TM = 256
NA_ROWS = 8
NA_SPAN = 16


def _cparams(sem):
    return pltpu.CompilerParams(dimension_semantics=sem, vmem_limit_bytes=VMEM_LIMIT)


def _mods_kernel(c_ref, w_ref, b_ref, o_ref):
    c = c_ref[...]
    s = c * jax.nn.sigmoid(c)
    o_ref[0] = jnp.dot(s.astype(BF16), w_ref[0].astype(BF16), preferred_element_type=F32) + b_ref[0]


def _mods(cvec, ada_w, ada_b):
    depth, d, d6 = ada_w.shape
    tn = 1536
    return pl.pallas_call(
        _mods_kernel,
        grid=(depth, d6 // tn),
        in_specs=[pl.BlockSpec((8, d), lambda l, j: (0, 0)),
                  pl.BlockSpec((1, d, tn), lambda l, j: (l, 0, j)),
                  pl.BlockSpec((1, 1, tn), lambda l, j: (l, 0, j))],
        out_specs=pl.BlockSpec((1, 8, tn), lambda l, j: (l, 0, j)),
        out_shape=jax.ShapeDtypeStruct((depth, 8, d6), F32),
        compiler_params=_cparams(("arbitrary", "arbitrary")),
        name="adaln_mods",
    )(cvec, ada_w, ada_b.reshape(depth, 1, d6))


def _proj_kernel(x_ref, g_ref, sh_ref, sc_ref, w_ref, cos_ref, sin_ref, gm_ref, qg_ref, kg_ref,
                 o_ref, *, plan):
    x = x_ref[0]
    ms = jnp.mean(x * x, axis=-1, keepdims=True)
    h = x * lax.rsqrt(ms + EPS) * g_ref[...]
    h = h * (1.0 + sc_ref[0]) + sh_ref[0]
    y = jnp.dot(h.astype(BF16), w_ref[...], preferred_element_type=F32)
    cosf = cos_ref[...]
    sins = sin_ref[...]
    even = (lax.broadcasted_iota(jnp.int32, (1, LANES), 1) % 2) == 0
    for c, (norm, rope, scale) in enumerate(plan):
        yc = y[:, c * LANES:(c + 1) * LANES]
        if norm:
            ms2 = jnp.dot((yc * yc).astype(BF16), gm_ref[...], preferred_element_type=F32)
            gain = qg_ref[...] if norm == "q" else kg_ref[...]
            yc = yc * lax.rsqrt(ms2 + EPS) * gain
        if rope:
            sw = jnp.where(even, pltpu.roll(yc, LANES - 1, 1), pltpu.roll(yc, 1, 1))
            yc = yc * cosf + sw * sins
        if scale:
            yc = yc * (HEAD_DIM ** -0.5)
        o_ref[0, :, c * LANES:(c + 1) * LANES] = yc.astype(BF16)


def _proj(xa, gain, mods, sh_idx, sc_idx, w, cosf, sins, gm, qg, kg, plan, n_lat):
    b, s, d = xa.shape
    wcols = w.shape[1]
    nt = s // TM
    lat_tiles = n_lat // TM

    def mod_map(chunk):
        return lambda bi, i: (jnp.where(i >= lat_tiles, b, bi) * 6 + chunk, 0, 0)

    return pl.pallas_call(
        functools.partial(_proj_kernel, plan=plan),
        grid=(b, nt),
        in_specs=[pl.BlockSpec((1, TM, d), lambda bi, i: (bi, i, 0)),
                  pl.BlockSpec((1, d), lambda bi, i: (0, 0)),
                  pl.BlockSpec((1, 1, d), mod_map(sh_idx)),
                  pl.BlockSpec((1, 1, d), mod_map(sc_idx)),
                  pl.BlockSpec((d, wcols), lambda bi, i: (0, 0)),
                  pl.BlockSpec((TM, LANES), lambda bi, i: (i, 0)),
                  pl.BlockSpec((TM, LANES), lambda bi, i: (i, 0)),
                  pl.BlockSpec((LANES, LANES), lambda bi, i: (0, 0)),
                  pl.BlockSpec((1, LANES), lambda bi, i: (0, 0)),
                  pl.BlockSpec((1, LANES), lambda bi, i: (0, 0))],
        out_specs=pl.BlockSpec((1, TM, wcols), lambda bi, i: (bi, i, 0)),
        out_shape=jax.ShapeDtypeStruct((b, s, wcols), BF16),
        compiler_params=_cparams(("parallel", "arbitrary")),
        name="norm_inproj",
    )(xa, gain, mods, mods, w, cosf, sins, gm, qg, kg)


def _flash_kernel(tbl_ref, *refs, tq, tk, n_lat, n_ctx, mode, lam_init):
    del tbl_ref
    refs = list(refs)
    q_ref = refs.pop(0)
    if n_lat:
        k_ref = refs.pop(0)
        v_ref = refs.pop(0)
    kc_ref = refs.pop(0)
    vc_ref = refs.pop(0)
    if mode == "diff":
        lq1, lk1, lq2, lk2, sg_ref = refs[:5]
        refs = refs[5:]
    o_ref, qs_ref, m_ref, l_ref, acc_ref = refs

    lane = lax.broadcasted_iota(jnp.int32, (1, LANES), 1)
    lo = lane < HEAD_DIM
    q = q_ref[0]
    zero = jnp.zeros_like(q)
    qs_ref[0:tq, :] = jnp.where(lo, q, zero)
    qs_ref[tq:2 * tq, :] = jnp.where(lo, zero, q)
    m_ref[...] = jnp.full(m_ref.shape, NEG, F32)
    l_ref[...] = jnp.zeros(l_ref.shape, F32)
    acc_ref[...] = jnp.zeros(acc_ref.shape, F32)

    def chunk(kc, vc):
        s = lax.dot_general(qs_ref[...], kc, (((1,), (1,)), ((), ())), preferred_element_type=F32)
        nb = kc.shape[0] // LANES
        m_prev = m_ref[...]
        m_next = jnp.maximum(m_prev, jnp.max(s, axis=1, keepdims=True))
        alpha = jnp.exp(m_prev - m_next)
        p = jnp.exp(s - jnp.concatenate([m_next] * nb, axis=1))
        l_cur = p[:, 0:LANES]
        for j in range(1, nb):
            l_cur = l_cur + p[:, j * LANES:(j + 1) * LANES]
        l_ref[...] = alpha * l_ref[...] + l_cur
        acc_ref[...] = alpha * acc_ref[...] + jnp.dot(p.astype(BF16), vc, preferred_element_type=F32)
        m_ref[...] = m_next

    if n_lat:
        def body(j, carry):
            off = pl.multiple_of(j * tk, tk)
            chunk(k_ref[0, pl.ds(off, tk), :], v_ref[0, pl.ds(off, tk), :])
            return carry
        lax.fori_loop(0, n_lat // tk, body, 0)
    chunk(kc_ref[0], vc_ref[0])

    l = jnp.sum(l_ref[...], axis=1, keepdims=True)
    o = acc_ref[...] / l
    if mode == "pair":
        out = jnp.where(lo, o[0:tq], o[tq:2 * tq])
    else:
        lam = (jnp.exp(jnp.sum(lq1[...] * lk1[...], axis=1, keepdims=True))
               - jnp.exp(jnp.sum(lq2[...] * lk2[...], axis=1, keepdims=True)) + lam_init)
        dlt = o[0:tq] - lam * o[tq:2 * tq]
        ms = jnp.mean(dlt * dlt, axis=-1, keepdims=True)
        out = dlt * lax.rsqrt(ms + EPS) * sg_ref[...] * (1.0 - lam_init)
    o_ref[0] = out.astype(o_ref.dtype)


def _flash(qkv, tbl, *, n_q, q_row0, n_lat, n_ctx, ctx_row0, tq, tk, mode="pair", diff_params=None,
           lam_init=0.0):
    b = qkv.shape[0]
    ncols = tbl.shape[1]
    qb0 = q_row0 // tq
    cb0 = ctx_row0 // n_ctx
    in_specs = [pl.BlockSpec((1, tq, LANES), lambda bi, c, i, t: (bi, qb0 + i, t[0, c]))]
    args = [qkv]
    if n_lat:
        in_specs += [pl.BlockSpec((1, n_lat, LANES), lambda bi, c, i, t: (bi, 0, t[1, c])),
                     pl.BlockSpec((1, n_lat, LANES), lambda bi, c, i, t: (bi, 0, t[2, c]))]
        args += [qkv, qkv]
    in_specs += [pl.BlockSpec((1, n_ctx, LANES), lambda bi, c, i, t: (bi, cb0, t[1, c])),
                 pl.BlockSpec((1, n_ctx, LANES), lambda bi, c, i, t: (bi, cb0, t[2, c]))]
    args += [qkv, qkv]
    if mode == "diff":
        in_specs += [pl.BlockSpec((1, HEAD_DIM), lambda bi, c, i, t: (0, 0))] * 4
        in_specs += [pl.BlockSpec((1, LANES), lambda bi, c, i, t: (0, 0))]
        args += list(diff_params)
    grid_spec = pltpu.PrefetchScalarGridSpec(
        num_scalar_prefetch=1,
        grid=(b, ncols, n_q // tq),
        in_specs=in_specs,
        out_specs=pl.BlockSpec((1, tq, LANES), lambda bi, c, i, t: (bi, i, c)),
        scratch_shapes=[pltpu.VMEM((2 * tq, LANES), BF16),
                        pltpu.VMEM((2 * tq, LANES), F32),
                        pltpu.VMEM((2 * tq, LANES), F32),
                        pltpu.VMEM((2 * tq, LANES), F32)])
    return pl.pallas_call(
        functools.partial(_flash_kernel, tq=tq, tk=tk, n_lat=n_lat, n_ctx=n_ctx, mode=mode,
                          lam_init=lam_init),
        grid_spec=grid_spec,
        out_shape=jax.ShapeDtypeStruct((b, n_q, ncols * LANES), BF16),
        compiler_params=_cparams(("parallel", "parallel", "arbitrary")),
        name="flash_" + mode + ("_lat" if n_lat else "_ctx"),
    )(tbl, *args)


def _na_case_geometry(case, rows):
    r0 = {0: 0, 1: NA_ROWS, 2: rows - NA_ROWS}[case]
    start = min(max(r0 - WIN_ROWS // 2, 0), rows - NA_SPAN)
    return r0, start


def _na_tile_index(case, dr, dk, rows):
    r0, start = _na_case_geometry(case, rows)
    r, kr = r0 + dr, start + dk
    rs = min(max(r - WIN_ROWS // 2, 0), rows - WIN_ROWS)
    if rs <= kr < rs + WIN_ROWS:
        return kr - r + WIN_ROWS
    return 0


def _na_kernel(q_ref, k_ref, v_ref, kc_ref, vc_ref, tl_ref, tr_ref, o_ref, bias_ref, *, rows):
    nblk = rows // NA_ROWS
    bq = NA_ROWS * GRID_W
    bk = NA_SPAN * GRID_W
    for hh in range(2):
        for case in range(3):
            for dr in range(NA_ROWS):
                for dkp in range(NA_SPAN // 2):
                    ia = _na_tile_index(case, dr, 2 * dkp, rows)
                    ib = _na_tile_index(case, dr, 2 * dkp + 1, rows)
                    bias_ref[hh, case, dr * GRID_W:(dr + 1) * GRID_W, dkp * LANES:(dkp + 1) * LANES] = (
                        tl_ref[hh, ia] + tr_ref[hh, ib])

    lane = lax.broadcasted_iota(jnp.int32, (1, LANES), 1)
    lo = lane < HEAD_DIM
    kctx = kc_ref[0]
    vctx = vc_ref[0]
    nt = (((1,), (1,)), ((), ()))

    def body(i, carry):
        r0 = i * NA_ROWS
        start = jnp.clip(r0 - WIN_ROWS // 2, 0, rows - NA_SPAN)
        case = jnp.where(i == 0, 0, jnp.where(i == nblk - 1, 2, 1))
        qoff = pl.multiple_of(i * bq, bq)
        koff = pl.multiple_of(start * GRID_W, GRID_W)
        qb = q_ref[0, pl.ds(qoff, bq), :]
        ks = k_ref[0, pl.ds(koff, bk), :]
        vs = v_ref[0, pl.ds(koff, bk), :]
        zero = jnp.zeros_like(qb)
        outs = []
        for hh in range(2):
            qm = jnp.where(lo, qb, zero) if hh == 0 else jnp.where(lo, zero, qb)
            s_loc = lax.dot_general(qm, ks, nt, preferred_element_type=F32) + bias_ref[hh, case]
            s_ctx = lax.dot_general(qm, kctx, nt, preferred_element_type=F32)
            m = jnp.maximum(jnp.max(s_loc, axis=1, keepdims=True), jnp.max(s_ctx, axis=1, keepdims=True))
            p_loc = jnp.exp(s_loc - m)
            p_ctx = jnp.exp(s_ctx - m)
            l = jnp.sum(p_loc, axis=1, keepdims=True) + jnp.sum(p_ctx, axis=1, keepdims=True)
            o = (jnp.dot(p_loc.astype(BF16), vs, preferred_element_type=F32)
                 + jnp.dot(p_ctx.astype(BF16), vctx, preferred_element_type=F32))
            outs.append(o / l)
        o_ref[0, pl.ds(qoff, bq), :] = jnp.where(lo, outs[0], outs[1]).astype(o_ref.dtype)
        return carry

    lax.fori_loop(0, nblk, body, 0)


def _na(qkv, tl, tr, *, n_lat, n_ctx, npairs, qc0, kc0, vc0):
    b = qkv.shape[0]
    rows = n_lat // GRID_W
    cb0 = n_lat // n_ctx
    nt = tl.shape[1]
    return pl.pallas_call(
        functools.partial(_na_kernel, rows=rows),
        grid=(b, npairs),
        in_specs=[pl.BlockSpec((1, n_lat, LANES), lambda bi, j: (bi, 0, qc0 + j)),
                  pl.BlockSpec((1, n_lat, LANES), lambda bi, j: (bi, 0, kc0 + j)),
                  pl.BlockSpec((1, n_lat, LANES), lambda bi, j: (bi, 0, vc0 + j)),
                  pl.BlockSpec((1, n_ctx, LANES), lambda bi, j: (bi, cb0, kc0 + j)),
                  pl.BlockSpec((1, n_ctx, LANES), lambda bi, j: (bi, cb0, vc0 + j)),
                  pl.BlockSpec((2, nt, GRID_W, LANES), lambda bi, j: (j, 0, 0, 0)),
                  pl.BlockSpec((2, nt, GRID_W, LANES), lambda bi, j: (j, 0, 0, 0))],
        out_specs=pl.BlockSpec((1, n_lat, LANES), lambda bi, j: (bi, 0, j)),
        out_shape=jax.ShapeDtypeStruct((b, n_lat, npairs * LANES), BF16),
        scratch_shapes=[pltpu.VMEM((2, 3, NA_ROWS * GRID_W, NA_SPAN * GRID_W), F32)],
        compiler_params=_cparams(("parallel", "arbitrary")),
        name="neighbourhood_attn",
    )(qkv, qkv, qkv, qkv, qkv, tl, tr)


def _na_bias_tiles(rpb):
    h = rpb.shape[0]
    cols = np.arange(GRID_W)
    cs = np.clip(cols - WIN_COLS // 2, 0, GRID_W - WIN_COLS)
    kc = cols[None, :]
    valid = (kc >= cs[:, None]) & (kc < cs[:, None] + WIN_COLS)
    ci = np.clip(kc - cols[:, None] + WIN_COLS - 1, 0, 2 * WIN_COLS - 2)
    t = jnp.where(jnp.asarray(valid)[None, None], rpb[:, :, ci].astype(F32), NEG)
    t = jnp.concatenate([jnp.full((h, 1, GRID_W, GRID_W), NEG, F32), t], axis=1)
    z = jnp.zeros_like(t)
    return jnp.concatenate([t, z], axis=-1), jnp.concatenate([z, t], axis=-1)


def _outproj_kernel(*refs, n_parts, has_ctx, lat_tiles):
    refs = list(refs)
    parts = [refs.pop(0) for _ in range(n_parts)]
    octx_ref = refs.pop(0) if has_ctx else None
    w_ref, x_ref, gm_ref, ng_ref, sh_ref, sc_ref, xo_ref, h_ref, proj_ref = refs

    def lat():
        acc = None
        off = 0
        for p in parts:
            wdt = p.shape[-1]
            t = jnp.dot(p[0], w_ref[off:off + wdt, :], preferred_element_type=F32)
            acc = t if acc is None else acc + t
            off += wdt
        proj_ref[...] = acc

    if has_ctx:
        is_ctx = pl.program_id(1) >= lat_tiles
        pl.when(jnp.logical_not(is_ctx))(lat)

        @pl.when(is_ctx)
        def _():
            proj_ref[...] = jnp.dot(octx_ref[0], w_ref[...], preferred_element_type=F32)
    else:
        lat()

    x = x_ref[0] + gm_ref[0] * proj_ref[...]
    xo_ref[0] = x
    ms = jnp.mean(x * x, axis=-1, keepdims=True)
    h = x * lax.rsqrt(ms + EPS) * ng_ref[...]
    h_ref[0] = h * (1.0 + sc_ref[0]) + sh_ref[0]


def _outproj(parts, octx, w, xa, mods, gain, *, n_lat, n_rows):
    b, _, d = xa.shape
    lat_tiles = n_lat // TM
    nt = n_rows // TM
    has_ctx = octx is not None

    def mod_map(chunk):
        return lambda bi, i: (jnp.where(i >= lat_tiles, b, bi) * 6 + chunk, 0, 0)

    in_specs = [pl.BlockSpec((1, TM, p.shape[-1]), lambda bi, i: (bi, jnp.minimum(i, lat_tiles - 1), 0))
                for p in parts]
    args = list(parts)
    if has_ctx:
        in_specs.append(pl.BlockSpec((1, TM, d), lambda bi, i: (bi, 0, 0)))
        args.append(octx)
    in_specs += [pl.BlockSpec((d, d), lambda bi, i: (0, 0)),
                 pl.BlockSpec((1, TM, d), lambda bi, i: (bi, i, 0)),
                 pl.BlockSpec((1, 1, d), mod_map(2)),
                 pl.BlockSpec((1, d), lambda bi, i: (0, 0)),
                 pl.BlockSpec((1, 1, d), mod_map(3)),
                 pl.BlockSpec((1, 1, d), mod_map(4))]
    args += [w, xa, mods, gain, mods, mods]
    return pl.pallas_call(
        functools.partial(_outproj_kernel, n_parts=len(parts), has_ctx=has_ctx, lat_tiles=lat_tiles),
        grid=(b, nt),
        in_specs=in_specs,
        out_specs=[pl.BlockSpec((1, TM, d), lambda bi, i: (bi, i, 0)),
                   pl.BlockSpec((1, TM, d), lambda bi, i: (bi, i, 0))],
        out_shape=[jax.ShapeDtypeStruct((b, n_rows, d), F32),
                   jax.ShapeDtypeStruct((b, n_rows, d), F32)],
        scratch_shapes=[pltpu.VMEM((TM, d), F32)],
        compiler_params=_cparams(("parallel", "arbitrary")),
        name="outproj_ffnnorm",
    )(*args)


def _router_kernel(h_ref, rw_ref, rb_ref, gt_ref, g_ref):
    logits = lax.dot_general(rw_ref[...], h_ref[...], (((1,), (1,)), ((), ())),
                             preferred_element_type=F32, precision=lax.Precision.HIGHEST)
    scores = jax.nn.sigmoid(logits)
    work = scores + rb_ref[...]
    eidx = lax.broadcasted_iota(jnp.int32, work.shape, 0)
    sel = jnp.zeros(work.shape, jnp.bool_)
    for _ in range(TOP_K):
        mx = jnp.max(work, axis=0, keepdims=True)
        first = jnp.min(jnp.where(work == mx, eidx, N_EXPERTS), axis=0, keepdims=True)
        hit = eidx == first
        sel = jnp.logical_or(sel, hit)
        work = jnp.where(hit, -jnp.inf, work)
    s_sel = jnp.where(sel, scores, 0.0)
    gates = s_sel / jnp.sum(s_sel, axis=0, keepdims=True) * ROUTED_SCALE
    gt_ref[...] = gates
    g_ref[...] = gates.T


def _router(h2, rw_t, rb):
    t, d = h2.shape
    return pl.pallas_call(
        _router_kernel,
        grid=(t // TM,),
        in_specs=[pl.BlockSpec((TM, d), lambda i: (i, 0)),
                  pl.BlockSpec((N_EXPERTS, d), lambda i: (0, 0)),
                  pl.BlockSpec((N_EXPERTS, 1), lambda i: (0, 0))],
        out_specs=[pl.BlockSpec((N_EXPERTS, TM), lambda i: (0, i)),
                   pl.BlockSpec((TM, N_EXPERTS), lambda i: (i, 0))],
        out_shape=[jax.ShapeDtypeStruct((N_EXPERTS, t), F32),
                   jax.ShapeDtypeStruct((t, N_EXPERTS), F32)],
        compiler_params=_cparams(("parallel",)),
        name="router_topk",
    )(h2, rw_t, rb)


def _moe_dense_kernel(h_ref, g_ref, wg_ref, wu_ref, wd_ref, sg_ref, su_ref, sd_ref, y_ref, hb_ref, acc_ref):
    e = pl.program_id(1)

    @pl.when(e == 0)
    def _():
        hb = h_ref[...].astype(BF16)
        hb_ref[...] = hb
        hid = (jax.nn.silu(jnp.dot(hb, sg_ref[...], preferred_element_type=F32))
               * jnp.dot(hb, su_ref[...], preferred_element_type=F32))
        acc_ref[...] = jnp.dot(hid.astype(BF16), sd_ref[...], preferred_element_type=F32)

    hb = hb_ref[...]
    lane = lax.broadcasted_iota(jnp.int32, (1, N_EXPERTS), 1)
    gcol = jnp.sum(jnp.where(lane == e, g_ref[...], 0.0), axis=1, keepdims=True)
    hid = (jax.nn.silu(jnp.dot(hb, wg_ref[0], preferred_element_type=F32))
           * jnp.dot(hb, wu_ref[0], preferred_element_type=F32)) * gcol
    acc_ref[...] += jnp.dot(hid.astype(BF16), wd_ref[0], preferred_element_type=F32)

    @pl.when(e == N_EXPERTS - 1)
    def _():
        y_ref[...] = acc_ref[...]


def _moe_dense(h2, g, wg, wu, wd, sg, su, sd, tmo):
    t, d = h2.shape
    f = wg.shape[-1]
    return pl.pallas_call(
        _moe_dense_kernel,
        grid=(t // tmo, N_EXPERTS),
        in_specs=[pl.BlockSpec((tmo, d), lambda i, e: (i, 0)),
                  pl.BlockSpec((tmo, N_EXPERTS), lambda i, e: (i, 0)),
                  pl.BlockSpec((1, d, f), lambda i, e: (e, 0, 0)),
                  pl.BlockSpec((1, d, f), lambda i, e: (e, 0, 0)),
                  pl.BlockSpec((1, f, d), lambda i, e: (e, 0, 0)),
                  pl.BlockSpec((d, f), lambda i, e: (0, 0)),
                  pl.BlockSpec((d, f), lambda i, e: (0, 0)),
                  pl.BlockSpec((f, d), lambda i, e: (0, 0))],
        out_specs=pl.BlockSpec((tmo, d), lambda i, e: (i, 0)),
        out_shape=jax.ShapeDtypeStruct((t, d), F32),
        scratch_shapes=[pltpu.VMEM((tmo, d), BF16), pltpu.VMEM((tmo, d), F32)],
        compiler_params=_cparams(("parallel", "arbitrary")),
        name="moe_dense",
    )(h2, g, wg, wu, wd, sg, su, sd)


def _resid_kernel(x_ref, y_ref, gf_ref, fn_ref, o_ref, *, final):
    x = x_ref[0] + gf_ref[0] * y_ref[0]
    if final:
        ms = jnp.mean(x * x, axis=-1, keepdims=True)
        x = x * lax.rsqrt(ms + EPS) * fn_ref[...]
    o_ref[0] = x


def _resid(x, y, mods, fn, *, n_lat, final):
    b, n_rows, d = x.shape
    lat_tiles = n_lat // TM
    return pl.pallas_call(
        functools.partial(_resid_kernel, final=final),
        grid=(b, n_rows // TM),
        in_specs=[pl.BlockSpec((1, TM, d), lambda bi, i: (bi, i, 0)),
                  pl.BlockSpec((1, TM, d), lambda bi, i: (bi, i, 0)),
                  pl.BlockSpec((1, 1, d), lambda bi, i: (jnp.where(i >= lat_tiles, b, bi) * 6 + 5, 0, 0)),
                  pl.BlockSpec((1, d), lambda bi, i: (0, 0))],
        out_specs=pl.BlockSpec((1, TM, d), lambda bi, i: (bi, i, 0)),
        out_shape=jax.ShapeDtypeStruct((b, n_rows, d), F32),
        compiler_params=_cparams(("parallel", "arbitrary")),
        name="ffn_residual",
    )(x, y, mods, fn)


def _rope_tables(n_lat, n_ctx):
    t = np.arange(n_lat)
    row = (t // GRID_W).astype(np.float32)
    col = (t % GRID_W).astype(np.float32)
    npairs = HEAD_DIM // 4
    inv_freq = jnp.asarray(ROPE_THETA, F32) ** (-jnp.arange(npairs, dtype=F32) / npairs)
    ang = jnp.concatenate([jnp.asarray(row)[:, None] * inv_freq, jnp.asarray(col)[:, None] * inv_freq], axis=-1)
    cos = jnp.repeat(jnp.cos(ang), 2, axis=-1)
    sin = jnp.repeat(jnp.sin(ang), 2, axis=-1)
    sign = jnp.asarray(np.tile(np.array([-1.0, 1.0], np.float32), HEAD_DIM // 2))
    cosf = jnp.tile(cos, (1, LANES // HEAD_DIM))
    sins = jnp.tile(sin * sign, (1, LANES // HEAD_DIM))
    cosf = jnp.concatenate([cosf, jnp.ones((n_ctx, LANES), F32)], axis=0)
    sins = jnp.concatenate([sins, jnp.zeros((n_ctx, LANES), F32)], axis=0)
    return cosf, sins


def _moe_block(h2, x_new, mods_l, rw, rb, wg, wu, wd, sg, su, sd, fn, *, n_lat, final):
    b, n_rows, d = h2.shape
    t = b * n_rows
    hf = h2.reshape(t, d)
    _, g = _router(hf, rw.T, rb.reshape(N_EXPERTS, 1))
    tmo = t // 16
    y = _moe_dense(hf, g, wg.astype(BF16), wu.astype(BF16), wd.astype(BF16),
                   sg.astype(BF16), su.astype(BF16), sd.astype(BF16), tmo)
    return _resid(x_new, y.reshape(b, n_rows, d), mods_l, fn, n_lat=n_lat, final=final)


def kernel(x, c, ctx, c_ctx, ada_w, ada_b, norm_mix, norm_ffn, ab_w_in, ab_w_out, na_rpb, gqa_q_gain,
           gqa_k_gain, diff_w_in, diff_w_out, diff_lq1, diff_lk1, diff_lq2, diff_lk2, diff_sub_gain,
           router_w, router_bias, expert_w_gate, expert_w_up, expert_w_down, shared_w_gate, shared_w_up,
           shared_w_down, final_norm):
    b, n, d = x.shape
    n_ctx = ctx.shape[1]
    depth = ada_w.shape[0]
    assert depth == 2 and n_ctx == TM and n % (NA_ROWS * GRID_W) == 0 and d % LANES == 0

    cvec = jnp.concatenate([c, c_ctx[None], jnp.zeros((8 - b - 1, d), F32)], axis=0)
    mods = _mods(cvec, ada_w, ada_b)[:, :b + 1].reshape(depth, (b + 1) * 6, 1, d)
    cosf, sins = _rope_tables(n, n_ctx)
    gm = jnp.asarray(np.kron(np.eye(LANES // HEAD_DIM), np.full((HEAD_DIM, HEAD_DIM), 1.0 / HEAD_DIM)), BF16)
    ones = jnp.ones((1, LANES), F32)
    fn = final_norm.reshape(1, d)

    xa = jnp.concatenate([x, ctx], axis=1)

    w = ab_w_in[0]
    kb =[w[:, 2048 + HEAD_DIM * g: 2048 + HEAD_DIM * (g + 1)] for g in range(2)]
    vb = [w[:, 2176 + HEAD_DIM * g: 2176 + HEAD_DIM * (g + 1)] for g in range(2)]
    w0 = jnp.concatenate([w[:, :2048], kb[0], kb[0], kb[1], kb[1], vb[0], vb[0], vb[1], vb[1]],
                         axis=1).astype(BF16)
    plan0 = ([(None, False, True)] * 4 + [(None, False, False)] * 8 + [("q", True, True)] * 4
             + [("k", True, False)] * 2 + [(None, False, False)] * 2)
    qg = jnp.tile(gqa_q_gain[0].reshape(1, HEAD_DIM), (1, LANES // HEAD_DIM))
    kg = jnp.tile(gqa_k_gain[0].reshape(1, HEAD_DIM), (1, LANES // HEAD_DIM))
    qkv = _proj(xa, norm_mix[0].reshape(1, d), mods[0], 0, 1, w0, cosf, sins, gm, qg, kg, plan0, n)

    tl, tr = _na_bias_tiles(na_rpb[0])
    o_na = _na(qkv, tl, tr, n_lat=n, n_ctx=n_ctx, npairs=4, qc0=0, kc0=4, vc0=8)
    tbl_g = jnp.asarray([[12, 13, 14, 15], [16, 16, 17, 17], [18, 18, 19, 19]], jnp.int32)
    o_gqa = _flash(qkv, tbl_g, n_q=n, q_row0=0, n_lat=n, n_ctx=n_ctx, ctx_row0=n, tq=512, tk=512)
    tbl_c = jnp.asarray([[0, 1, 2, 3, 12, 13, 14, 15], [4, 5, 6, 7, 16, 16, 17, 17],
                         [8, 9, 10, 11, 18, 18, 19, 19]], jnp.int32)
    o_ctx = _flash(qkv, tbl_c, n_q=n_ctx, q_row0=n, n_lat=0, n_ctx=n_ctx, ctx_row0=n, tq=n_ctx, tk=n_ctx)

    x_new, h2 = _outproj([o_na, o_gqa], o_ctx, ab_w_out[0].astype(BF16), xa, mods[0],
                         norm_ffn[0].reshape(1, d), n_lat=n, n_rows=n + n_ctx)
    xa = _moe_block(h2, x_new, mods[0], router_w[0], router_bias[0], expert_w_gate[0], expert_w_up[0],
                    expert_w_down[0], shared_w_gate[0], shared_w_up[0], shared_w_down[0], fn,
                    n_lat=n, final=False)

    lam_init = 0.8 - 0.6 * math.exp(-0.3 * 1)
    plan1 = [(None, True, True)] * 8 + [(None, True, False)] * 8 + [(None, False, False)] * 8
    qkv = _proj(xa, norm_mix[1].reshape(1, d), mods[1], 0, 1, diff_w_in[0].astype(BF16), cosf, sins, gm,
                ones, ones, plan1, n)
    tbl_d = jnp.asarray([list(range(0, 8)), list(range(8, 16)), list(range(16, 24))], jnp.int32)
    dp = [diff_lq1[0].reshape(1, HEAD_DIM), diff_lk1[0].reshape(1, HEAD_DIM),
          diff_lq2[0].reshape(1, HEAD_DIM), diff_lk2[0].reshape(1, HEAD_DIM),
          diff_sub_gain[0].reshape(1, LANES)]
    o_diff = _flash(qkv, tbl_d, n_q=n, q_row0=0, n_lat=n, n_ctx=n_ctx, ctx_row0=n, tq=512, tk=512,
                    mode="diff", diff_params=dp, lam_init=lam_init)
    x_new, h2 = _outproj([o_diff], None, diff_w_out[0].astype(BF16), xa, mods[1],
                         norm_ffn[1].reshape(1, d), n_lat=n, n_rows=n)
    return _moe_block(h2, x_new, mods[1], router_w[1], router_bias[1], expert_w_gate[1], expert_w_up[1],
                      expert_w_down[1], shared_w_gate[1], shared_w_up[1], shared_w_down[1], fn,
                      n_lat=n, final=True)
```

```python
import functools
import math

import jax
import jax.numpy as jnp
import numpy as np
from jax import lax
from jax.experimental import pallas as pl
from jax.experimental.pallas import tpu as pltpu

F32 = jnp.float32
BF16 = jnp.bfloat16

LANES = 128
HEAD_DIM = 64
GRID_W = 64
WIN_ROWS = 8
WIN_COLS = 16
ROPE_THETA = 10000.0
EPS = 1e-6
N_EXPERTS = 64
TOP_K = 8
ROUTED_SCALE = 2.5
NEG = -1e30
LOG2E = math.log2(math.e)
Q_SCALE = HEAD_DIM ** -0.5 * LOG2E
VMEM_LIMIT = 56 * 1024 * 1024

TM = 256
NA_ROWS = 8
NA_SPAN = 16
FLASH_TQ = 512
FLASH_TK = 768


def _cparams(sem):
    return pltpu.CompilerParams(dimension_semantics=sem, vmem_limit_bytes=VMEM_LIMIT)


def _mods_kernel(c_ref, w_ref, b_ref, o_ref):
    c = c_ref[...]
    s = c * jax.nn.sigmoid(c)
    o_ref[0] = jnp.dot(s.astype(BF16), w_ref[0].astype(BF16), preferred_element_type=F32) + b_ref[0]


def _mods(cvec, ada_w, ada_b):
    depth, d, d6 = ada_w.shape
    tn = 1536
    return pl.pallas_call(
        _mods_kernel,
        grid=(depth, d6 // tn),
        in_specs=[pl.BlockSpec((8, d), lambda l, j: (0, 0)),
                  pl.BlockSpec((1, d, tn), lambda l, j: (l, 0, j)),
                  pl.BlockSpec((1, 1, tn), lambda l, j: (l, 0, j))],
        out_specs=pl.BlockSpec((1, 8, tn), lambda l, j: (l, 0, j)),
        out_shape=jax.ShapeDtypeStruct((depth, 8, d6), F32),
        compiler_params=_cparams(("arbitrary", "arbitrary")),
        name="adaln_mods",
    )(cvec, ada_w, ada_b.reshape(depth, 1, d6))


def _proj_kernel(x_ref, g_ref, sh_ref, sc_ref, w_ref, cos_ref, sin_ref, gm_ref, qg_ref, kg_ref,
                 o_ref, *, plan):
    x = x_ref[0]
    ms = jnp.mean(x * x, axis=-1, keepdims=True)
    h = x * lax.rsqrt(ms + EPS) * g_ref[...]
    h = h * (1.0 + sc_ref[0]) + sh_ref[0]
    y = jnp.dot(h.astype(BF16), w_ref[...], preferred_element_type=F32)
    cosf = cos_ref[...]
    sins = sin_ref[...]
    even = (lax.broadcasted_iota(jnp.int32, (1, LANES), 1) % 2) == 0
    for c, (norm, rope, scale) in enumerate(plan):
        yc = y[:, c * LANES:(c + 1) * LANES]
        if norm:
            ms2 = jnp.dot((yc * yc).astype(BF16), gm_ref[...], preferred_element_type=F32)
            gain = qg_ref[...] if norm == "q" else kg_ref[...]
            yc = yc * lax.rsqrt(ms2 + EPS) * gain
        if rope:
            sw = jnp.where(even, pltpu.roll(yc, LANES - 1, 1), pltpu.roll(yc, 1, 1))
            yc = yc * cosf + sw * sins
        if scale:
            yc = yc * Q_SCALE
        o_ref[0, :, c * LANES:(c + 1) * LANES] = yc.astype(BF16)


def _proj(xa, gain, mods, sh_idx, sc_idx, w, cosf, sins, gm, qg, kg, plan, n_lat):
    b, s, d = xa.shape
    wcols = w.shape[1]
    nt = s // TM
    lat_tiles = n_lat // TM

    def mod_map(chunk):
        return lambda bi, i: (jnp.where(i >= lat_tiles, b, bi) * 6 + chunk, 0, 0)

    return pl.pallas_call(
        functools.partial(_proj_kernel, plan=plan),
        grid=(b, nt),
        in_specs=[pl.BlockSpec((1, TM, d), lambda bi, i: (bi, i, 0)),
                  pl.BlockSpec((1, d), lambda bi, i: (0, 0)),
                  pl.BlockSpec((1, 1, d), mod_map(sh_idx)),
                  pl.BlockSpec((1, 1, d), mod_map(sc_idx)),
                  pl.BlockSpec((d, wcols), lambda bi, i: (0, 0)),
                  pl.BlockSpec((TM, LANES), lambda bi, i: (i, 0)),
                  pl.BlockSpec((TM, LANES), lambda bi, i: (i, 0)),
                  pl.BlockSpec((LANES, LANES), lambda bi, i: (0, 0)),
                  pl.BlockSpec((1, LANES), lambda bi, i: (0, 0)),
                  pl.BlockSpec((1, LANES), lambda bi, i: (0, 0))],
        out_specs=pl.BlockSpec((1, TM, wcols), lambda bi, i: (bi, i, 0)),
        out_shape=jax.ShapeDtypeStruct((b, s, wcols), BF16),
        compiler_params=_cparams(("parallel", "arbitrary")),
        name="norm_inproj",
    )(xa, gain, mods, mods, w, cosf, sins, gm, qg, kg)


def _flash_kernel(tbl_ref, *refs, tq, tk, n_keys, mode, lam_init):
    del tbl_ref
    refs = list(refs)
    q_ref, k_ref, v_ref = refs[:3]
    refs = refs[3:]
    if mode == "diff":
        lq1, lk1, lq2, lk2, sg_ref = refs[:5]
        refs = refs[5:]
    o_ref, qs_ref, s_ref, m_ref, acc_ref = refs

    lane = lax.broadcasted_iota(jnp.int32, (1, LANES), 1)
    lo = lane < HEAD_DIM
    q = q_ref[0]
    zero = jnp.zeros_like(q)
    qs_ref[0:tq, :] = jnp.where(lo, q, zero)
    qs_ref[tq:2 * tq, :] = jnp.where(lo, zero, q)
    m_ref[...] = jnp.full(m_ref.shape, NEG, F32)
    acc_ref[...] = jnp.zeros(acc_ref.shape, F32)
    nb = tk // LANES
    nchunks = n_keys // tk
    ones = jnp.ones((tk, LANES), BF16)

    def qk(slot, j):
        off = pl.multiple_of(j * tk, tk)
        s_ref[slot] = lax.dot_general(qs_ref[...], k_ref[0, pl.ds(off, tk), :], (((1,), (1,)), ((), ())),
                                      preferred_element_type=F32)

    def softmax_pv(slot, j):
        off = pl.multiple_of(j * tk, tk)
        s = s_ref[slot]
        m_prev = m_ref[...]
        m_next = jnp.maximum(m_prev, jnp.max(s, axis=1, keepdims=True))
        alpha = jnp.exp2(m_prev - m_next)
        p = jnp.exp2(s - jnp.concatenate([m_next] * nb, axis=1))
        v1 = jnp.concatenate([v_ref[0, pl.ds(off, tk), :], ones], axis=1)
        acc_ref[...] = (jnp.concatenate([alpha, alpha], axis=1) * acc_ref[...]
                        + jnp.dot(p.astype(BF16), v1, preferred_element_type=F32))
        m_ref[...] = m_next

    qk(0, 0)
    npairs = (nchunks - 1) // 2

    def body(jj, carry):
        j = 2 * jj
        qk(1, j + 1)
        softmax_pv(0, j)
        qk(0, j + 2)
        softmax_pv(1, j + 1)
        return carry

    lax.fori_loop(0, npairs, body, 0)
    if nchunks % 2 == 0:
        qk(1, nchunks - 1)
        softmax_pv(0, nchunks - 2)
        softmax_pv(1, nchunks - 1)
    else:
        softmax_pv(0, nchunks - 1)

    acc = acc_ref[...]
    o = acc[:, 0:LANES] / acc[:, LANES:2 * LANES]
    if mode == "pair":
        out = jnp.where(lo, o[0:tq], o[tq:2 * tq])
    else:
        lam = (jnp.exp(jnp.sum(lq1[...] * lk1[...], axis=1, keepdims=True))
               - jnp.exp(jnp.sum(lq2[...] * lk2[...], axis=1, keepdims=True)) + lam_init)
        dlt = o[0:tq] - lam * o[tq:2 * tq]
        ms = jnp.mean(dlt * dlt, axis=-1, keepdims=True)
        out = dlt * lax.rsqrt(ms + EPS) * sg_ref[...] * (1.0 - lam_init)
    o_ref[0] = out.astype(o_ref.dtype)


def _flash(qkv, tbl, *, n_q, q_row0, n_keys, key_row0, tq, tk, mode="pair", diff_params=None, lam_init=0.0):
    b = qkv.shape[0]
    ncols = tbl.shape[1]
    qb0 = q_row0 // tq
    kb0 = key_row0 // n_keys
    in_specs = [pl.BlockSpec((1, tq, LANES), lambda bi, c, i, t: (bi, qb0 + i, t[0, c])),
                pl.BlockSpec((1, n_keys, LANES), lambda bi, c, i, t: (bi, kb0, t[1, c])),
                pl.BlockSpec((1, n_keys, LANES), lambda bi, c, i, t: (bi, kb0, t[2, c]))]
    args = [qkv, qkv, qkv]
    if mode == "diff":
        in_specs += [pl.BlockSpec((1, HEAD_DIM), lambda bi, c, i, t: (0, 0))] * 4
        in_specs += [pl.BlockSpec((1, LANES), lambda bi, c, i, t: (0, 0))]
        args += list(diff_params)
    grid_spec = pltpu.PrefetchScalarGridSpec(
        num_scalar_prefetch=1,
        grid=(b, ncols, n_q // tq),
        in_specs=in_specs,
        out_specs=pl.BlockSpec((1, tq, LANES), lambda bi, c, i, t: (bi, i, c)),
        scratch_shapes=[pltpu.VMEM((2 * tq, LANES), BF16),
                        pltpu.VMEM((2, 2 * tq, tk), F32),
                        pltpu.VMEM((2 * tq, LANES), F32),
                        pltpu.VMEM((2 * tq, 2 * LANES), F32)])
    return pl.pallas_call(
        functools.partial(_flash_kernel, tq=tq, tk=tk, n_keys=n_keys, mode=mode, lam_init=lam_init),
        grid_spec=grid_spec,
        out_shape=jax.ShapeDtypeStruct((b, n_q, ncols * LANES), BF16),
        compiler_params=_cparams(("parallel", "parallel", "arbitrary")),
        name="flash_%s_%d" % (mode, n_keys),
    )(tbl, *args)


def _na_case_geometry(case, rows):
    r0 = {0: 0, 1: NA_ROWS, 2: rows - NA_ROWS}[case]
    start = min(max(r0 - WIN_ROWS // 2, 0), rows - NA_SPAN)
    return r0, start


def _na_tile_index(case, dr, dk, rows):
    r0, start = _na_case_geometry(case, rows)
    r, kr = r0 + dr, start + dk
    rs = min(max(r - WIN_ROWS // 2, 0), rows - WIN_ROWS)
    if rs <= kr < rs + WIN_ROWS:
        return kr - r + WIN_ROWS
    return 0


def _na_kernel(q_ref, k_ref, v_ref, kc_ref, vc_ref, tl_ref, tr_ref, o_ref, bias_ref, *, rows):
    nblk = rows // NA_ROWS
    bq = NA_ROWS * GRID_W
    bk = NA_SPAN * GRID_W
    for hh in range(2):
        for case in range(3):
            for dr in range(NA_ROWS):
                for dkp in range(NA_SPAN // 2):
                    ia = _na_tile_index(case, dr, 2 * dkp, rows)
                    ib = _na_tile_index(case, dr, 2 * dkp + 1, rows)
                    bias_ref[hh, case, dr * GRID_W:(dr + 1) * GRID_W, dkp * LANES:(dkp + 1) * LANES] = (
                        tl_ref[hh, ia] + tr_ref[hh, ib])

    lane = lax.broadcasted_iota(jnp.int32, (1, LANES), 1)
    lo = lane < HEAD_DIM
    kctx = kc_ref[0]
    vctx = vc_ref[0]
    nt = (((1,), (1,)), ((), ()))

    def body(i, carry):
        r0 = i * NA_ROWS
        start = jnp.clip(r0 - WIN_ROWS // 2, 0, rows - NA_SPAN)
        case = jnp.where(i == 0, 0, jnp.where(i == nblk - 1, 2, 1))
        qoff = pl.multiple_of(i * bq, bq)
        koff = pl.multiple_of(start * GRID_W, GRID_W)
        qb = q_ref[0, pl.ds(qoff, bq), :]
        ks = k_ref[0, pl.ds(koff, bk), :]
        vs = v_ref[0, pl.ds(koff, bk), :]
        zero = jnp.zeros_like(qb)
        outs = []
        for hh in range(2):
            qm = jnp.where(lo, qb, zero) if hh == 0 else jnp.where(lo, zero, qb)
            s_loc = lax.dot_general(qm, ks, nt, preferred_element_type=F32) + bias_ref[hh, case]
            s_ctx = lax.dot_general(qm, kctx, nt, preferred_element_type=F32)
            m = jnp.maximum(jnp.max(s_loc, axis=1, keepdims=True), jnp.max(s_ctx, axis=1, keepdims=True))
            p_loc = jnp.exp2(s_loc - m)
            p_ctx = jnp.exp2(s_ctx - m)
            l = jnp.sum(p_loc, axis=1, keepdims=True) + jnp.sum(p_ctx, axis=1, keepdims=True)
            o = (jnp.dot(p_loc.astype(BF16), vs, preferred_element_type=F32)
                 + jnp.dot(p_ctx.astype(BF16), vctx, preferred_element_type=F32))
            outs.append(o / l)
        o_ref[0, pl.ds(qoff, bq), :] = jnp.where(lo, outs[0], outs[1]).astype(o_ref.dtype)
        return carry

    lax.fori_loop(0, nblk, body, 0)


def _na(qkv, tl, tr, *, n_lat, n_ctx, npairs, qc0, kc0, vc0):
    b = qkv.shape[0]
    rows = n_lat // GRID_W
    cb0 = n_lat // n_ctx
    nt = tl.shape[1]
    return pl.pallas_call(
        functools.partial(_na_kernel, rows=rows),
        grid=(b, npairs),
        in_specs=[pl.BlockSpec((1, n_lat, LANES), lambda bi, j: (bi, 0, qc0 + j)),
                  pl.BlockSpec((1, n_lat, LANES), lambda bi, j: (bi, 0, kc0 + j)),
                  pl.BlockSpec((1, n_lat, LANES), lambda bi, j: (bi, 0, vc0 + j)),
                  pl.BlockSpec((1, n_ctx, LANES), lambda bi, j: (bi, cb0, kc0 + j)),
                  pl.BlockSpec((1, n_ctx, LANES), lambda bi, j: (bi, cb0, vc0 + j)),
                  pl.BlockSpec((2, nt, GRID_W, LANES), lambda bi, j: (j, 0, 0, 0)),
                  pl.BlockSpec((2, nt, GRID_W, LANES), lambda bi, j: (j, 0, 0, 0))],
        out_specs=pl.BlockSpec((1, n_lat, LANES), lambda bi, j: (bi, 0, j)),
        out_shape=jax.ShapeDtypeStruct((b, n_lat, npairs * LANES), BF16),
        scratch_shapes=[pltpu.VMEM((2, 3, NA_ROWS * GRID_W, NA_SPAN * GRID_W), F32)],
        compiler_params=_cparams(("parallel", "arbitrary")),
        name="neighbourhood_attn",
    )(qkv, qkv, qkv, qkv, qkv, tl, tr)


def _na_bias_tiles(rpb):
    h = rpb.shape[0]
    cols = np.arange(GRID_W)
    cs = np.clip(cols - WIN_COLS // 2, 0, GRID_W - WIN_COLS)
    kc = cols[None, :]
    valid = (kc >= cs[:, None]) & (kc < cs[:, None] + WIN_COLS)
    ci = np.clip(kc - cols[:, None] + WIN_COLS - 1, 0, 2 * WIN_COLS - 2)
    t = jnp.where(jnp.asarray(valid)[None, None], rpb[:, :, ci].astype(F32) * LOG2E, NEG)
    t = jnp.concatenate([jnp.full((h, 1, GRID_W, GRID_W), NEG, F32), t], axis=1)
    z = jnp.zeros_like(t)
    return jnp.concatenate([t, z], axis=-1), jnp.concatenate([z, t], axis=-1)


def _outproj_kernel(*refs, n_parts, has_ctx, lat_tiles):
    refs = list(refs)
    parts = [refs.pop(0) for _ in range(n_parts)]
    octx_ref = refs.pop(0) if has_ctx else None
    w_ref, x_ref, gm_ref, ng_ref, sh_ref, sc_ref, xo_ref, h_ref, proj_ref = refs

    def lat():
        acc = None
        off = 0
        for p in parts:
            wdt = p.shape[-1]
            t = jnp.dot(p[0], w_ref[off:off + wdt, :], preferred_element_type=F32)
            acc = t if acc is None else acc + t
            off += wdt
        proj_ref[...] = acc

    if has_ctx:
        is_ctx = pl.program_id(1) >= lat_tiles
        pl.when(jnp.logical_not(is_ctx))(lat)

        @pl.when(is_ctx)
        def _():
            proj_ref[...] = jnp.dot(octx_ref[0], w_ref[...], preferred_element_type=F32)
    else:
        lat()

    x = x_ref[0] + gm_ref[0] * proj_ref[...]
    xo_ref[0] = x
    ms = jnp.mean(x * x, axis=-1, keepdims=True)
    h = x * lax.rsqrt(ms + EPS) * ng_ref[...]
    h_ref[0] = h * (1.0 + sc_ref[0]) + sh_ref[0]


def _outproj(parts, octx, w, xa, mods, gain, *, n_lat, n_rows):
    b, _, d = xa.shape
    lat_tiles = n_lat // TM
    nt = n_rows // TM
    has_ctx = octx is not None

    def mod_map(chunk):
        return lambda bi, i: (jnp.where(i >= lat_tiles, b, bi) * 6 + chunk, 0, 0)

    in_specs = [pl.BlockSpec((1, TM, p.shape[-1]), lambda bi, i: (bi, jnp.minimum(i, lat_tiles - 1), 0))
                for p in parts]
    args = list(parts)
    if has_ctx:
        in_specs.append(pl.BlockSpec((1, TM, d), lambda bi, i: (bi, 0, 0)))
        args.append(octx)
    in_specs += [pl.BlockSpec((d, d), lambda bi, i: (0, 0)),
                 pl.BlockSpec((1, TM, d), lambda bi, i: (bi, i, 0)),
                 pl.BlockSpec((1, 1, d), mod_map(2)),
                 pl.BlockSpec((1, d), lambda bi, i: (0, 0)),
                 pl.BlockSpec((1, 1, d), mod_map(3)),
                 pl.BlockSpec((1, 1, d), mod_map(4))]
    args += [w, xa, mods, gain, mods, mods]
    return pl.pallas_call(
        functools.partial(_outproj_kernel, n_parts=len(parts), has_ctx=has_ctx, lat_tiles=lat_tiles),
        grid=(b, nt),
        in_specs=in_specs,
        out_specs=[pl.BlockSpec((1, TM, d), lambda bi, i: (bi, i, 0)),
                   pl.BlockSpec((1, TM, d), lambda bi, i: (bi, i, 0))],
        out_shape=[jax.ShapeDtypeStruct((b, n_rows, d), F32),
                   jax.ShapeDtypeStruct((b, n_rows, d), F32)],
        scratch_shapes=[pltpu.VMEM((TM, d), F32)],
        compiler_params=_cparams(("parallel", "arbitrary")),
        name="outproj_ffnnorm",
    )(*args)


def _router_kernel(h_ref, rw_ref, rb_ref, gt_ref, g_ref):
    logits = lax.dot_general(rw_ref[...], h_ref[...], (((1,), (1,)), ((), ())),
                             preferred_element_type=F32, precision=lax.Precision.HIGHEST)
    scores = jax.nn.sigmoid(logits)
    work = scores + rb_ref[...]
    eidx = lax.broadcasted_iota(jnp.int32, work.shape, 0)
    sel = jnp.zeros(work.shape, jnp.bool_)
    for _ in range(TOP_K):
        mx = jnp.max(work, axis=0, keepdims=True)
        first = jnp.min(jnp.where(work == mx, eidx, N_EXPERTS), axis=0, keepdims=True)
        hit = eidx == first
        sel = jnp.logical_or(sel, hit)
        work = jnp.where(hit, -jnp.inf, work)
    s_sel = jnp.where(sel, scores, 0.0)
    gates = s_sel / jnp.sum(s_sel, axis=0, keepdims=True) * ROUTED_SCALE
    gt_ref[...] = gates
    g_ref[...] = gates.T


def _router(h2, rw_t, rb):
    t, d = h2.shape
    return pl.pallas_call(
        _router_kernel,
        grid=(t // TM,),
        in_specs=[pl.BlockSpec((TM, d), lambda i: (i, 0)),
                  pl.BlockSpec((N_EXPERTS, d), lambda i: (0, 0)),
                  pl.BlockSpec((N_EXPERTS, 1), lambda i: (0, 0))],
        out_specs=[pl.BlockSpec((N_EXPERTS, TM), lambda i: (0, i)),
                   pl.BlockSpec((TM, N_EXPERTS), lambda i: (i, 0))],
        out_shape=[jax.ShapeDtypeStruct((N_EXPERTS, t), F32),
                   jax.ShapeDtypeStruct((t, N_EXPERTS), F32)],
        compiler_params=_cparams(("parallel",)),
        name="router_topk",
    )(h2, rw_t, rb)


def _moe_dense_kernel(h_ref, g_ref, wg_ref, wu_ref, wd_ref, sg_ref, su_ref, sd_ref, y_ref, hb_ref, acc_ref):
    e = pl.program_id(1)

    @pl.when(e == 0)
    def _():
        hb = h_ref[...].astype(BF16)
        hb_ref[...] = hb
        hid = (jax.nn.silu(jnp.dot(hb, sg_ref[...], preferred_element_type=F32))
               * jnp.dot(hb, su_ref[...], preferred_element_type=F32))
        acc_ref[...] = jnp.dot(hid.astype(BF16), sd_ref[...], preferred_element_type=F32)

    hb = hb_ref[...]
    lane = lax.broadcasted_iota(jnp.int32, (1, N_EXPERTS), 1)
    gcol = jnp.sum(jnp.where(lane == e, g_ref[...], 0.0), axis=1, keepdims=True)
    hid = (jax.nn.silu(jnp.dot(hb, wg_ref[0], preferred_element_type=F32))
           * jnp.dot(hb, wu_ref[0], preferred_element_type=F32)) * gcol
    acc_ref[...] += jnp.dot(hid.astype(BF16), wd_ref[0], preferred_element_type=F32)

    @pl.when(e == N_EXPERTS - 1)
    def _():
        y_ref[...] = acc_ref[...]


def _moe_dense(h2, g, wg, wu, wd, sg, su, sd, tmo):
    t, d = h2.shape
    f = wg.shape[-1]
    return pl.pallas_call(
        _moe_dense_kernel,
        grid=(t // tmo, N_EXPERTS),
        in_specs=[pl.BlockSpec((tmo, d), lambda i, e: (i, 0)),
                  pl.BlockSpec((tmo, N_EXPERTS), lambda i, e: (i, 0)),
                  pl.BlockSpec((1, d, f), lambda i, e: (e, 0, 0)),
                  pl.BlockSpec((1, d, f), lambda i, e: (e, 0, 0)),
                  pl.BlockSpec((1, f, d), lambda i, e: (e, 0, 0)),
                  pl.BlockSpec((d, f), lambda i, e: (0, 0)),
                  pl.BlockSpec((d, f), lambda i, e: (0, 0)),
                  pl.BlockSpec((f, d), lambda i, e: (0, 0))],
        out_specs=pl.BlockSpec((tmo, d), lambda i, e: (i, 0)),
        out_shape=jax.ShapeDtypeStruct((t, d), F32),
        scratch_shapes=[pltpu.VMEM((tmo, d), BF16), pltpu.VMEM((tmo, d), F32)],
        compiler_params=_cparams(("parallel", "arbitrary")),
        name="moe_dense",
    )(h2, g, wg, wu, wd, sg, su, sd)


def _resid_kernel(x_ref, y_ref, gf_ref, fn_ref, o_ref, *, final):
    x = x_ref[0] + gf_ref[0] * y_ref[0]
    if final:
        ms = jnp.mean(x * x, axis=-1, keepdims=True)
        x = x * lax.rsqrt(ms + EPS) * fn_ref[...]
    o_ref[0] = x


def _resid(x, y, mods, fn, *, n_lat, final):
    b, n_rows, d = x.shape
    lat_tiles = n_lat // TM
    return pl.pallas_call(
        functools.partial(_resid_kernel, final=final),
        grid=(b, n_rows // TM),
        in_specs=[pl.BlockSpec((1, TM, d), lambda bi, i: (bi, i, 0)),
                  pl.BlockSpec((1, TM, d), lambda bi, i: (bi, i, 0)),
                  pl.BlockSpec((1, 1, d), lambda bi, i: (jnp.where(i >= lat_tiles, b, bi) * 6 + 5, 0, 0)),
                  pl.BlockSpec((1, d), lambda bi, i: (0, 0))],
        out_specs=pl.BlockSpec((1, TM, d), lambda bi, i: (bi, i, 0)),
        out_shape=jax.ShapeDtypeStruct((b, n_rows, d), F32),
        compiler_params=_cparams(("parallel", "arbitrary")),
        name="ffn_residual",
    )(x, y, mods, fn)


def _rope_tables(n_lat, n_ctx):
    t = np.arange(n_lat)
    row = (t // GRID_W).astype(np.float32)
    col = (t % GRID_W).astype(np.float32)
    npairs = HEAD_DIM // 4
    inv_freq = jnp.asarray(ROPE_THETA, F32) ** (-jnp.arange(npairs, dtype=F32) / npairs)
    ang = jnp.concatenate([jnp.asarray(row)[:, None] * inv_freq, jnp.asarray(col)[:, None] * inv_freq], axis=-1)
    cos = jnp.repeat(jnp.cos(ang), 2, axis=-1)
    sin = jnp.repeat(jnp.sin(ang), 2, axis=-1)
    sign = jnp.asarray(np.tile(np.array([-1.0, 1.0], np.float32), HEAD_DIM // 2))
    cosf = jnp.tile(cos, (1, LANES // HEAD_DIM))
    sins = jnp.tile(sin * sign, (1, LANES // HEAD_DIM))
    cosf = jnp.concatenate([cosf, jnp.ones((n_ctx, LANES), F32)], axis=0)
    sins = jnp.concatenate([sins, jnp.zeros((n_ctx, LANES), F32)], axis=0)
    return cosf, sins


def _moe_block(h2, x_new, mods_l, rw, rb, wg, wu, wd, sg, su, sd, fn, *, n_lat, final):
    b, n_rows, d = h2.shape
    t = b * n_rows
    hf = h2.reshape(t, d)
    _, g = _router(hf, rw.T, rb.reshape(N_EXPERTS, 1))
    tmo = t // 16
    y = _moe_dense(hf, g, wg.astype(BF16), wu.astype(BF16), wd.astype(BF16),
                   sg.astype(BF16), su.astype(BF16), sd.astype(BF16), tmo)
    return _resid(x_new, y.reshape(b, n_rows, d), mods_l, fn, n_lat=n_lat, final=final)


def kernel(x, c, ctx, c_ctx, ada_w, ada_b, norm_mix, norm_ffn, ab_w_in, ab_w_out, na_rpb, gqa_q_gain,
           gqa_k_gain, diff_w_in, diff_w_out, diff_lq1, diff_lk1, diff_lq2, diff_lk2, diff_sub_gain,
           router_w, router_bias, expert_w_gate, expert_w_up, expert_w_down, shared_w_gate, shared_w_up,
           shared_w_down, final_norm):
    b, n, d = x.shape
    n_ctx = ctx.shape[1]
    depth = ada_w.shape[0]
    assert depth == 2 and n_ctx == TM and n % (NA_ROWS * GRID_W) == 0 and d % LANES == 0
    assert (n + n_ctx) % FLASH_TK == 0 and n % FLASH_TQ == 0

    cvec = jnp.concatenate([c, c_ctx[None], jnp.zeros((8 - b - 1, d), F32)], axis=0)
    mods = _mods(cvec, ada_w, ada_b)[:, :b + 1].reshape(depth, (b + 1) * 6, 1, d)
    cosf, sins = _rope_tables(n, n_ctx)
    gm = jnp.asarray(np.kron(np.eye(LANES // HEAD_DIM), np.full((HEAD_DIM, HEAD_DIM), 1.0 / HEAD_DIM)), BF16)
    ones = jnp.ones((1, LANES), F32)
    fn = final_norm.reshape(1, d)

    xa = jnp.concatenate([x, ctx], axis=1)

    w = ab_w_in[0]
    kb = [w[:, 2048 + HEAD_DIM * g: 2048 + HEAD_DIM * (g + 1)] for g in range(2)]
    vb = [w[:, 2176 + HEAD_DIM * g: 2176 + HEAD_DIM * (g + 1)] for g in range(2)]
    w0 = jnp.concatenate([w[:, :2048], kb[0], kb[0], kb[1], kb[1], vb[0], vb[0], vb[1], vb[1]],
                         axis=1).astype(BF16)
    plan0 = ([(None, False, True)] * 4 + [(None, False, False)] * 8 + [("q", True, True)] * 4
             + [("k", True, False)] * 2 + [(None, False, False)] * 2)
    qg = jnp.tile(gqa_q_gain[0].reshape(1, HEAD_DIM), (1, LANES // HEAD_DIM))
    kg = jnp.tile(gqa_k_gain[0].reshape(1, HEAD_DIM), (1, LANES // HEAD_DIM))
    qkv = _proj(xa, norm_mix[0].reshape(1, d), mods[0], 0, 1, w0, cosf, sins, gm, qg, kg, plan0, n)

    tl, tr = _na_bias_tiles(na_rpb[0])
    o_na = _na(qkv, tl, tr, n_lat=n, n_ctx=n_ctx, npairs=4, qc0=0, kc0=4, vc0=8)
    tbl_g = jnp.asarray([[12, 13, 14, 15], [16, 16, 17, 17], [18, 18, 19, 19]], jnp.int32)
    o_gqa = _flash(qkv, tbl_g, n_q=n, q_row0=0, n_keys=n + n_ctx, key_row0=0, tq=FLASH_TQ, tk=FLASH_TK)
    tbl_c = jnp.asarray([[0, 1, 2, 3, 12, 13, 14, 15], [4, 5, 6, 7, 16, 16, 17, 17],
                         [8, 9, 10, 11, 18, 18, 19, 19]], jnp.int32)
    o_ctx = _flash(qkv, tbl_c, n_q=n_ctx, q_row0=n, n_keys=n_ctx, key_row0=n, tq=n_ctx, tk=n_ctx)

    x_new, h2 = _outproj([o_na, o_gqa], o_ctx, ab_w_out[0].astype(BF16), xa, mods[0],
                         norm_ffn[0].reshape(1, d), n_lat=n, n_rows=n + n_ctx)
    xa = _moe_block(h2, x_new, mods[0], router_w[0], router_bias[0], expert_w_gate[0], expert_w_up[0],
                    expert_w_down[0], shared_w_gate[0], shared_w_up[0], shared_w_down[0], fn,
                    n_lat=n, final=False)

    lam_init = 0.8 - 0.6 * math.exp(-0.3 * 1)
    plan1 = [(None, True, True)] * 8 + [(None, True, False)] * 8 + [(None, False, False)] * 8
    qkv = _proj(xa, norm_mix[1].reshape(1, d), mods[1], 0, 1, diff_w_in[0].astype(BF16), cosf, sins, gm,
                ones, ones, plan1, n)
    tbl_d = jnp.asarray([list(range(0, 8)), list(range(8, 16)), list(range(16, 24))], jnp.int32)
    dp = [diff_lq1[0].reshape(1, HEAD_DIM), diff_lk1[0].reshape(1, HEAD_DIM),
          diff_lq2[0].reshape(1, HEAD_DIM), diff_lk2[0].reshape(1, HEAD_DIM),
          diff_sub_gain[0].reshape(1, LANES)]
    o_diff = _flash(qkv, tbl_d, n_q=n, q_row0=0, n_keys=n + n_ctx, key_row0=0, tq=FLASH_TQ, tk=FLASH_TK,
                    mode="diff", diff_params=dp, lam_init=lam_init)
    x_new, h2 = _outproj([o_diff], None, diff_w_out[0].astype(BF16), xa, mods[1],
                         norm_ffn[1].reshape(1, d), n_lat=n, n_rows=n)
    return _moe_block(h2, x_new, mods[1], router_w[1], router_bias[1], expert_w_gate[1], expert_w_up[1],
                      expert_w_down[1], shared_w_gate[1], shared_w_up[1], shared_w_down[1], fn,
                      n_lat=n, final=True)
```

```python
import functools
import math

import jax
import jax.numpy as jnp
import numpy as np
from jax import lax
from jax.experimental import pallas as pl
from jax.experimental.pallas import tpu as pltpu

F32 = jnp.float32
BF16 = jnp.bfloat16
U32 = jnp.uint32

LANES = 128
HEAD_DIM = 64
GRID_W = 64
WIN_ROWS = 8
WIN_COLS = 16
ROPE_THETA = 10000.0
EPS = 1e-6
N_EXPERTS = 64
TOP_K = 8
ROUTED_SCALE = 2.5
NEG = -1e30
LOG2E = math.log2(math.e)
Q_SCALE = HEAD_DIM ** -0.5 * LOG2E
VMEM_LIMIT = 56 * 1024 * 1024

TM = 256
NA_ROWS = 8
NA_SPAN = 16
FLASH_TQ = 512
FLASH_TK = 768
MOE_BLOCK = 256
MOE_TT = 128
DMA_BATCH = 2
HI_MASK = 0xFFFF0000


def _cparams(sem):
    return pltpu.CompilerParams(dimension_semantics=sem, vmem_limit_bytes=VMEM_LIMIT)


def _pack_bf16_pairs(x):
    w = x.shape[1] // 2
    bits = lax.bitcast_convert_type(x.astype(BF16).astype(F32), U32)
    return (bits[:, w:] & jnp.uint32(HI_MASK)) | (bits[:, :w] >> 16)


def _unpack_bf16_pairs(p):
    lo = lax.bitcast_convert_type(p << 16, F32)
    hi = lax.bitcast_convert_type(p & jnp.uint32(HI_MASK), F32)
    return lo, hi


def _mods_kernel(c_ref, w_ref, b_ref, o_ref):
    c = c_ref[...]
    s = c * jax.nn.sigmoid(c)
    o_ref[0] = jnp.dot(s.astype(BF16), w_ref[0].astype(BF16), preferred_element_type=F32) + b_ref[0]


def _mods(cvec, ada_w, ada_b):
    depth, d, d6 = ada_w.shape
    tn = 1536
    return pl.pallas_call(
        _mods_kernel,
        grid=(depth, d6 // tn),
        in_specs=[pl.BlockSpec((8, d), lambda l, j: (0, 0)),
                  pl.BlockSpec((1, d, tn), lambda l, j: (l, 0, j)),
                  pl.BlockSpec((1, 1, tn), lambda l, j: (l, 0, j))],
        out_specs=pl.BlockSpec((1, 8, tn), lambda l, j: (l, 0, j)),
        out_shape=jax.ShapeDtypeStruct((depth, 8, d6), F32),
        compiler_params=_cparams(("arbitrary", "arbitrary")),
        name="adaln_mods",
    )(cvec, ada_w, ada_b.reshape(depth, 1, d6))


def _proj_kernel(x_ref, g_ref, sh_ref, sc_ref, w_ref, cos_ref, sin_ref, gm_ref, qg_ref, kg_ref,
                 o_ref, *, plan):
    x = x_ref[0]
    ms = jnp.mean(x * x, axis=-1, keepdims=True)
    h = x * lax.rsqrt(ms + EPS) * g_ref[...]
    h = h * (1.0 + sc_ref[0]) + sh_ref[0]
    y = jnp.dot(h.astype(BF16), w_ref[...], preferred_element_type=F32)
    cosf = cos_ref[...]
    sins = sin_ref[...]
    even = (lax.broadcasted_iota(jnp.int32, (1, LANES), 1) % 2) == 0
    for c, (norm, rope, scale) in enumerate(plan):
        yc = y[:, c * LANES:(c + 1) * LANES]
        if norm:
            ms2 = jnp.dot((yc * yc).astype(BF16), gm_ref[...], preferred_element_type=F32)
            gain = qg_ref[...] if norm == "q" else kg_ref[...]
            yc = yc * lax.rsqrt(ms2 + EPS) * gain
        if rope:
            sw = jnp.where(even, pltpu.roll(yc, LANES - 1, 1), pltpu.roll(yc, 1, 1))
            yc = yc * cosf + sw * sins
        if scale:
            yc = yc * Q_SCALE
        o_ref[0, :, c * LANES:(c + 1) * LANES] = yc.astype(BF16)


def _proj(xa, gain, mods, sh_idx, sc_idx, w, cosf, sins, gm, qg, kg, plan, n_lat):
    b, s, d = xa.shape
    wcols = w.shape[1]
    nt = s // TM
    lat_tiles = n_lat // TM

    def mod_map(chunk):
        return lambda bi, i: (jnp.where(i >= lat_tiles, b, bi) * 6 + chunk, 0, 0)

    return pl.pallas_call(
        functools.partial(_proj_kernel, plan=plan),
        grid=(b, nt),
        in_specs=[pl.BlockSpec((1, TM, d), lambda bi, i: (bi, i, 0)),
                  pl.BlockSpec((1, d), lambda bi, i: (0, 0)),
                  pl.BlockSpec((1, 1, d), mod_map(sh_idx)),
                  pl.BlockSpec((1, 1, d), mod_map(sc_idx)),
                  pl.BlockSpec((d, wcols), lambda bi, i: (0, 0)),
                  pl.BlockSpec((TM, LANES), lambda bi, i: (i, 0)),
                  pl.BlockSpec((TM, LANES), lambda bi, i: (i, 0)),
                  pl.BlockSpec((LANES, LANES), lambda bi, i: (0, 0)),
                  pl.BlockSpec((1, LANES), lambda bi, i: (0, 0)),
                  pl.BlockSpec((1, LANES), lambda bi, i: (0, 0))],
        out_specs=pl.BlockSpec((1, TM, wcols), lambda bi, i: (bi, i, 0)),
        out_shape=jax.ShapeDtypeStruct((b, s, wcols), BF16),
        compiler_params=_cparams(("parallel", "arbitrary")),
        name="norm_inproj",
    )(xa, gain, mods, mods, w, cosf, sins, gm, qg, kg)


def _flash_kernel(tbl_ref, *refs, tq, tk, n_keys, mode, lam_init):
    del tbl_ref
    refs = list(refs)
    q_ref, k_ref, v_ref = refs[:3]
    refs = refs[3:]
    if mode == "diff":
        lq1, lk1, lq2, lk2, sg_ref = refs[:5]
        refs = refs[5:]
    o_ref, qs_ref, s_ref, m_ref, acc_ref = refs

    lane = lax.broadcasted_iota(jnp.int32, (1, LANES), 1)
    lo = lane < HEAD_DIM
    q = q_ref[0]
    zero = jnp.zeros_like(q)
    qs_ref[0:tq, :] = jnp.where(lo, q, zero)
    qs_ref[tq:2 * tq, :] = jnp.where(lo, zero, q)
    m_ref[...] = jnp.full(m_ref.shape, NEG, F32)
    acc_ref[...] = jnp.zeros(acc_ref.shape, F32)
    nb = tk // LANES
    nchunks = n_keys // tk
    ones = jnp.ones((tk, LANES), BF16)

    def qk(slot, j):
        off = pl.multiple_of(j * tk, tk)
        s_ref[slot] = lax.dot_general(qs_ref[...], k_ref[0, pl.ds(off, tk), :], (((1,), (1,)), ((), ())),
                                      preferred_element_type=F32)

    def softmax_pv(slot, j):
        off = pl.multiple_of(j * tk, tk)
        s = s_ref[slot]
        m_prev = m_ref[...]
        m_next = jnp.maximum(m_prev, jnp.max(s, axis=1, keepdims=True))
        alpha = jnp.exp2(m_prev - m_next)
        p = jnp.exp2(s - jnp.concatenate([m_next] * nb, axis=1))
        v1 = jnp.concatenate([v_ref[0, pl.ds(off, tk), :], ones], axis=1)
        acc_ref[...] = (jnp.concatenate([alpha, alpha], axis=1) * acc_ref[...]
                        + jnp.dot(p.astype(BF16), v1, preferred_element_type=F32))
        m_ref[...] = m_next

    qk(0, 0)
    npairs = (nchunks - 1) // 2

    def body(jj, carry):
        j = 2 * jj
        qk(1, j + 1)
        softmax_pv(0, j)
        qk(0, j + 2)
        softmax_pv(1, j + 1)
        return carry

    lax.fori_loop(0, npairs, body, 0)
    if nchunks % 2 == 0:
        qk(1, nchunks - 1)
        softmax_pv(0, nchunks - 2)
        softmax_pv(1, nchunks - 1)
    else:
        softmax_pv(0, nchunks - 1)

    acc = acc_ref[...]
    o = acc[:, 0:LANES] / acc[:, LANES:2 * LANES]
    if mode == "pair":
        out = jnp.where(lo, o[0:tq], o[tq:2 * tq])
    else:
        lam = (jnp.exp(jnp.sum(lq1[...] * lk1[...], axis=1, keepdims=True))
               - jnp.exp(jnp.sum(lq2[...] * lk2[...], axis=1, keepdims=True)) + lam_init)
        dlt = o[0:tq] - lam * o[tq:2 * tq]
        ms = jnp.mean(dlt * dlt, axis=-1, keepdims=True)
        out = dlt * lax.rsqrt(ms + EPS) * sg_ref[...] * (1.0 - lam_init)
    o_ref[0] = out.astype(o_ref.dtype)


def _flash(qkv, tbl, *, n_q, q_row0, n_keys, key_row0, tq, tk, mode="pair", diff_params=None, lam_init=0.0):
    b = qkv.shape[0]
    ncols = tbl.shape[1]
    qb0 = q_row0 // tq
    kb0 = key_row0 // n_keys
    in_specs = [pl.BlockSpec((1, tq, LANES), lambda bi, c, i, t: (bi, qb0 + i, t[0, c])),
                pl.BlockSpec((1, n_keys, LANES), lambda bi, c, i, t: (bi, kb0, t[1, c])),
                pl.BlockSpec((1, n_keys, LANES), lambda bi, c, i, t: (bi, kb0, t[2, c]))]
    args = [qkv, qkv, qkv]
    if mode == "diff":
        in_specs += [pl.BlockSpec((1, HEAD_DIM), lambda bi, c, i, t: (0, 0))] * 4
        in_specs += [pl.BlockSpec((1, LANES), lambda bi, c, i, t: (0, 0))]
        args += list(diff_params)
    grid_spec = pltpu.PrefetchScalarGridSpec(
        num_scalar_prefetch=1,
        grid=(b, ncols, n_q // tq),
        in_specs=in_specs,
        out_specs=pl.BlockSpec((1, tq, LANES), lambda bi, c, i, t: (bi, i, c)),
        scratch_shapes=[pltpu.VMEM((2 * tq, LANES), BF16),
                        pltpu.VMEM((2, 2 * tq, tk), F32),
                        pltpu.VMEM((2 * tq, LANES), F32),
                        pltpu.VMEM((2 * tq, 2 * LANES), F32)])
    return pl.pallas_call(
        functools.partial(_flash_kernel, tq=tq, tk=tk, n_keys=n_keys, mode=mode, lam_init=lam_init),
        grid_spec=grid_spec,
        out_shape=jax.ShapeDtypeStruct((b, n_q, ncols * LANES), BF16),
        compiler_params=_cparams(("parallel", "parallel", "arbitrary")),
        name="flash_%s_%d" % (mode, n_keys),
    )(tbl, *args)


def _na_case_geometry(case, rows):
    r0 = {0: 0, 1: NA_ROWS, 2: rows - NA_ROWS}[case]
    start = min(max(r0 - WIN_ROWS // 2, 0), rows - NA_SPAN)
    return r0, start


def _na_tile_index(case, dr, dk, rows):
    r0, start = _na_case_geometry(case, rows)
    r, kr = r0 + dr, start + dk
    rs = min(max(r - WIN_ROWS // 2, 0), rows - WIN_ROWS)
    if rs <= kr < rs + WIN_ROWS:
        return kr - r + WIN_ROWS
    return 0


def _na_kernel(q_ref, k_ref, v_ref, kc_ref, vc_ref, tl_ref, tr_ref, o_ref, bias_ref, *, rows):
    nblk = rows // NA_ROWS
    bq = NA_ROWS * GRID_W
    bk = NA_SPAN * GRID_W
    for hh in range(2):
        for case in range(3):
            for dr in range(NA_ROWS):
                for dkp in range(NA_SPAN // 2):
                    ia = _na_tile_index(case, dr, 2 * dkp, rows)
                    ib = _na_tile_index(case, dr, 2 * dkp + 1, rows)
                    bias_ref[hh, case, dr * GRID_W:(dr + 1) * GRID_W, dkp * LANES:(dkp + 1) * LANES] = (
                        tl_ref[hh, ia] + tr_ref[hh, ib])

    lane = lax.broadcasted_iota(jnp.int32, (1, LANES), 1)
    lo = lane < HEAD_DIM
    kctx = kc_ref[0]
    vctx = vc_ref[0]
    nt = (((1,), (1,)), ((), ()))

    def body(i, carry):
        r0 = i * NA_ROWS
        start = jnp.clip(r0 - WIN_ROWS // 2, 0, rows - NA_SPAN)
        case = jnp.where(i == 0, 0, jnp.where(i == nblk - 1, 2, 1))
        qoff = pl.multiple_of(i * bq, bq)
        koff = pl.multiple_of(start * GRID_W, GRID_W)
        qb = q_ref[0, pl.ds(qoff, bq), :]
        ks = k_ref[0, pl.ds(koff, bk), :]
        vs = v_ref[0, pl.ds(koff, bk), :]
        zero = jnp.zeros_like(qb)
        outs = []
        for hh in range(2):
            qm = jnp.where(lo, qb, zero) if hh == 0 else jnp.where(lo, zero, qb)
            s_loc = lax.dot_general(qm, ks, nt, preferred_element_type=F32) + bias_ref[hh, case]
            s_ctx = lax.dot_general(qm, kctx, nt, preferred_element_type=F32)
            m = jnp.maximum(jnp.max(s_loc, axis=1, keepdims=True), jnp.max(s_ctx, axis=1, keepdims=True))
            p_loc = jnp.exp2(s_loc - m)
            p_ctx = jnp.exp2(s_ctx - m)
            l = jnp.sum(p_loc, axis=1, keepdims=True) + jnp.sum(p_ctx, axis=1, keepdims=True)
            o = (jnp.dot(p_loc.astype(BF16), vs, preferred_element_type=F32)
                 + jnp.dot(p_ctx.astype(BF16), vctx, preferred_element_type=F32))
            outs.append(o / l)
        o_ref[0, pl.ds(qoff, bq), :] = jnp.where(lo, outs[0], outs[1]).astype(o_ref.dtype)
        return carry

    lax.fori_loop(0, nblk, body, 0)


def _na(qkv, tl, tr, *, n_lat, n_ctx, npairs, qc0, kc0, vc0):
    b = qkv.shape[0]
    rows = n_lat // GRID_W
    cb0 = n_lat // n_ctx
    nt = tl.shape[1]
    return pl.pallas_call(
        functools.partial(_na_kernel, rows=rows),
        grid=(b, npairs),
        in_specs=[pl.BlockSpec((1, n_lat, LANES), lambda bi, j: (bi, 0, qc0 + j)),
                  pl.BlockSpec((1, n_lat, LANES), lambda bi, j: (bi, 0, kc0 + j)),
                  pl.BlockSpec((1, n_lat, LANES), lambda bi, j: (bi, 0, vc0 + j)),
                  pl.BlockSpec((1, n_ctx, LANES), lambda bi, j: (bi, cb0, kc0 + j)),
                  pl.BlockSpec((1, n_ctx, LANES), lambda bi, j: (bi, cb0, vc0 + j)),
                  pl.BlockSpec((2, nt, GRID_W, LANES), lambda bi, j: (j, 0, 0, 0)),
                  pl.BlockSpec((2, nt, GRID_W, LANES), lambda bi, j: (j, 0, 0, 0))],
        out_specs=pl.BlockSpec((1, n_lat, LANES), lambda bi, j: (bi, 0, j)),
        out_shape=jax.ShapeDtypeStruct((b, n_lat, npairs * LANES), BF16),
        scratch_shapes=[pltpu.VMEM((2, 3, NA_ROWS * GRID_W, NA_SPAN * GRID_W), F32)],
        compiler_params=_cparams(("parallel", "arbitrary")),
        name="neighbourhood_attn",
    )(qkv, qkv, qkv, qkv, qkv, tl, tr)


def _na_bias_tiles(rpb):
    h = rpb.shape[0]
    cols = np.arange(GRID_W)
    cs = np.clip(cols - WIN_COLS // 2, 0, GRID_W - WIN_COLS)
    kc = cols[None, :]
    valid = (kc >= cs[:, None]) & (kc < cs[:, None] + WIN_COLS)
    ci = np.clip(kc - cols[:, None] + WIN_COLS - 1, 0, 2 * WIN_COLS - 2)
    t = jnp.where(jnp.asarray(valid)[None, None], rpb[:, :, ci].astype(F32) * LOG2E, NEG)
    t = jnp.concatenate([jnp.full((h, 1, GRID_W, GRID_W), NEG, F32), t], axis=1)
    z = jnp.zeros_like(t)
    return jnp.concatenate([t, z], axis=-1), jnp.concatenate([z, t], axis=-1)


def _outproj_kernel(*refs, n_parts, has_ctx, lat_tiles):
    refs = list(refs)
    parts = [refs.pop(0) for _ in range(n_parts)]
    octx_ref = refs.pop(0) if has_ctx else None
    w_ref, x_ref, gm_ref, ng_ref, sh_ref, sc_ref, rw_ref, xo_ref, hp_ref, lg_ref, proj_ref = refs

    def lat():
        acc = None
        off = 0
        for p in parts:
            wdt = p.shape[-1]
            t = jnp.dot(p[0], w_ref[off:off + wdt, :], preferred_element_type=F32)
            acc = t if acc is None else acc + t
            off += wdt
        proj_ref[...] = acc

    if has_ctx:
        is_ctx = pl.program_id(1) >= lat_tiles
        pl.when(jnp.logical_not(is_ctx))(lat)

        @pl.when(is_ctx)
        def _():
            proj_ref[...] = jnp.dot(octx_ref[0], w_ref[...], preferred_element_type=F32)
    else:
        lat()

    x = x_ref[0] + gm_ref[0] * proj_ref[...]
    xo_ref[0] = x
    ms = jnp.mean(x * x, axis=-1, keepdims=True)
    h = x * lax.rsqrt(ms + EPS) * ng_ref[...]
    h = h * (1.0 + sc_ref[0]) + sh_ref[0]
    hp_ref[0] = _pack_bf16_pairs(h)
    lg_ref[...] = lax.dot_general(rw_ref[...], h, (((1,), (1,)), ((), ())), preferred_element_type=F32,
                                  precision=lax.Precision.HIGHEST)


def _outproj(parts, octx, w, xa, mods, gain, rw_t, *, n_lat, n_rows):
    b, _, d = xa.shape
    lat_tiles = n_lat // TM
    nt = n_rows // TM
    has_ctx = octx is not None

    def mod_map(chunk):
        return lambda bi, i: (jnp.where(i >= lat_tiles, b, bi) * 6 + chunk, 0, 0)

    in_specs = [pl.BlockSpec((1, TM, p.shape[-1]), lambda bi, i: (bi, jnp.minimum(i, lat_tiles - 1), 0))
                for p in parts]
    args = list(parts)
    if has_ctx:
        in_specs.append(pl.BlockSpec((1, TM, d), lambda bi, i: (bi, 0, 0)))
        args.append(octx)
    in_specs += [pl.BlockSpec((d, d), lambda bi, i: (0, 0)),
                 pl.BlockSpec((1, TM, d), lambda bi, i: (bi, i, 0)),
                 pl.BlockSpec((1, 1, d), mod_map(2)),
                 pl.BlockSpec((1, d), lambda bi, i: (0, 0)),
                 pl.BlockSpec((1, 1, d), mod_map(3)),
                 pl.BlockSpec((1, 1, d), mod_map(4)),
                 pl.BlockSpec((N_EXPERTS, d), lambda bi, i: (0, 0))]
    args += [w, xa, mods, gain, mods, mods, rw_t]
    return pl.pallas_call(
        functools.partial(_outproj_kernel, n_parts=len(parts), has_ctx=has_ctx, lat_tiles=lat_tiles),
        grid=(b, nt),
        in_specs=in_specs,
        out_specs=[pl.BlockSpec((1, TM, d), lambda bi, i: (bi, i, 0)),
                   pl.BlockSpec((1, TM, d // 2), lambda bi, i: (bi, i, 0)),
                   pl.BlockSpec((N_EXPERTS, TM), lambda bi, i: (0, bi * nt + i))],
        out_shape=[jax.ShapeDtypeStruct((b, n_rows, d), F32),
                   jax.ShapeDtypeStruct((b, n_rows, d // 2), U32),
                   jax.ShapeDtypeStruct((N_EXPERTS, b * n_rows), F32)],
        scratch_shapes=[pltpu.VMEM((TM, d), F32)],
        compiler_params=_cparams(("parallel", "arbitrary")),
        name="outproj_ffnnorm",
    )(*args)


def _router_kernel(lg_ref, rb_ref, tri_ref, idx_ref, gate_ref, pos_ref, cnt_ref, run_ref):
    @pl.when(pl.program_id(0) == 0)
    def _():
        run_ref[...] = jnp.zeros(run_ref.shape, F32)

    scores = jax.nn.sigmoid(lg_ref[...])
    work = scores + rb_ref[...]
    eidx = lax.broadcasted_iota(jnp.int32, work.shape, 0)
    hits, idx_rows, sel_rows = [], [], []
    for _ in range(TOP_K):
        mx = jnp.max(work, axis=0, keepdims=True)
        first = jnp.min(jnp.where(work == mx, eidx, N_EXPERTS), axis=0, keepdims=True)
        hit = eidx == first
        hits.append(hit)
        idx_rows.append(first)
        sel_rows.append(jnp.sum(jnp.where(hit, scores, 0.0), axis=0, keepdims=True))
        work = jnp.where(hit, NEG, work)
    mask = jnp.zeros(work.shape, F32)
    for hit in hits:
        mask = mask + hit.astype(F32)
    denom = sel_rows[0]
    for r in sel_rows[1:]:
        denom = denom + r
    csum = jnp.dot(mask.astype(BF16), tri_ref[...], preferred_element_type=F32)
    tm = mask.shape[1]
    posall = run_ref[:, 0:1] + csum - mask
    pos_rows = [jnp.sum(jnp.where(hit, posall, 0.0), axis=0, keepdims=True) for hit in hits]
    run_ref[...] = run_ref[...] + csum[:, tm - 1:tm]
    idx_ref[...] = jnp.concatenate(idx_rows, axis=0)
    gate_ref[...] = jnp.concatenate(sel_rows, axis=0) / denom * ROUTED_SCALE
    pos_ref[...] = jnp.concatenate(pos_rows, axis=0).astype(jnp.int32)
    cnt_ref[...] = run_ref[...]


def _router(logits_t, rb):
    e, t = logits_t.shape
    tri = jnp.asarray(np.triu(np.ones((TM, TM), np.float32)), BF16)
    return pl.pallas_call(
        _router_kernel,
        grid=(t // TM,),
        in_specs=[pl.BlockSpec((e, TM), lambda i: (0, i)),
                  pl.BlockSpec((e, 1), lambda i: (0, 0)),
                  pl.BlockSpec((TM, TM), lambda i: (0, 0))],
        out_specs=[pl.BlockSpec((TOP_K, TM), lambda i: (0, i)),
                   pl.BlockSpec((TOP_K, TM), lambda i: (0, i)),
                   pl.BlockSpec((TOP_K, TM), lambda i: (0, i)),
                   pl.BlockSpec((e, LANES), lambda i: (0, 0))],
        out_shape=[jax.ShapeDtypeStruct((TOP_K, t), jnp.int32),
                   jax.ShapeDtypeStruct((TOP_K, t), F32),
                   jax.ShapeDtypeStruct((TOP_K, t), jnp.int32),
                   jax.ShapeDtypeStruct((e, LANES), F32)],
        scratch_shapes=[pltpu.VMEM((e, LANES), F32)],
        compiler_params=_cparams(("arbitrary",)),
        name="router_topk",
    )(logits_t, rb, tri)


def _dispatch_kernel(fill_ref, dest_ref, h_hbm, xs_hbm, zbuf, fsem, sem):
    i = pl.program_id(0)
    last = pl.num_programs(0) - 1
    nrow = TOP_K * MOE_TT

    def fill_copy(e):
        start = pl.multiple_of(fill_ref[0, e], MOE_BLOCK)
        return pltpu.make_async_copy(zbuf, xs_hbm.at[pl.ds(start, MOE_BLOCK), :], fsem)

    @pl.when(i == 0)
    def _():
        zbuf[...] = jnp.zeros(zbuf.shape, zbuf.dtype)
        for e in range(fill_ref.shape[1]):
            pl.when(fill_ref[1, e] != 0)(lambda e=e: fill_copy(e).start())
        for e in range(fill_ref.shape[1]):
            pl.when(fill_ref[1, e] != 0)(lambda e=e: fill_copy(e).wait())

    slot = i % 2
    base = i * MOE_TT
    for tl0 in range(0, MOE_TT, DMA_BATCH):
        ds = [[dest_ref[0, 0, k * MOE_TT + tl0 + u] for k in range(TOP_K)] for u in range(DMA_BATCH)]
        for u in range(DMA_BATCH):
            src = h_hbm.at[pl.ds(base + tl0 + u, 1), :]
            for k in range(TOP_K):
                pltpu.make_async_copy(src, xs_hbm.at[pl.ds(ds[u][k], 1), :], sem.at[slot]).start()

    def drain(s):
        pltpu.make_async_copy(h_hbm.at[pl.ds(0, nrow), :], xs_hbm.at[pl.ds(0, nrow), :], sem.at[s]).wait()

    pl.when(i > 0)(lambda: drain(1 - slot))
    pl.when(i == last)(lambda: drain(slot))


def _dispatch(h2p, dest_tiles, fill, n_pad):
    t, w = h2p.shape
    grid_spec = pltpu.PrefetchScalarGridSpec(
        num_scalar_prefetch=1,
        grid=(t // MOE_TT,),
        in_specs=[pl.BlockSpec((1, 1, TOP_K * MOE_TT), lambda i, f: (i, 0, 0), memory_space=pltpu.SMEM),
                  pl.BlockSpec(memory_space=pl.ANY)],
        out_specs=pl.BlockSpec(memory_space=pl.ANY),
        scratch_shapes=[pltpu.VMEM((MOE_BLOCK, w), U32),
                        pltpu.SemaphoreType.DMA,
                        pltpu.SemaphoreType.DMA((2,))])
    return pl.pallas_call(
        _dispatch_kernel,
        grid_spec=grid_spec,
        out_shape=jax.ShapeDtypeStruct((n_pad, w), U32),
        compiler_params=_cparams(("arbitrary",)),
        name="moe_dispatch",
    )(fill, dest_tiles, h2p)


def _expert_kernel(be_ref, nu_ref, xs_ref, wg_ref, wu_ref, wd_ref, ys_ref, wgb, wub, wdb):
    i = pl.program_id(0)

    @pl.when(i >= nu_ref[0])
    def _():
        ys_ref[...] = jnp.zeros(ys_ref.shape, ys_ref.dtype)

    @pl.when(i < nu_ref[0])
    def _():
        e = be_ref[i]
        prev = be_ref[jnp.maximum(i - 1, 0)]

        @pl.when(jnp.logical_or(i == 0, e != prev))
        def _():
            wgb[...] = wg_ref[0].astype(BF16)
            wub[...] = wu_ref[0].astype(BF16)
            wdb[...] = wd_ref[0].astype(BF16)

        lo, hi = _unpack_bf16_pairs(xs_ref[...])
        lo = lo.astype(BF16)
        hi = hi.astype(BF16)
        half = lo.shape[1]
        g = (jnp.dot(lo, wgb[0:half, :], preferred_element_type=F32)
             + jnp.dot(hi, wgb[half:2 * half, :], preferred_element_type=F32))
        u = (jnp.dot(lo, wub[0:half, :], preferred_element_type=F32)
             + jnp.dot(hi, wub[half:2 * half, :], preferred_element_type=F32))
        hid = (g * jax.nn.sigmoid(g)) * u
        y = jnp.dot(hid.astype(BF16), wdb[...], preferred_element_type=F32)
        ys_ref[...] = _pack_bf16_pairs(y)


def _experts(xs, blk_expert, n_used, wg, wu, wd):
    n_pad, w = xs.shape
    _, d, f = wg.shape
    nblk = n_pad // MOE_BLOCK

    def row_map(i, be, nu):
        return (jnp.minimum(i, nu[0] - 1), 0)

    grid_spec = pltpu.PrefetchScalarGridSpec(
        num_scalar_prefetch=2,
        grid=(nblk,),
        in_specs=[pl.BlockSpec((MOE_BLOCK, w), row_map),
                  pl.BlockSpec((1, d, f), lambda i, be, nu: (be[i], 0, 0)),
                  pl.BlockSpec((1, d, f), lambda i, be, nu: (be[i], 0, 0)),
                  pl.BlockSpec((1, f, d), lambda i, be, nu: (be[i], 0, 0))],
        out_specs=pl.BlockSpec((MOE_BLOCK, w), lambda i, be, nu: (i, 0)),
        scratch_shapes=[pltpu.VMEM((d, f), BF16), pltpu.VMEM((d, f), BF16), pltpu.VMEM((f, d), BF16)])
    return pl.pallas_call(
        _expert_kernel,
        grid_spec=grid_spec,
        out_shape=jax.ShapeDtypeStruct((n_pad, w), U32),
        compiler_params=_cparams(("arbitrary",)),
        name="moe_experts",
    )(blk_expert, n_used, xs, wg, wu, wd)


def _combine_kernel(dest_ref, ys_hbm, gate_ref, hp_ref, x_ref, gf_ref, sg_ref, su_ref, sd_ref, fn_ref,
                    o_ref, buf, sem, *, final):
    for tl0 in range(0, MOE_TT, DMA_BATCH):
        ds = [[dest_ref[0, 0, k * MOE_TT + tl0 + u] for k in range(TOP_K)] for u in range(DMA_BATCH)]
        for u in range(DMA_BATCH):
            for k in range(TOP_K):
                pltpu.make_async_copy(ys_hbm.at[pl.ds(ds[u][k], 1), :], buf.at[k, pl.ds(tl0 + u, 1), :],
                                      sem).start()

    lo, hi = _unpack_bf16_pairs(hp_ref[...])
    lo = lo.astype(BF16)
    hi = hi.astype(BF16)
    half = lo.shape[1]
    g = (jnp.dot(lo, sg_ref[0:half, :], preferred_element_type=F32)
         + jnp.dot(hi, sg_ref[half:2 * half, :], preferred_element_type=F32))
    u = (jnp.dot(lo, su_ref[0:half, :], preferred_element_type=F32)
         + jnp.dot(hi, su_ref[half:2 * half, :], preferred_element_type=F32))
    y = jnp.dot(((g * jax.nn.sigmoid(g)) * u).astype(BF16), sd_ref[...], preferred_element_type=F32)

    for k in range(TOP_K):
        pltpu.make_async_copy(ys_hbm.at[pl.ds(0, MOE_TT), :], buf.at[k], sem).wait()
    gates = gate_ref[...]
    for k in range(TOP_K):
        ylo, yhi = _unpack_bf16_pairs(buf[k])
        y = y + gates[:, k:k + 1] * jnp.concatenate([ylo, yhi], axis=1)
    x = x_ref[...] + gf_ref[0] * y
    if final:
        ms = jnp.mean(x * x, axis=-1, keepdims=True)
        x = x * lax.rsqrt(ms + EPS) * fn_ref[...]
    o_ref[...] = x


def _combine(ys, dest_tiles, gates_t, h2p, x_new, mods, sg, su, sd, fn, *, n_batch, n_rows, n_lat, final):
    t, d = x_new.shape
    w = h2p.shape[1]
    f = sg.shape[1]
    tiles_b = n_rows // MOE_TT
    lat_tiles = n_lat // MOE_TT

    def mod_map(i, dest):
        return (jnp.where(i % tiles_b >= lat_tiles, n_batch, i // tiles_b) * 6 + 5, 0, 0)

    grid_spec = pltpu.PrefetchScalarGridSpec(
        num_scalar_prefetch=0,
        grid=(t // MOE_TT,),
        in_specs=[pl.BlockSpec((1, 1, TOP_K * MOE_TT), lambda i: (i, 0, 0), memory_space=pltpu.SMEM),
                  pl.BlockSpec(memory_space=pl.ANY),
                  pl.BlockSpec((MOE_TT, TOP_K), lambda i: (i, 0)),
                  pl.BlockSpec((MOE_TT, w), lambda i: (i, 0)),
                  pl.BlockSpec((MOE_TT, d), lambda i: (i, 0)),
                  pl.BlockSpec((1, 1, d), lambda i: mod_map(i, None)),
                  pl.BlockSpec((d, f), lambda i: (0, 0)),
                  pl.BlockSpec((d, f), lambda i: (0, 0)),
                  pl.BlockSpec((f, d), lambda i: (0, 0)),
                  pl.BlockSpec((1, d), lambda i: (0, 0))],
        out_specs=pl.BlockSpec((MOE_TT, d), lambda i: (i, 0)),
        scratch_shapes=[pltpu.VMEM((TOP_K, MOE_TT, w), U32), pltpu.SemaphoreType.DMA])
    return pl.pallas_call(
        functools.partial(_combine_kernel, final=final),
        grid_spec=grid_spec,
        out_shape=jax.ShapeDtypeStruct((t, d), F32),
        compiler_params=_cparams(("arbitrary",)),
        name="moe_combine",
    )(dest_tiles, ys, gates_t, h2p, x_new, mods, sg, su, sd, fn)


def _rope_tables(n_lat, n_ctx):
    t = np.arange(n_lat)
    row = (t // GRID_W).astype(np.float32)
    col = (t % GRID_W).astype(np.float32)
    npairs = HEAD_DIM // 4
    inv_freq = jnp.asarray(ROPE_THETA, F32) ** (-jnp.arange(npairs, dtype=F32) / npairs)
    ang = jnp.concatenate([jnp.asarray(row)[:, None] * inv_freq, jnp.asarray(col)[:, None] * inv_freq], axis=-1)
    cos = jnp.repeat(jnp.cos(ang), 2, axis=-1)
    sin = jnp.repeat(jnp.sin(ang), 2, axis=-1)
    sign = jnp.asarray(np.tile(np.array([-1.0, 1.0], np.float32), HEAD_DIM // 2))
    cosf = jnp.tile(cos, (1, LANES // HEAD_DIM))
    sins = jnp.tile(sin * sign, (1, LANES // HEAD_DIM))
    cosf = jnp.concatenate([cosf, jnp.ones((n_ctx, LANES), F32)], axis=0)
    sins = jnp.concatenate([sins, jnp.zeros((n_ctx, LANES), F32)], axis=0)
    return cosf, sins


def _moe_block(logits_t, h2p, x_new, mods_l, rb, wg, wu, wd, sg, su, sd, fn, *, n_batch, n_rows, n_lat, final):
    t = h2p.shape[0]
    idx, gates, pos, cnt = _router(logits_t, rb.reshape(N_EXPERTS, 1))
    counts = cnt[:, 0].astype(jnp.int32)
    padded = (counts + MOE_BLOCK - 1) // MOE_BLOCK * MOE_BLOCK
    pad_end = jnp.cumsum(padded)
    pad_start = pad_end - padded
    n_blocks = (t * TOP_K + N_EXPERTS * (MOE_BLOCK - 1) + MOE_BLOCK - 1) // MOE_BLOCK
    n_pad = n_blocks * MOE_BLOCK
    dest = pad_start[idx] + pos
    dest_tiles = dest.reshape(TOP_K, t // MOE_TT, MOE_TT).transpose(1, 0, 2).reshape(t // MOE_TT, 1, TOP_K * MOE_TT)
    blk_start = jnp.arange(n_blocks, dtype=jnp.int32) * MOE_BLOCK
    blk_expert = jnp.minimum(jnp.searchsorted(pad_end, blk_start, side="right"), N_EXPERTS - 1).astype(jnp.int32)
    n_used = (pad_end[-1:] // MOE_BLOCK).astype(jnp.int32)
    tail = n_used + jnp.arange(N_EXPERTS + 1, dtype=jnp.int32)
    fill = jnp.stack([jnp.concatenate([jnp.maximum(pad_end - MOE_BLOCK, 0), jnp.minimum(tail, n_blocks - 1) * MOE_BLOCK]),
                      jnp.concatenate([counts > 0, tail < n_blocks]).astype(jnp.int32)]).astype(jnp.int32)
    xs = _dispatch(h2p, dest_tiles, fill, n_pad)
    ys = _experts(xs, blk_expert, n_used, wg, wu, wd)
    return _combine(ys, dest_tiles, gates.T, h2p, x_new, mods_l, sg.astype(BF16), su.astype(BF16),
                    sd.astype(BF16), fn, n_batch=n_batch, n_rows=n_rows, n_lat=n_lat, final=final)


def kernel(x, c, ctx, c_ctx, ada_w, ada_b, norm_mix, norm_ffn, ab_w_in, ab_w_out, na_rpb, gqa_q_gain,
           gqa_k_gain, diff_w_in, diff_w_out, diff_lq1, diff_lk1, diff_lq2, diff_lk2, diff_sub_gain,
           router_w, router_bias, expert_w_gate, expert_w_up, expert_w_down, shared_w_gate, shared_w_up,
           shared_w_down, final_norm):
    b, n, d = x.shape
    n_ctx = ctx.shape[1]
    depth = ada_w.shape[0]
    assert depth == 2 and n_ctx == TM and n % (NA_ROWS * GRID_W) == 0 and d % LANES == 0
    assert (n + n_ctx) % FLASH_TK == 0 and n % FLASH_TQ == 0 and n % MOE_TT == 0 and n_ctx % MOE_TT == 0

    cvec = jnp.concatenate([c, c_ctx[None], jnp.zeros((8 - b - 1, d), F32)], axis=0)
    mods = _mods(cvec, ada_w, ada_b)[:, :b + 1].reshape(depth, (b + 1) * 6, 1, d)
    cosf, sins = _rope_tables(n, n_ctx)
    gm = jnp.asarray(np.kron(np.eye(LANES // HEAD_DIM), np.full((HEAD_DIM, HEAD_DIM), 1.0 / HEAD_DIM)), BF16)
    ones = jnp.ones((1, LANES), F32)
    fn = final_norm.reshape(1, d)

    xa = jnp.concatenate([x, ctx], axis=1)

    w = ab_w_in[0]
    kb = [w[:, 2048 + HEAD_DIM * g: 2048 + HEAD_DIM * (g + 1)] for g in range(2)]
    vb = [w[:, 2176 + HEAD_DIM * g: 2176 + HEAD_DIM * (g + 1)] for g in range(2)]
    w0 = jnp.concatenate([w[:, :2048], kb[0], kb[0], kb[1], kb[1], vb[0], vb[0], vb[1], vb[1]],
                         axis=1).astype(BF16)
    plan0 = ([(None, False, True)] * 4 + [(None, False, False)] * 8 + [("q", True, True)] * 4
             + [("k", True, False)] * 2 + [(None, False, False)] * 2)
    qg = jnp.tile(gqa_q_gain[0].reshape(1, HEAD_DIM), (1, LANES // HEAD_DIM))
    kg = jnp.tile(gqa_k_gain[0].reshape(1, HEAD_DIM), (1, LANES // HEAD_DIM))
    qkv = _proj(xa, norm_mix[0].reshape(1, d), mods[0], 0, 1, w0, cosf, sins, gm, qg, kg, plan0, n)

    tl, tr = _na_bias_tiles(na_rpb[0])
    o_na = _na(qkv, tl, tr, n_lat=n, n_ctx=n_ctx, npairs=4, qc0=0, kc0=4, vc0=8)
    tbl_g = jnp.asarray([[12, 13, 14, 15], [16, 16, 17, 17], [18, 18, 19, 19]], jnp.int32)
    o_gqa = _flash(qkv, tbl_g, n_q=n, q_row0=0, n_keys=n + n_ctx, key_row0=0, tq=FLASH_TQ, tk=FLASH_TK)
    tbl_c = jnp.asarray([[0, 1, 2, 3, 12, 13, 14, 15], [4, 5, 6, 7, 16, 16, 17, 17],
                         [8, 9, 10, 11, 18, 18, 19, 19]], jnp.int32)
    o_ctx = _flash(qkv, tbl_c, n_q=n_ctx, q_row0=n, n_keys=n_ctx, key_row0=n, tq=n_ctx, tk=n_ctx)

    s_all = n + n_ctx
    x_new, h2p, lg = _outproj([o_na, o_gqa], o_ctx, ab_w_out[0].astype(BF16), xa, mods[0],
                              norm_ffn[0].reshape(1, d), router_w[0].T, n_lat=n, n_rows=s_all)
    xa = _moe_block(lg, h2p.reshape(b * s_all, d // 2), x_new.reshape(b * s_all, d), mods[0], router_bias[0],
                    expert_w_gate[0], expert_w_up[0], expert_w_down[0], shared_w_gate[0], shared_w_up[0],
                    shared_w_down[0], fn, n_batch=b, n_rows=s_all, n_lat=n, final=False).reshape(b, s_all, d)

    lam_init = 0.8 - 0.6 * math.exp(-0.3 * 1)
    plan1 = [(None, True, True)] * 8 + [(None, True, False)] * 8 + [(None, False, False)] * 8
    qkv = _proj(xa, norm_mix[1].reshape(1, d), mods[1], 0, 1, diff_w_in[0].astype(BF16), cosf, sins, gm,
                ones, ones, plan1, n)
    tbl_d = jnp.asarray([list(range(0, 8)), list(range(8, 16)), list(range(16, 24))], jnp.int32)
    dp = [diff_lq1[0].reshape(1, HEAD_DIM), diff_lk1[0].reshape(1, HEAD_DIM),
          diff_lq2[0].reshape(1, HEAD_DIM), diff_lk2[0].reshape(1, HEAD_DIM),
          diff_sub_gain[0].reshape(1, LANES)]
    o_diff = _flash(qkv, tbl_d, n_q=n, q_row0=0, n_keys=n + n_ctx, key_row0=0, tq=FLASH_TQ, tk=FLASH_TK,
                    mode="diff", diff_params=dp, lam_init=lam_init)
    x_new, h2p, lg = _outproj([o_diff], None, diff_w_out[0].astype(BF16), xa, mods[1],
                              norm_ffn[1].reshape(1, d), router_w[1].T, n_lat=n, n_rows=n)
    return _moe_block(lg, h2p.reshape(b * n, d // 2), x_new.reshape(b * n, d), mods[1], router_bias[1],
                      expert_w_gate[1], expert_w_up[1], expert_w_down[1], shared_w_gate[1], shared_w_up[1],
                      shared_w_down[1], fn, n_batch=b, n_rows=n, n_lat=n, final=True).reshape(b, n, d)
```

```python
import functools
import math

import jax
import jax.numpy as jnp
import numpy as np
from jax import lax
from jax.experimental import pallas as pl
from jax.experimental.pallas import tpu as pltpu

F32 = jnp.float32
BF16 = jnp.bfloat16
U32 = jnp.uint32

LANES = 128
HEAD_DIM = 64
GRID_W = 64
WIN_ROWS = 8
WIN_COLS = 16
ROPE_THETA = 10000.0
EPS = 1e-6
N_EXPERTS = 64
TOP_K = 8
ROUTED_SCALE = 2.5
NEG = -1e30
LOG2E = math.log2(math.e)
Q_SCALE = HEAD_DIM ** -0.5 * LOG2E
VMEM_LIMIT = 56 * 1024 * 1024

TM = 256
NA_ROWS = 8
NA_SPAN = 16
FLASH_TQ = 512
FLASH_TK = 768
MOE_BLOCK = 256
MOE_TT = 128
DMA_BATCH = 2
HI_MASK = 0xFFFF0000


def _cparams(sem):
    return pltpu.CompilerParams(dimension_semantics=sem, vmem_limit_bytes=VMEM_LIMIT)


def _pack_bf16_pairs(x):
    w = x.shape[1] // 2
    bits = lax.bitcast_convert_type(x.astype(BF16).astype(F32), U32)
    return (bits[:, w:] & jnp.uint32(HI_MASK)) | (bits[:, :w] >> 16)


def _unpack_bf16_pairs(p):
    lo = lax.bitcast_convert_type(p << 16, F32)
    hi = lax.bitcast_convert_type(p & jnp.uint32(HI_MASK), F32)
    return lo, hi


ROW_CH = 4


def _store_rowchunks(ref, idx, packed):
    m = packed.shape[0]
    for c in range(ROW_CH):
        ref[idx + (pl.ds(c, m, stride=ROW_CH), slice(None))] = packed[:, c * LANES:(c + 1) * LANES]


def _load_rowchunks(ref, idx, m, dtype):
    planes = [_unpack_bf16_pairs(ref[idx + (pl.ds(c, m, stride=ROW_CH), slice(None))]) for c in range(ROW_CH)]
    return jnp.concatenate([lo.astype(dtype) for lo, _ in planes] + [hi.astype(dtype) for _, hi in planes], axis=1)


def _mods_kernel(c_ref, w_ref, b_ref, o_ref):
    c = c_ref[...]
    s = c * jax.nn.sigmoid(c)
    o_ref[0] = jnp.dot(s.astype(BF16), w_ref[0].astype(BF16), preferred_element_type=F32) + b_ref[0]


def _mods(cvec, ada_w, ada_b):
    depth, d, d6 = ada_w.shape
    tn = 1536
    return pl.pallas_call(
        _mods_kernel,
        grid=(depth, d6 // tn),
        in_specs=[pl.BlockSpec((8, d), lambda l, j: (0, 0)),
                  pl.BlockSpec((1, d, tn), lambda l, j: (l, 0, j)),
                  pl.BlockSpec((1, 1, tn), lambda l, j: (l, 0, j))],
        out_specs=pl.BlockSpec((1, 8, tn), lambda l, j: (l, 0, j)),
        out_shape=jax.ShapeDtypeStruct((depth, 8, d6), F32),
        compiler_params=_cparams(("arbitrary", "arbitrary")),
        name="adaln_mods",
    )(cvec, ada_w, ada_b.reshape(depth, 1, d6))


def _proj_kernel(x_ref, g_ref, sh_ref, sc_ref, w_ref, cos_ref, sin_ref, gm_ref, qg_ref, kg_ref,
                 o_ref, *, plan):
    x = x_ref[0]
    ms = jnp.mean(x * x, axis=-1, keepdims=True)
    h = x * lax.rsqrt(ms + EPS) * g_ref[...]
    h = h * (1.0 + sc_ref[0]) + sh_ref[0]
    y = jnp.dot(h.astype(BF16), w_ref[...], preferred_element_type=F32)
    cosf = cos_ref[...]
    sins = sin_ref[...]
    even = (lax.broadcasted_iota(jnp.int32, (1, LANES), 1) % 2) == 0
    for c, (norm, rope, scale) in enumerate(plan):
        yc = y[:, c * LANES:(c + 1) * LANES]
        if norm:
            ms2 = jnp.dot((yc * yc).astype(BF16), gm_ref[...], preferred_element_type=F32)
            gain = qg_ref[...] if norm == "q" else kg_ref[...]
            yc = yc * lax.rsqrt(ms2 + EPS) * gain
        if rope:
            sw = jnp.where(even, pltpu.roll(yc, LANES - 1, 1), pltpu.roll(yc, 1, 1))
            yc = yc * cosf + sw * sins
        if scale:
            yc = yc * Q_SCALE
        o_ref[0, :, c * LANES:(c + 1) * LANES] = yc.astype(BF16)


def _proj(xa, gain, mods, sh_idx, sc_idx, w, cosf, sins, gm, qg, kg, plan, n_lat):
    b, s, d = xa.shape
    wcols = w.shape[1]
    nt = s // TM
    lat_tiles = n_lat // TM

    def mod_map(chunk):
        return lambda bi, i: (jnp.where(i >= lat_tiles, b, bi) * 6 + chunk, 0, 0)

    return pl.pallas_call(
        functools.partial(_proj_kernel, plan=plan),
        grid=(b, nt),
        in_specs=[pl.BlockSpec((1, TM, d), lambda bi, i: (bi, i, 0)),
                  pl.BlockSpec((1, d), lambda bi, i: (0, 0)),
                  pl.BlockSpec((1, 1, d), mod_map(sh_idx)),
                  pl.BlockSpec((1, 1, d), mod_map(sc_idx)),
                  pl.BlockSpec((d, wcols), lambda bi, i: (0, 0)),
                  pl.BlockSpec((TM, LANES), lambda bi, i: (i, 0)),
                  pl.BlockSpec((TM, LANES), lambda bi, i: (i, 0)),
                  pl.BlockSpec((LANES, LANES), lambda bi, i: (0, 0)),
                  pl.BlockSpec((1, LANES), lambda bi, i: (0, 0)),
                  pl.BlockSpec((1, LANES), lambda bi, i: (0, 0))],
        out_specs=pl.BlockSpec((1, TM, wcols), lambda bi, i: (bi, i, 0)),
        out_shape=jax.ShapeDtypeStruct((b, s, wcols), BF16),
        compiler_params=_cparams(("parallel", "arbitrary")),
        name="norm_inproj",
    )(xa, gain, mods, mods, w, cosf, sins, gm, qg, kg)


def _flash_kernel(tbl_ref, *refs, tq, tk, n_keys, mode, lam_init):
    del tbl_ref
    refs = list(refs)
    q_ref, k_ref, v_ref = refs[:3]
    refs = refs[3:]
    if mode == "diff":
        lq1, lk1, lq2, lk2, sg_ref = refs[:5]
        refs = refs[5:]
    o_ref, qs_ref, s_ref, m_ref, acc_ref = refs

    lane = lax.broadcasted_iota(jnp.int32, (1, LANES), 1)
    lo = lane < HEAD_DIM
    q = q_ref[0]
    zero = jnp.zeros_like(q)
    qs_ref[0:tq, :] = jnp.where(lo, q, zero)
    qs_ref[tq:2 * tq, :] = jnp.where(lo, zero, q)
    m_ref[...] = jnp.full(m_ref.shape, NEG, F32)
    acc_ref[...] = jnp.zeros(acc_ref.shape, F32)
    nb = tk // LANES
    nchunks = n_keys // tk
    ones = jnp.ones((tk, LANES), BF16)

    def qk(slot, j):
        off = pl.multiple_of(j * tk, tk)
        s_ref[slot] = lax.dot_general(qs_ref[...], k_ref[0, pl.ds(off, tk), :], (((1,), (1,)), ((), ())),
                                      preferred_element_type=F32)

    def softmax_pv(slot, j):
        off = pl.multiple_of(j * tk, tk)
        s = s_ref[slot]
        m_prev = m_ref[...]
        m_next = jnp.maximum(m_prev, jnp.max(s, axis=1, keepdims=True))
        alpha = jnp.exp2(m_prev - m_next)
        p = jnp.exp2(s - jnp.concatenate([m_next] * nb, axis=1))
        v1 = jnp.concatenate([v_ref[0, pl.ds(off, tk), :], ones], axis=1)
        acc_ref[...] = (jnp.concatenate([alpha, alpha], axis=1) * acc_ref[...]
                        + jnp.dot(p.astype(BF16), v1, preferred_element_type=F32))
        m_ref[...] = m_next

    qk(0, 0)
    npairs = (nchunks - 1) // 2

    def body(jj, carry):
        j = 2 * jj
        qk(1, j + 1)
        softmax_pv(0, j)
        qk(0, j + 2)
        softmax_pv(1, j + 1)
        return carry

    lax.fori_loop(0, npairs, body, 0)
    if nchunks % 2 == 0:
        qk(1, nchunks - 1)
        softmax_pv(0, nchunks - 2)
        softmax_pv(1, nchunks - 1)
    else:
        softmax_pv(0, nchunks - 1)

    acc = acc_ref[...]
    o = acc[:, 0:LANES] / acc[:, LANES:2 * LANES]
    if mode == "pair":
        out = jnp.where(lo, o[0:tq], o[tq:2 * tq])
    else:
        lam = (jnp.exp(jnp.sum(lq1[...] * lk1[...], axis=1, keepdims=True))
               - jnp.exp(jnp.sum(lq2[...] * lk2[...], axis=1, keepdims=True)) + lam_init)
        dlt = o[0:tq] - lam * o[tq:2 * tq]
        ms = jnp.mean(dlt * dlt, axis=-1, keepdims=True)
        out = dlt * lax.rsqrt(ms + EPS) * sg_ref[...] * (1.0 - lam_init)
    o_ref[0] = out.astype(o_ref.dtype)


def _flash(qkv, tbl, *, n_q, q_row0, n_keys, key_row0, tq, tk, mode="pair", diff_params=None, lam_init=0.0):
    b = qkv.shape[0]
    ncols = tbl.shape[1]
    qb0 = q_row0 // tq
    kb0 = key_row0 // n_keys
    in_specs = [pl.BlockSpec((1, tq, LANES), lambda bi, c, i, t: (bi, qb0 + i, t[0, c])),
                pl.BlockSpec((1, n_keys, LANES), lambda bi, c, i, t: (bi, kb0, t[1, c])),
                pl.BlockSpec((1, n_keys, LANES), lambda bi, c, i, t: (bi, kb0, t[2, c]))]
    args = [qkv, qkv, qkv]
    if mode == "diff":
        in_specs += [pl.BlockSpec((1, HEAD_DIM), lambda bi, c, i, t: (0, 0))] * 4
        in_specs += [pl.BlockSpec((1, LANES), lambda bi, c, i, t: (0, 0))]
        args += list(diff_params)
    grid_spec = pltpu.PrefetchScalarGridSpec(
        num_scalar_prefetch=1,
        grid=(b, ncols, n_q // tq),
        in_specs=in_specs,
        out_specs=pl.BlockSpec((1, tq, LANES), lambda bi, c, i, t: (bi, i, c)),
        scratch_shapes=[pltpu.VMEM((2 * tq, LANES), BF16),
                        pltpu.VMEM((2, 2 * tq, tk), F32),
                        pltpu.VMEM((2 * tq, LANES), F32),
                        pltpu.VMEM((2 * tq, 2 * LANES), F32)])
    return pl.pallas_call(
        functools.partial(_flash_kernel, tq=tq, tk=tk, n_keys=n_keys, mode=mode, lam_init=lam_init),
        grid_spec=grid_spec,
        out_shape=jax.ShapeDtypeStruct((b, n_q, ncols * LANES), BF16),
        compiler_params=_cparams(("parallel", "parallel", "arbitrary")),
        name="flash_%s_%d" % (mode, n_keys),
    )(tbl, *args)


def _na_case_geometry(case, rows):
    r0 = {0: 0, 1: NA_ROWS, 2: rows - NA_ROWS}[case]
    start = min(max(r0 - WIN_ROWS // 2, 0), rows - NA_SPAN)
    return r0, start


def _na_tile_index(case, dr, dk, rows):
    r0, start = _na_case_geometry(case, rows)
    r, kr = r0 + dr, start + dk
    rs = min(max(r - WIN_ROWS // 2, 0), rows - WIN_ROWS)
    if rs <= kr < rs + WIN_ROWS:
        return kr - r + WIN_ROWS
    return 0


def _na_kernel(q_ref, k_ref, v_ref, kc_ref, vc_ref, tl_ref, tr_ref, o_ref, bias_ref, *, rows):
    nblk = rows // NA_ROWS
    bq = NA_ROWS * GRID_W
    bk = NA_SPAN * GRID_W
    for hh in range(2):
        for case in range(3):
            for dr in range(NA_ROWS):
                for dkp in range(NA_SPAN // 2):
                    ia = _na_tile_index(case, dr, 2 * dkp, rows)
                    ib = _na_tile_index(case, dr, 2 * dkp + 1, rows)
                    bias_ref[hh, case, dr * GRID_W:(dr + 1) * GRID_W, dkp * LANES:(dkp + 1) * LANES] = (
                        tl_ref[hh, ia] + tr_ref[hh, ib])

    lane = lax.broadcasted_iota(jnp.int32, (1, LANES), 1)
    lo = lane < HEAD_DIM
    kctx = kc_ref[0]
    vctx = vc_ref[0]
    nt = (((1,), (1,)), ((), ()))

    def body(i, carry):
        r0 = i * NA_ROWS
        start = jnp.clip(r0 - WIN_ROWS // 2, 0, rows - NA_SPAN)
        case = jnp.where(i == 0, 0, jnp.where(i == nblk - 1, 2, 1))
        qoff = pl.multiple_of(i * bq, bq)
        koff = pl.multiple_of(start * GRID_W, GRID_W)
        qb = q_ref[0, pl.ds(qoff, bq), :]
        ks = k_ref[0, pl.ds(koff, bk), :]
        vs = v_ref[0, pl.ds(koff, bk), :]
        zero = jnp.zeros_like(qb)
        outs = []
        for hh in range(2):
            qm = jnp.where(lo, qb, zero) if hh == 0 else jnp.where(lo, zero, qb)
            s_loc = lax.dot_general(qm, ks, nt, preferred_element_type=F32) + bias_ref[hh, case]
            s_ctx = lax.dot_general(qm, kctx, nt, preferred_element_type=F32)
            m = jnp.maximum(jnp.max(s_loc, axis=1, keepdims=True), jnp.max(s_ctx, axis=1, keepdims=True))
            p_loc = jnp.exp2(s_loc - m)
            p_ctx = jnp.exp2(s_ctx - m)
            l = jnp.sum(p_loc, axis=1, keepdims=True) + jnp.sum(p_ctx, axis=1, keepdims=True)
            o = (jnp.dot(p_loc.astype(BF16), vs, preferred_element_type=F32)
                 + jnp.dot(p_ctx.astype(BF16), vctx, preferred_element_type=F32))
            outs.append(o / l)
        o_ref[0, pl.ds(qoff, bq), :] = jnp.where(lo, outs[0], outs[1]).astype(o_ref.dtype)
        return carry

    lax.fori_loop(0, nblk, body, 0)


def _na(qkv, tl, tr, *, n_lat, n_ctx, npairs, qc0, kc0, vc0):
    b = qkv.shape[0]
    rows = n_lat // GRID_W
    cb0 = n_lat // n_ctx
    nt = tl.shape[1]
    return pl.pallas_call(
        functools.partial(_na_kernel, rows=rows),
        grid=(b, npairs),
        in_specs=[pl.BlockSpec((1, n_lat, LANES), lambda bi, j: (bi, 0, qc0 + j)),
                  pl.BlockSpec((1, n_lat, LANES), lambda bi, j: (bi, 0, kc0 + j)),
                  pl.BlockSpec((1, n_lat, LANES), lambda bi, j: (bi, 0, vc0 + j)),
                  pl.BlockSpec((1, n_ctx, LANES), lambda bi, j: (bi, cb0, kc0 + j)),
                  pl.BlockSpec((1, n_ctx, LANES), lambda bi, j: (bi, cb0, vc0 + j)),
                  pl.BlockSpec((2, nt, GRID_W, LANES), lambda bi, j: (j, 0, 0, 0)),
                  pl.BlockSpec((2, nt, GRID_W, LANES), lambda bi, j: (j, 0, 0, 0))],
        out_specs=pl.BlockSpec((1, n_lat, LANES), lambda bi, j: (bi, 0, j)),
        out_shape=jax.ShapeDtypeStruct((b, n_lat, npairs * LANES), BF16),
        scratch_shapes=[pltpu.VMEM((2, 3, NA_ROWS * GRID_W, NA_SPAN * GRID_W), F32)],
        compiler_params=_cparams(("parallel", "arbitrary")),
        name="neighbourhood_attn",
    )(qkv, qkv, qkv, qkv, qkv, tl, tr)


def _na_bias_tiles(rpb):
    h = rpb.shape[0]
    cols = np.arange(GRID_W)
    cs = np.clip(cols - WIN_COLS // 2, 0, GRID_W - WIN_COLS)
    kc = cols[None, :]
    valid = (kc >= cs[:, None]) & (kc < cs[:, None] + WIN_COLS)
    ci = np.clip(kc - cols[:, None] + WIN_COLS - 1, 0, 2 * WIN_COLS - 2)
    t = jnp.where(jnp.asarray(valid)[None, None], rpb[:, :, ci].astype(F32) * LOG2E, NEG)
    t = jnp.concatenate([jnp.full((h, 1, GRID_W, GRID_W), NEG, F32), t], axis=1)
    z = jnp.zeros_like(t)
    return jnp.concatenate([t, z], axis=-1), jnp.concatenate([z, t], axis=-1)


def _outproj_kernel(*refs, n_parts, has_ctx, lat_tiles):
    refs = list(refs)
    parts = [refs.pop(0) for _ in range(n_parts)]
    octx_ref = refs.pop(0) if has_ctx else None
    w_ref, x_ref, gm_ref, ng_ref, sh_ref, sc_ref, rw_ref, xo_ref, hp_ref, lg_ref, proj_ref = refs

    def lat():
        acc = None
        off = 0
        for p in parts:
            wdt = p.shape[-1]
            t = jnp.dot(p[0], w_ref[off:off + wdt, :], preferred_element_type=F32)
            acc = t if acc is None else acc + t
            off += wdt
        proj_ref[...] = acc

    if has_ctx:
        is_ctx = pl.program_id(1) >= lat_tiles
        pl.when(jnp.logical_not(is_ctx))(lat)

        @pl.when(is_ctx)
        def _():
            proj_ref[...] = jnp.dot(octx_ref[0], w_ref[...], preferred_element_type=F32)
    else:
        lat()

    x = x_ref[0] + gm_ref[0] * proj_ref[...]
    xo_ref[0] = x
    ms = jnp.mean(x * x, axis=-1, keepdims=True)
    h = x * lax.rsqrt(ms + EPS) * ng_ref[...]
    h = h * (1.0 + sc_ref[0]) + sh_ref[0]
    _store_rowchunks(hp_ref, (0,), _pack_bf16_pairs(h))
    lg_ref[...] = lax.dot_general(rw_ref[...], h, (((1,), (1,)), ((), ())), preferred_element_type=F32,
                                  precision=lax.Precision.HIGHEST)


def _outproj(parts, octx, w, xa, mods, gain, rw_t, *, n_lat, n_rows):
    b, _, d = xa.shape
    lat_tiles = n_lat // TM
    nt = n_rows // TM
    has_ctx = octx is not None

    def mod_map(chunk):
        return lambda bi, i: (jnp.where(i >= lat_tiles, b, bi) * 6 + chunk, 0, 0)

    in_specs = [pl.BlockSpec((1, TM, p.shape[-1]), lambda bi, i: (bi, jnp.minimum(i, lat_tiles - 1), 0))
                for p in parts]
    args = list(parts)
    if has_ctx:
        in_specs.append(pl.BlockSpec((1, TM, d), lambda bi, i: (bi, 0, 0)))
        args.append(octx)
    in_specs += [pl.BlockSpec((d, d), lambda bi, i: (0, 0)),
                 pl.BlockSpec((1, TM, d), lambda bi, i: (bi, i, 0)),
                 pl.BlockSpec((1, 1, d), mod_map(2)),
                 pl.BlockSpec((1, d), lambda bi, i: (0, 0)),
                 pl.BlockSpec((1, 1, d), mod_map(3)),
                 pl.BlockSpec((1, 1, d), mod_map(4)),
                 pl.BlockSpec((N_EXPERTS, d), lambda bi, i: (0, 0))]
    args += [w, xa, mods, gain, mods, mods, rw_t]
    return pl.pallas_call(
        functools.partial(_outproj_kernel, n_parts=len(parts), has_ctx=has_ctx, lat_tiles=lat_tiles),
        grid=(b, nt),
        in_specs=in_specs,
        out_specs=[pl.BlockSpec((1, TM, d), lambda bi, i: (bi, i, 0)),
                   pl.BlockSpec((1, TM * ROW_CH, LANES), lambda bi, i: (bi, i, 0)),
                   pl.BlockSpec((N_EXPERTS, TM), lambda bi, i: (0, bi * nt + i))],
        out_shape=[jax.ShapeDtypeStruct((b, n_rows, d), F32),
                   jax.ShapeDtypeStruct((b, n_rows * ROW_CH, LANES), U32),
                   jax.ShapeDtypeStruct((N_EXPERTS, b * n_rows), F32)],
        scratch_shapes=[pltpu.VMEM((TM, d), F32)],
        compiler_params=_cparams(("parallel", "arbitrary")),
        name="outproj_ffnnorm",
    )(*args)


def _router_kernel(lg_ref, rb_ref, tri_ref, idx_ref, gate_ref, pos_ref, cnt_ref, run_ref):
    @pl.when(pl.program_id(0) == 0)
    def _():
        run_ref[...] = jnp.zeros(run_ref.shape, F32)

    scores = jax.nn.sigmoid(lg_ref[...])
    work = scores + rb_ref[...]
    eidx = lax.broadcasted_iota(jnp.int32, work.shape, 0)
    hits, idx_rows, sel_rows = [], [], []
    for _ in range(TOP_K):
        mx = jnp.max(work, axis=0, keepdims=True)
        first = jnp.min(jnp.where(work == mx, eidx, N_EXPERTS), axis=0, keepdims=True)
        hit = eidx == first
        hits.append(hit)
        idx_rows.append(first)
        sel_rows.append(jnp.sum(jnp.where(hit, scores, 0.0), axis=0, keepdims=True))
        work = jnp.where(hit, NEG, work)
    mask = jnp.zeros(work.shape, F32)
    for hit in hits:
        mask = mask + hit.astype(F32)
    denom = sel_rows[0]
    for r in sel_rows[1:]:
        denom = denom + r
    csum = jnp.dot(mask.astype(BF16), tri_ref[...], preferred_element_type=F32)
    tm = mask.shape[1]
    posall = run_ref[:, 0:1] + csum - mask
    pos_rows = [jnp.sum(jnp.where(hit, posall, 0.0), axis=0, keepdims=True) for hit in hits]
    run_ref[...] = run_ref[...] + csum[:, tm - 1:tm]
    idx_ref[...] = jnp.concatenate(idx_rows, axis=0)
    gate_ref[...] = jnp.concatenate(sel_rows, axis=0) / denom * ROUTED_SCALE
    pos_ref[...] = jnp.concatenate(pos_rows, axis=0).astype(jnp.int32)
    cnt_ref[...] = run_ref[...]


def _router(logits_t, rb):
    e, t = logits_t.shape
    tri = jnp.asarray(np.triu(np.ones((TM, TM), np.float32)), BF16)
    return pl.pallas_call(
        _router_kernel,
        grid=(t // TM,),
        in_specs=[pl.BlockSpec((e, TM), lambda i: (0, i)),
                  pl.BlockSpec((e, 1), lambda i: (0, 0)),
                  pl.BlockSpec((TM, TM), lambda i: (0, 0))],
        out_specs=[pl.BlockSpec((TOP_K, TM), lambda i: (0, i)),
                   pl.BlockSpec((TOP_K, TM), lambda i: (0, i)),
                   pl.BlockSpec((TOP_K, TM), lambda i: (0, i)),
                   pl.BlockSpec((e, LANES), lambda i: (0, 0))],
        out_shape=[jax.ShapeDtypeStruct((TOP_K, t), jnp.int32),
                   jax.ShapeDtypeStruct((TOP_K, t), F32),
                   jax.ShapeDtypeStruct((TOP_K, t), jnp.int32),
                   jax.ShapeDtypeStruct((e, LANES), F32)],
        scratch_shapes=[pltpu.VMEM((e, LANES), F32)],
        compiler_params=_cparams(("arbitrary",)),
        name="router_topk",
    )(logits_t, rb, tri)


def _dispatch_kernel(fill_ref, dest_ref, h_ref, xs_hbm, zbuf, fsem, sem):
    i = pl.program_id(0)

    def fill_copy(e):
        start = pl.multiple_of(fill_ref[0, e] * ROW_CH, MOE_BLOCK * ROW_CH)
        return pltpu.make_async_copy(zbuf, xs_hbm.at[pl.ds(start, MOE_BLOCK * ROW_CH), :], fsem)

    @pl.when(i == 0)
    def _():
        zbuf[...] = jnp.zeros(zbuf.shape, zbuf.dtype)
        for e in range(fill_ref.shape[1]):
            pl.when(fill_ref[1, e] != 0)(lambda e=e: fill_copy(e).start())
        for e in range(fill_ref.shape[1]):
            pl.when(fill_ref[1, e] != 0)(lambda e=e: fill_copy(e).wait())

    for tl0 in range(0, MOE_TT, DMA_BATCH):
        ds = [[dest_ref[0, 0, k * MOE_TT + tl0 + u] for k in range(TOP_K)] for u in range(DMA_BATCH)]
        for u in range(DMA_BATCH):
            src = h_ref.at[pl.ds(ROW_CH * (tl0 + u), ROW_CH), :]
            for k in range(TOP_K):
                d = pl.multiple_of(ds[u][k] * ROW_CH, ROW_CH)
                pltpu.make_async_copy(src, xs_hbm.at[pl.ds(d, ROW_CH), :], sem).start()
    for k in range(TOP_K):
        pltpu.make_async_copy(h_ref, xs_hbm.at[pl.ds(0, MOE_TT * ROW_CH), :], sem).wait()


def _dispatch(h2p, dest_tiles, fill, n_pad):
    grid_spec = pltpu.PrefetchScalarGridSpec(
        num_scalar_prefetch=1,
        grid=(h2p.shape[0] // (MOE_TT * ROW_CH),),
        in_specs=[pl.BlockSpec((1, 1, TOP_K * MOE_TT), lambda i, f: (i, 0, 0), memory_space=pltpu.SMEM),
                  pl.BlockSpec((MOE_TT * ROW_CH, LANES), lambda i, f: (i, 0))],
        out_specs=pl.BlockSpec(memory_space=pl.ANY),
        scratch_shapes=[pltpu.VMEM((MOE_BLOCK * ROW_CH, LANES), U32),
                        pltpu.SemaphoreType.DMA,
                        pltpu.SemaphoreType.DMA])
    return pl.pallas_call(
        _dispatch_kernel,
        grid_spec=grid_spec,
        out_shape=jax.ShapeDtypeStruct((n_pad * ROW_CH, LANES), U32),
        compiler_params=_cparams(("arbitrary",)),
        name="moe_dispatch",
    )(fill, dest_tiles, h2p)


def _expert_kernel(be_ref, nu_ref, xs_ref, wg_ref, wu_ref, wd_ref, ys_ref, wgb, wub, wdb):
    i = pl.program_id(0)

    @pl.when(i >= nu_ref[0])
    def _():
        ys_ref[...] = jnp.zeros(ys_ref.shape, ys_ref.dtype)

    @pl.when(i < nu_ref[0])
    def _():
        e = be_ref[i]
        prev = be_ref[jnp.maximum(i - 1, 0)]

        @pl.when(jnp.logical_or(i == 0, e != prev))
        def _():
            wgb[...] = wg_ref[0].astype(BF16)
            wub[...] = wu_ref[0].astype(BF16)
            wdb[...] = wd_ref[0].astype(BF16)

        xb = _load_rowchunks(xs_ref, (), MOE_BLOCK, BF16)
        g = jnp.dot(xb, wgb[...], preferred_element_type=F32)
        u = jnp.dot(xb, wub[...], preferred_element_type=F32)
        hid = (g * jax.nn.sigmoid(g)) * u
        y = jnp.dot(hid.astype(BF16), wdb[...], preferred_element_type=F32)
        _store_rowchunks(ys_ref, (), _pack_bf16_pairs(y))


def _experts(xs, blk_expert, n_used, wg, wu, wd):
    rows = xs.shape[0]
    _, d, f = wg.shape
    nblk = rows // (MOE_BLOCK * ROW_CH)

    def row_map(i, be, nu):
        return (jnp.minimum(i, nu[0] - 1), 0)

    grid_spec = pltpu.PrefetchScalarGridSpec(
        num_scalar_prefetch=2,
        grid=(nblk,),
        in_specs=[pl.BlockSpec((MOE_BLOCK * ROW_CH, LANES), row_map),
                  pl.BlockSpec((1, d, f), lambda i, be, nu: (be[i], 0, 0)),
                  pl.BlockSpec((1, d, f), lambda i, be, nu: (be[i], 0, 0)),
                  pl.BlockSpec((1, f, d), lambda i, be, nu: (be[i], 0, 0))],
        out_specs=pl.BlockSpec((MOE_BLOCK * ROW_CH, LANES), lambda i, be, nu: (i, 0)),
        scratch_shapes=[pltpu.VMEM((d, f), BF16), pltpu.VMEM((d, f), BF16), pltpu.VMEM((f, d), BF16)])
    return pl.pallas_call(
        _expert_kernel,
        grid_spec=grid_spec,
        out_shape=jax.ShapeDtypeStruct((rows, LANES), U32),
        compiler_params=_cparams(("arbitrary",)),
        name="moe_experts",
    )(blk_expert, n_used, xs, wg, wu, wd)


def _combine_kernel(dest_ref, ys_hbm, gate_ref, hp_ref, x_ref, gf_ref, sg_ref, su_ref, sd_ref, fn_ref,
                    o_ref, buf, ysum, sem, *, final):
    for tl0 in range(0, MOE_TT, DMA_BATCH):
        ds = [[dest_ref[0, 0, k * MOE_TT + tl0 + u] for k in range(TOP_K)] for u in range(DMA_BATCH)]
        for u in range(DMA_BATCH):
            for k in range(TOP_K):
                d = pl.multiple_of(ds[u][k] * ROW_CH, ROW_CH)
                pltpu.make_async_copy(ys_hbm.at[pl.ds(d, ROW_CH), :],
                                      buf.at[k, pl.ds(ROW_CH * (tl0 + u), ROW_CH), :], sem).start()

    xb = _load_rowchunks(hp_ref, (), MOE_TT, BF16)
    g = jnp.dot(xb, sg_ref[...], preferred_element_type=F32)
    u = jnp.dot(xb, su_ref[...], preferred_element_type=F32)
    y = jnp.dot(((g * jax.nn.sigmoid(g)) * u).astype(BF16), sd_ref[...], preferred_element_type=F32)

    for k in range(TOP_K):
        pltpu.make_async_copy(ys_hbm.at[pl.ds(0, MOE_TT * ROW_CH), :], buf.at[k], sem).wait()
    gates = gate_ref[...]
    lo_sum = None
    hi_sum = None
    for k in range(TOP_K):
        lo, hi = _unpack_bf16_pairs(buf[k])
        gk = gates[:, k:k + 1]
        lo_sum = gk * lo if lo_sum is None else lo_sum + gk * lo
        hi_sum = gk * hi if hi_sum is None else hi_sum + gk * hi
    ysum[0] = lo_sum
    ysum[1] = hi_sum
    routed = jnp.concatenate([ysum[half, pl.ds(c, MOE_TT, stride=ROW_CH), :]
                              for half in range(2) for c in range(ROW_CH)], axis=1)
    x = x_ref[...] + gf_ref[0] * (y + routed)
    if final:
        ms = jnp.mean(x * x, axis=-1, keepdims=True)
        x = x * lax.rsqrt(ms + EPS) * fn_ref[...]
    o_ref[...] = x


def _combine(ys, dest_tiles, gates_rep, h2p, x_new, mods, sg, su, sd, fn, *, n_batch, n_rows, n_lat, final):
    t, d = x_new.shape
    f = sg.shape[1]
    tiles_b = n_rows // MOE_TT
    lat_tiles = n_lat // MOE_TT

    def mod_map(i):
        return (jnp.where(i % tiles_b >= lat_tiles, n_batch, i // tiles_b) * 6 + 5, 0, 0)

    grid_spec = pltpu.PrefetchScalarGridSpec(
        num_scalar_prefetch=0,
        grid=(t // MOE_TT,),
        in_specs=[pl.BlockSpec((1, 1, TOP_K * MOE_TT), lambda i: (i, 0, 0), memory_space=pltpu.SMEM),
                  pl.BlockSpec(memory_space=pl.ANY),
                  pl.BlockSpec((MOE_TT * ROW_CH, TOP_K), lambda i: (i, 0)),
                  pl.BlockSpec((MOE_TT * ROW_CH, LANES), lambda i: (i, 0)),
                  pl.BlockSpec((MOE_TT, d), lambda i: (i, 0)),
                  pl.BlockSpec((1, 1, d), mod_map),
                  pl.BlockSpec((d, f), lambda i: (0, 0)),
                  pl.BlockSpec((d, f), lambda i: (0, 0)),
                  pl.BlockSpec((f, d), lambda i: (0, 0)),
                  pl.BlockSpec((1, d), lambda i: (0, 0))],
        out_specs=pl.BlockSpec((MOE_TT, d), lambda i: (i, 0)),
        scratch_shapes=[pltpu.VMEM((TOP_K, MOE_TT * ROW_CH, LANES), U32),
                        pltpu.VMEM((2, MOE_TT * ROW_CH, LANES), F32),
                        pltpu.SemaphoreType.DMA])
    return pl.pallas_call(
        functools.partial(_combine_kernel, final=final),
        grid_spec=grid_spec,
        out_shape=jax.ShapeDtypeStruct((t, d), F32),
        compiler_params=_cparams(("arbitrary",)),
        name="moe_combine",
    )(dest_tiles, ys, gates_rep, h2p, x_new, mods, sg, su, sd, fn)


def _rope_tables(n_lat, n_ctx):
    t = np.arange(n_lat)
    row = (t // GRID_W).astype(np.float32)
    col = (t % GRID_W).astype(np.float32)
    npairs = HEAD_DIM // 4
    inv_freq = jnp.asarray(ROPE_THETA, F32) ** (-jnp.arange(npairs, dtype=F32) / npairs)
    ang = jnp.concatenate([jnp.asarray(row)[:, None] * inv_freq, jnp.asarray(col)[:, None] * inv_freq], axis=-1)
    cos = jnp.repeat(jnp.cos(ang), 2, axis=-1)
    sin = jnp.repeat(jnp.sin(ang), 2, axis=-1)
    sign = jnp.asarray(np.tile(np.array([-1.0, 1.0], np.float32), HEAD_DIM // 2))
    cosf = jnp.tile(cos, (1, LANES // HEAD_DIM))
    sins = jnp.tile(sin * sign, (1, LANES // HEAD_DIM))
    cosf = jnp.concatenate([cosf, jnp.ones((n_ctx, LANES), F32)], axis=0)
    sins = jnp.concatenate([sins, jnp.zeros((n_ctx, LANES), F32)], axis=0)
    return cosf, sins


def _moe_block(logits_t, h2p, x_new, mods_l, rb, wg, wu, wd, sg, su, sd, fn, *, n_batch, n_rows, n_lat, final):
    t = x_new.shape[0]
    idx, gates, pos, cnt = _router(logits_t, rb.reshape(N_EXPERTS, 1))
    counts = cnt[:, 0].astype(jnp.int32)
    padded = (counts + MOE_BLOCK - 1) // MOE_BLOCK * MOE_BLOCK
    pad_end = jnp.cumsum(padded)
    pad_start = pad_end - padded
    n_blocks = (t * TOP_K + N_EXPERTS * (MOE_BLOCK - 1) + MOE_BLOCK - 1) // MOE_BLOCK
    n_pad = n_blocks * MOE_BLOCK
    experts = jnp.arange(N_EXPERTS, dtype=jnp.int32)
    dest = jnp.sum(jnp.where(idx[:, :, None] == experts, pad_start, 0), axis=-1) + pos
    dest_tiles = dest.reshape(TOP_K, t // MOE_TT, MOE_TT).transpose(1, 0, 2).reshape(t // MOE_TT, 1, TOP_K * MOE_TT)
    blk_start = jnp.arange(n_blocks, dtype=jnp.int32) * MOE_BLOCK
    blk_expert = jnp.minimum(jnp.sum((pad_end[None, :] <= blk_start[:, None]).astype(jnp.int32), axis=1),
                             N_EXPERTS - 1)
    n_used = (pad_end[-1:] // MOE_BLOCK).astype(jnp.int32)
    tail = n_used + jnp.arange(N_EXPERTS + 1, dtype=jnp.int32)
    fill = jnp.stack([jnp.concatenate([jnp.maximum(pad_end - MOE_BLOCK, 0), jnp.minimum(tail, n_blocks - 1) * MOE_BLOCK]),
                      jnp.concatenate([counts > 0, tail < n_blocks]).astype(jnp.int32)]).astype(jnp.int32)
    xs = _dispatch(h2p, dest_tiles, fill, n_pad)
    ys = _experts(xs, blk_expert, n_used, wg, wu, wd)
    gates_rep = jnp.repeat(gates.T, ROW_CH, axis=0)
    return _combine(ys, dest_tiles, gates_rep, h2p, x_new, mods_l, sg.astype(BF16), su.astype(BF16),
                    sd.astype(BF16), fn, n_batch=n_batch, n_rows=n_rows, n_lat=n_lat, final=final)


def kernel(x, c, ctx, c_ctx, ada_w, ada_b, norm_mix, norm_ffn, ab_w_in, ab_w_out, na_rpb, gqa_q_gain,
           gqa_k_gain, diff_w_in, diff_w_out, diff_lq1, diff_lk1, diff_lq2, diff_lk2, diff_sub_gain,
           router_w, router_bias, expert_w_gate, expert_w_up, expert_w_down, shared_w_gate, shared_w_up,
           shared_w_down, final_norm):
    b, n, d = x.shape
    n_ctx = ctx.shape[1]
    depth = ada_w.shape[0]
    assert d == 2 * ROW_CH * LANES
    assert depth == 2 and n_ctx == TM and n % (NA_ROWS * GRID_W) == 0 and d % LANES == 0
    assert (n + n_ctx) % FLASH_TK == 0 and n % FLASH_TQ == 0 and n % MOE_TT == 0 and n_ctx % MOE_TT == 0

    cvec = jnp.concatenate([c, c_ctx[None], jnp.zeros((8 - b - 1, d), F32)], axis=0)
    mods = _mods(cvec, ada_w, ada_b)[:, :b + 1].reshape(depth, (b + 1) * 6, 1, d)
    cosf, sins = _rope_tables(n, n_ctx)
    gm = jnp.asarray(np.kron(np.eye(LANES // HEAD_DIM), np.full((HEAD_DIM, HEAD_DIM), 1.0 / HEAD_DIM)), BF16)
    ones = jnp.ones((1, LANES), F32)
    fn = final_norm.reshape(1, d)

    xa = jnp.concatenate([x, ctx], axis=1)

    w = ab_w_in[0]
    kb = [w[:, 2048 + HEAD_DIM * g: 2048 + HEAD_DIM * (g + 1)] for g in range(2)]
    vb = [w[:, 2176 + HEAD_DIM * g: 2176 + HEAD_DIM * (g + 1)] for g in range(2)]
    w0 = jnp.concatenate([w[:, :2048], kb[0], kb[0], kb[1], kb[1], vb[0], vb[0], vb[1], vb[1]],
                         axis=1).astype(BF16)
    plan0 = ([(None, False, True)] * 4 + [(None, False, False)] * 8 + [("q", True, True)] * 4
             + [("k", True, False)] * 2 + [(None, False, False)] * 2)
    qg = jnp.tile(gqa_q_gain[0].reshape(1, HEAD_DIM), (1, LANES // HEAD_DIM))
    kg = jnp.tile(gqa_k_gain[0].reshape(1, HEAD_DIM), (1, LANES // HEAD_DIM))
    qkv = _proj(xa, norm_mix[0].reshape(1, d), mods[0], 0, 1, w0, cosf, sins, gm, qg, kg, plan0, n)

    tl, tr = _na_bias_tiles(na_rpb[0])
    o_na = _na(qkv, tl, tr, n_lat=n, n_ctx=n_ctx, npairs=4, qc0=0, kc0=4, vc0=8)
    tbl_g = jnp.asarray([[12, 13, 14, 15], [16, 16, 17, 17], [18, 18, 19, 19]], jnp.int32)
    o_gqa = _flash(qkv, tbl_g, n_q=n, q_row0=0, n_keys=n + n_ctx, key_row0=0, tq=FLASH_TQ, tk=FLASH_TK)
    tbl_c = jnp.asarray([[0, 1, 2, 3, 12, 13, 14, 15], [4, 5, 6, 7, 16, 16, 17, 17],
                         [8, 9, 10, 11, 18, 18, 19, 19]], jnp.int32)
    o_ctx = _flash(qkv, tbl_c, n_q=n_ctx, q_row0=n, n_keys=n_ctx, key_row0=n, tq=n_ctx, tk=n_ctx)

    s_all = n + n_ctx
    x_new, h2p, lg = _outproj([o_na, o_gqa], o_ctx, ab_w_out[0].astype(BF16), xa, mods[0],
                              norm_ffn[0].reshape(1, d), router_w[0].T, n_lat=n, n_rows=s_all)
    xa = _moe_block(lg, h2p.reshape(-1, LANES), x_new.reshape(b * s_all, d), mods[0], router_bias[0],
                    expert_w_gate[0], expert_w_up[0], expert_w_down[0], shared_w_gate[0], shared_w_up[0],
                    shared_w_down[0], fn, n_batch=b, n_rows=s_all, n_lat=n, final=False).reshape(b, s_all, d)

    lam_init = 0.8 - 0.6 * math.exp(-0.3 * 1)
    plan1 = [(None, True, True)] * 8 + [(None, True, False)] * 8 + [(None, False, False)] * 8
    qkv = _proj(xa, norm_mix[1].reshape(1, d), mods[1], 0, 1, diff_w_in[0].astype(BF16), cosf, sins, gm,
                ones, ones, plan1, n)
    tbl_d = jnp.asarray([list(range(0, 8)), list(range(8, 16)), list(range(16, 24))], jnp.int32)
    dp = [diff_lq1[0].reshape(1, HEAD_DIM), diff_lk1[0].reshape(1, HEAD_DIM),
          diff_lq2[0].reshape(1, HEAD_DIM), diff_lk2[0].reshape(1, HEAD_DIM),
          diff_sub_gain[0].reshape(1, LANES)]
    o_diff = _flash(qkv, tbl_d, n_q=n, q_row0=0, n_keys=n + n_ctx, key_row0=0, tq=FLASH_TQ, tk=FLASH_TK,
                    mode="diff", diff_params=dp, lam_init=lam_init)
    x_new, h2p, lg = _outproj([o_diff], None, diff_w_out[0].astype(BF16), xa, mods[1],
                              norm_ffn[1].reshape(1, d), router_w[1].T, n_lat=n, n_rows=n)
    return _moe_block(lg, h2p.reshape(-1, LANES), x_new.reshape(b * n, d), mods[1], router_bias[1],
                      expert_w_gate[1], expert_w_up[1], expert_w_down[1], shared_w_gate[1], shared_w_up[1],
                      shared_w_down[1], fn, n_batch=b, n_rows=n, n_lat=n, final=True).reshape(b, n, d)
```

```python
import functools
import math

import jax
import jax.numpy as jnp
import numpy as np
from jax import lax
from jax.experimental import pallas as pl
from jax.experimental.pallas import tpu as pltpu

F32 = jnp.float32
BF16 = jnp.bfloat16
U32 = jnp.uint32

LANES = 128
HEAD_DIM = 64
GRID_W = 64
WIN_ROWS = 8
WIN_COLS = 16
ROPE_THETA = 10000.0
EPS = 1e-6
N_EXPERTS = 64
TOP_K = 8
ROUTED_SCALE = 2.5
NEG = -1e30
LOG2E = math.log2(math.e)
Q_SCALE = HEAD_DIM ** -0.5 * LOG2E
VMEM_LIMIT = 56 * 1024 * 1024

TM = 256
NA_ROWS = 8
NA_SPAN = 16
FLASH_TQ = 512
FLASH_TK = 768
MOE_BLOCK = 512
MOE_TT = 128
DMA_BATCH = 2
HI_MASK = 0xFFFF0000


def _cparams(sem):
    return pltpu.CompilerParams(dimension_semantics=sem, vmem_limit_bytes=VMEM_LIMIT)


def _pack_bf16_pairs(x):
    w = x.shape[1] // 2
    bits = lax.bitcast_convert_type(x.astype(BF16).astype(F32), U32)
    return (bits[:, w:] & jnp.uint32(HI_MASK)) | (bits[:, :w] >> 16)


def _unpack_bf16_pairs(p):
    lo = lax.bitcast_convert_type(p << 16, F32)
    hi = lax.bitcast_convert_type(p & jnp.uint32(HI_MASK), F32)
    return lo, hi


ROW_CH = 4


def _store_rowchunks(ref, idx, packed):
    m = packed.shape[0]
    for c in range(ROW_CH):
        ref[idx + (pl.ds(c, m, stride=ROW_CH), slice(None))] = packed[:, c * LANES:(c + 1) * LANES]


def _load_rowchunks(ref, idx, m, dtype):
    planes = [_unpack_bf16_pairs(ref[idx + (pl.ds(c, m, stride=ROW_CH), slice(None))]) for c in range(ROW_CH)]
    return jnp.concatenate([lo.astype(dtype) for lo, _ in planes] + [hi.astype(dtype) for _, hi in planes], axis=1)


def _mods_kernel(c_ref, w_ref, b_ref, o_ref):
    c = c_ref[...]
    s = c * jax.nn.sigmoid(c)
    o_ref[0] = jnp.dot(s.astype(BF16), w_ref[0].astype(BF16), preferred_element_type=F32) + b_ref[0]


def _mods(cvec, ada_w, ada_b):
    depth, d, d6 = ada_w.shape
    tn = 1536
    return pl.pallas_call(
        _mods_kernel,
        grid=(depth, d6 // tn),
        in_specs=[pl.BlockSpec((8, d), lambda l, j: (0, 0)),
                  pl.BlockSpec((1, d, tn), lambda l, j: (l, 0, j)),
                  pl.BlockSpec((1, 1, tn), lambda l, j: (l, 0, j))],
        out_specs=pl.BlockSpec((1, 8, tn), lambda l, j: (l, 0, j)),
        out_shape=jax.ShapeDtypeStruct((depth, 8, d6), F32),
        compiler_params=_cparams(("arbitrary", "arbitrary")),
        name="adaln_mods",
    )(cvec, ada_w, ada_b.reshape(depth, 1, d6))


def _proj_kernel(x_ref, g_ref, sh_ref, sc_ref, w_ref, cos_ref, sin_ref, gm_ref, qg_ref, kg_ref,
                 o_ref, *, plan):
    x = x_ref[0]
    ms = jnp.mean(x * x, axis=-1, keepdims=True)
    h = x * lax.rsqrt(ms + EPS) * g_ref[...]
    h = h * (1.0 + sc_ref[0]) + sh_ref[0]
    y = jnp.dot(h.astype(BF16), w_ref[...], preferred_element_type=F32)
    cosf = cos_ref[...]
    sins = sin_ref[...]
    even = (lax.broadcasted_iota(jnp.int32, (1, LANES), 1) % 2) == 0
    for c, (norm, rope, scale) in enumerate(plan):
        yc = y[:, c * LANES:(c + 1) * LANES]
        if norm:
            ms2 = jnp.dot((yc * yc).astype(BF16), gm_ref[...], preferred_element_type=F32)
            gain = qg_ref[...] if norm == "q" else kg_ref[...]
            yc = yc * lax.rsqrt(ms2 + EPS) * gain
        if rope:
            sw = jnp.where(even, pltpu.roll(yc, LANES - 1, 1), pltpu.roll(yc, 1, 1))
            yc = yc * cosf + sw * sins
        if scale:
            yc = yc * Q_SCALE
        o_ref[0, :, c * LANES:(c + 1) * LANES] = yc.astype(BF16)


def _proj(xa, gain, mods, sh_idx, sc_idx, w, cosf, sins, gm, qg, kg, plan, n_lat):
    b, s, d = xa.shape
    wcols = w.shape[1]
    nt = s // TM
    lat_tiles = n_lat // TM

    def mod_map(chunk):
        return lambda bi, i: (jnp.where(i >= lat_tiles, b, bi) * 6 + chunk, 0, 0)

    return pl.pallas_call(
        functools.partial(_proj_kernel, plan=plan),
        grid=(b, nt),
        in_specs=[pl.BlockSpec((1, TM, d), lambda bi, i: (bi, i, 0)),
                  pl.BlockSpec((1, d), lambda bi, i: (0, 0)),
                  pl.BlockSpec((1, 1, d), mod_map(sh_idx)),
                  pl.BlockSpec((1, 1, d), mod_map(sc_idx)),
                  pl.BlockSpec((d, wcols), lambda bi, i: (0, 0)),
                  pl.BlockSpec((TM, LANES), lambda bi, i: (i, 0)),
                  pl.BlockSpec((TM, LANES), lambda bi, i: (i, 0)),
                  pl.BlockSpec((LANES, LANES), lambda bi, i: (0, 0)),
                  pl.BlockSpec((1, LANES), lambda bi, i: (0, 0)),
                  pl.BlockSpec((1, LANES), lambda bi, i: (0, 0))],
        out_specs=pl.BlockSpec((1, TM, wcols), lambda bi, i: (bi, i, 0)),
        out_shape=jax.ShapeDtypeStruct((b, s, wcols), BF16),
        compiler_params=_cparams(("parallel", "arbitrary")),
        name="norm_inproj",
    )(xa, gain, mods, mods, w, cosf, sins, gm, qg, kg)


def _flash_kernel(tbl_ref, *refs, tq, tk, n_keys, mode, lam_init):
    del tbl_ref
    refs = list(refs)
    q_ref, k_ref, v_ref = refs[:3]
    refs = refs[3:]
    if mode == "diff":
        lq1, lk1, lq2, lk2, sg_ref = refs[:5]
        refs = refs[5:]
    o_ref, qs_ref, s_ref, m_ref, acc_ref = refs

    lane = lax.broadcasted_iota(jnp.int32, (1, LANES), 1)
    lo = lane < HEAD_DIM
    q = q_ref[0]
    zero = jnp.zeros_like(q)
    qs_ref[0:tq, :] = jnp.where(lo, q, zero)
    qs_ref[tq:2 * tq, :] = jnp.where(lo, zero, q)
    m_ref[...] = jnp.full(m_ref.shape, NEG, F32)
    acc_ref[...] = jnp.zeros(acc_ref.shape, F32)
    nb = tk // LANES
    nchunks = n_keys // tk
    ones = jnp.ones((tk, LANES), BF16)

    def qk(slot, j):
        off = pl.multiple_of(j * tk, tk)
        s_ref[slot] = lax.dot_general(qs_ref[...], k_ref[0, pl.ds(off, tk), :], (((1,), (1,)), ((), ())),
                                      preferred_element_type=F32)

    def softmax_pv(slot, j):
        off = pl.multiple_of(j * tk, tk)
        s = s_ref[slot]
        m_prev = m_ref[...]
        m_next = jnp.maximum(m_prev, jnp.max(s, axis=1, keepdims=True))
        alpha = jnp.exp2(m_prev - m_next)
        p = jnp.exp2(s - jnp.concatenate([m_next] * nb, axis=1))
        v1 = jnp.concatenate([v_ref[0, pl.ds(off, tk), :], ones], axis=1)
        acc_ref[...] = (jnp.concatenate([alpha, alpha], axis=1) * acc_ref[...]
                        + jnp.dot(p.astype(BF16), v1, preferred_element_type=F32))
        m_ref[...] = m_next

    qk(0, 0)
    npairs = (nchunks - 1) // 2

    def body(jj, carry):
        j = 2 * jj
        qk(1, j + 1)
        softmax_pv(0, j)
        qk(0, j + 2)
        softmax_pv(1, j + 1)
        return carry

    lax.fori_loop(0, npairs, body, 0)
    if nchunks % 2 == 0:
        qk(1, nchunks - 1)
        softmax_pv(0, nchunks - 2)
        softmax_pv(1, nchunks - 1)
    else:
        softmax_pv(0, nchunks - 1)

    acc = acc_ref[...]
    o = acc[:, 0:LANES] / acc[:, LANES:2 * LANES]
    if mode == "pair":
        out = jnp.where(lo, o[0:tq], o[tq:2 * tq])
    else:
        lam = (jnp.exp(jnp.sum(lq1[...] * lk1[...], axis=1, keepdims=True))
               - jnp.exp(jnp.sum(lq2[...] * lk2[...], axis=1, keepdims=True)) + lam_init)
        dlt = o[0:tq] - lam * o[tq:2 * tq]
        ms = jnp.mean(dlt * dlt, axis=-1, keepdims=True)
        out = dlt * lax.rsqrt(ms + EPS) * sg_ref[...] * (1.0 - lam_init)
    o_ref[0] = out.astype(o_ref.dtype)


def _flash(qkv, tbl, *, n_q, q_row0, n_keys, key_row0, tq, tk, mode="pair", diff_params=None, lam_init=0.0):
    b = qkv.shape[0]
    ncols = tbl.shape[1]
    qb0 = q_row0 // tq
    kb0 = key_row0 // n_keys
    in_specs = [pl.BlockSpec((1, tq, LANES), lambda bi, c, i, t: (bi, qb0 + i, t[0, c])),
                pl.BlockSpec((1, n_keys, LANES), lambda bi, c, i, t: (bi, kb0, t[1, c])),
                pl.BlockSpec((1, n_keys, LANES), lambda bi, c, i, t: (bi, kb0, t[2, c]))]
    args = [qkv, qkv, qkv]
    if mode == "diff":
        in_specs += [pl.BlockSpec((1, HEAD_DIM), lambda bi, c, i, t: (0, 0))] * 4
        in_specs += [pl.BlockSpec((1, LANES), lambda bi, c, i, t: (0, 0))]
        args += list(diff_params)
    grid_spec = pltpu.PrefetchScalarGridSpec(
        num_scalar_prefetch=1,
        grid=(b, ncols, n_q // tq),
        in_specs=in_specs,
        out_specs=pl.BlockSpec((1, tq, LANES), lambda bi, c, i, t: (bi, i, c)),
        scratch_shapes=[pltpu.VMEM((2 * tq, LANES), BF16),
                        pltpu.VMEM((2, 2 * tq, tk), F32),
                        pltpu.VMEM((2 * tq, LANES), F32),
                        pltpu.VMEM((2 * tq, 2 * LANES), F32)])
    return pl.pallas_call(
        functools.partial(_flash_kernel, tq=tq, tk=tk, n_keys=n_keys, mode=mode, lam_init=lam_init),
        grid_spec=grid_spec,
        out_shape=jax.ShapeDtypeStruct((b, n_q, ncols * LANES), BF16),
        compiler_params=_cparams(("parallel", "parallel", "arbitrary")),
        name="flash_%s_%d" % (mode, n_keys),
    )(tbl, *args)


def _na_case_geometry(case, rows):
    r0 = {0: 0, 1: NA_ROWS, 2: rows - NA_ROWS}[case]
    start = min(max(r0 - WIN_ROWS // 2, 0), rows - NA_SPAN)
    return r0, start


def _na_tile_index(case, dr, dk, rows):
    r0, start = _na_case_geometry(case, rows)
    r, kr = r0 + dr, start + dk
    rs = min(max(r - WIN_ROWS // 2, 0), rows - WIN_ROWS)
    if rs <= kr < rs + WIN_ROWS:
        return kr - r + WIN_ROWS
    return 0


def _na_kernel(q_ref, k_ref, v_ref, kc_ref, vc_ref, tl_ref, tr_ref, o_ref, bias_ref, *, rows):
    nblk = rows // NA_ROWS
    bq = NA_ROWS * GRID_W
    bk = NA_SPAN * GRID_W
    for hh in range(2):
        for case in range(3):
            for dr in range(NA_ROWS):
                for dkp in range(NA_SPAN // 2):
                    ia = _na_tile_index(case, dr, 2 * dkp, rows)
                    ib = _na_tile_index(case, dr, 2 * dkp + 1, rows)
                    bias_ref[hh, case, dr * GRID_W:(dr + 1) * GRID_W, dkp * LANES:(dkp + 1) * LANES] = (
                        tl_ref[hh, ia] + tr_ref[hh, ib])

    lane = lax.broadcasted_iota(jnp.int32, (1, LANES), 1)
    lo = lane < HEAD_DIM
    kctx = kc_ref[0]
    vctx = vc_ref[0]
    nt = (((1,), (1,)), ((), ()))

    def body(i, carry):
        r0 = i * NA_ROWS
        start = jnp.clip(r0 - WIN_ROWS // 2, 0, rows - NA_SPAN)
        case = jnp.where(i == 0, 0, jnp.where(i == nblk - 1, 2, 1))
        qoff = pl.multiple_of(i * bq, bq)
        koff = pl.multiple_of(start * GRID_W, GRID_W)
        qb = q_ref[0, pl.ds(qoff, bq), :]
        ks = k_ref[0, pl.ds(koff, bk), :]
        vs = v_ref[0, pl.ds(koff, bk), :]
        zero = jnp.zeros_like(qb)
        outs = []
        for hh in range(2):
            qm = jnp.where(lo, qb, zero) if hh == 0 else jnp.where(lo, zero, qb)
            s_loc = lax.dot_general(qm, ks, nt, preferred_element_type=F32) + bias_ref[hh, case]
            s_ctx = lax.dot_general(qm, kctx, nt, preferred_element_type=F32)
            m = jnp.maximum(jnp.max(s_loc, axis=1, keepdims=True), jnp.max(s_ctx, axis=1, keepdims=True))
            p_loc = jnp.exp2(s_loc - m)
            p_ctx = jnp.exp2(s_ctx - m)
            l = jnp.sum(p_loc, axis=1, keepdims=True) + jnp.sum(p_ctx, axis=1, keepdims=True)
            o = (jnp.dot(p_loc.astype(BF16), vs, preferred_element_type=F32)
                 + jnp.dot(p_ctx.astype(BF16), vctx, preferred_element_type=F32))
            outs.append(o / l)
        o_ref[0, pl.ds(qoff, bq), :] = jnp.where(lo, outs[0], outs[1]).astype(o_ref.dtype)
        return carry

    lax.fori_loop(0, nblk, body, 0)


def _na(qkv, tl, tr, *, n_lat, n_ctx, npairs, qc0, kc0, vc0):
    b = qkv.shape[0]
    rows = n_lat // GRID_W
    cb0 = n_lat // n_ctx
    nt = tl.shape[1]
    return pl.pallas_call(
        functools.partial(_na_kernel, rows=rows),
        grid=(b, npairs),
        in_specs=[pl.BlockSpec((1, n_lat, LANES), lambda bi, j: (bi, 0, qc0 + j)),
                  pl.BlockSpec((1, n_lat, LANES), lambda bi, j: (bi, 0, kc0 + j)),
                  pl.BlockSpec((1, n_lat, LANES), lambda bi, j: (bi, 0, vc0 + j)),
                  pl.BlockSpec((1, n_ctx, LANES), lambda bi, j: (bi, cb0, kc0 + j)),
                  pl.BlockSpec((1, n_ctx, LANES), lambda bi, j: (bi, cb0, vc0 + j)),
                  pl.BlockSpec((2, nt, GRID_W, LANES), lambda bi, j: (j, 0, 0, 0)),
                  pl.BlockSpec((2, nt, GRID_W, LANES), lambda bi, j: (j, 0, 0, 0))],
        out_specs=pl.BlockSpec((1, n_lat, LANES), lambda bi, j: (bi, 0, j)),
        out_shape=jax.ShapeDtypeStruct((b, n_lat, npairs * LANES), BF16),
        scratch_shapes=[pltpu.VMEM((2, 3, NA_ROWS * GRID_W, NA_SPAN * GRID_W), F32)],
        compiler_params=_cparams(("parallel", "arbitrary")),
        name="neighbourhood_attn",
    )(qkv, qkv, qkv, qkv, qkv, tl, tr)


def _na_bias_tiles(rpb):
    h = rpb.shape[0]
    cols = np.arange(GRID_W)
    cs = np.clip(cols - WIN_COLS // 2, 0, GRID_W - WIN_COLS)
    kc = cols[None, :]
    valid = (kc >= cs[:, None]) & (kc < cs[:, None] + WIN_COLS)
    ci = np.clip(kc - cols[:, None] + WIN_COLS - 1, 0, 2 * WIN_COLS - 2)
    t = jnp.where(jnp.asarray(valid)[None, None], rpb[:, :, ci].astype(F32) * LOG2E, NEG)
    t = jnp.concatenate([jnp.full((h, 1, GRID_W, GRID_W), NEG, F32), t], axis=1)
    z = jnp.zeros_like(t)
    return jnp.concatenate([t, z], axis=-1), jnp.concatenate([z, t], axis=-1)


def _outproj_kernel(*refs, n_parts, has_ctx, lat_tiles):
    refs = list(refs)
    parts = [refs.pop(0) for _ in range(n_parts)]
    octx_ref = refs.pop(0) if has_ctx else None
    w_ref, x_ref, gm_ref, ng_ref, sh_ref, sc_ref, rw_ref, xo_ref, hp_ref, lg_ref, proj_ref = refs

    def lat():
        acc = None
        off = 0
        for p in parts:
            wdt = p.shape[-1]
            t = jnp.dot(p[0], w_ref[off:off + wdt, :], preferred_element_type=F32)
            acc = t if acc is None else acc + t
            off += wdt
        proj_ref[...] = acc

    if has_ctx:
        is_ctx = pl.program_id(1) >= lat_tiles
        pl.when(jnp.logical_not(is_ctx))(lat)

        @pl.when(is_ctx)
        def _():
            proj_ref[...] = jnp.dot(octx_ref[0], w_ref[...], preferred_element_type=F32)
    else:
        lat()

    x = x_ref[0] + gm_ref[0] * proj_ref[...]
    xo_ref[0] = x
    ms = jnp.mean(x * x, axis=-1, keepdims=True)
    h = x * lax.rsqrt(ms + EPS) * ng_ref[...]
    h = h * (1.0 + sc_ref[0]) + sh_ref[0]
    _store_rowchunks(hp_ref, (0,), _pack_bf16_pairs(h))
    lg_ref[...] = lax.dot_general(rw_ref[...], h, (((1,), (1,)), ((), ())), preferred_element_type=F32,
                                  precision=lax.Precision.HIGHEST)


def _outproj(parts, octx, w, xa, mods, gain, rw_t, *, n_lat, n_rows):
    b, _, d = xa.shape
    lat_tiles = n_lat // TM
    nt = n_rows // TM
    has_ctx = octx is not None

    def mod_map(chunk):
        return lambda bi, i: (jnp.where(i >= lat_tiles, b, bi) * 6 + chunk, 0, 0)

    in_specs = [pl.BlockSpec((1, TM, p.shape[-1]), lambda bi, i: (bi, jnp.minimum(i, lat_tiles - 1), 0))
                for p in parts]
    args = list(parts)
    if has_ctx:
        in_specs.append(pl.BlockSpec((1, TM, d), lambda bi, i: (bi, 0, 0)))
        args.append(octx)
    in_specs += [pl.BlockSpec((d, d), lambda bi, i: (0, 0)),
                 pl.BlockSpec((1, TM, d), lambda bi, i: (bi, i, 0)),
                 pl.BlockSpec((1, 1, d), mod_map(2)),
                 pl.BlockSpec((1, d), lambda bi, i: (0, 0)),
                 pl.BlockSpec((1, 1, d), mod_map(3)),
                 pl.BlockSpec((1, 1, d), mod_map(4)),
                 pl.BlockSpec((N_EXPERTS, d), lambda bi, i: (0, 0))]
    args += [w, xa, mods, gain, mods, mods, rw_t]
    return pl.pallas_call(
        functools.partial(_outproj_kernel, n_parts=len(parts), has_ctx=has_ctx, lat_tiles=lat_tiles),
        grid=(b, nt),
        in_specs=in_specs,
        out_specs=[pl.BlockSpec((1, TM, d), lambda bi, i: (bi, i, 0)),
                   pl.BlockSpec((1, TM * ROW_CH, LANES), lambda bi, i: (bi, i, 0)),
                   pl.BlockSpec((N_EXPERTS, TM), lambda bi, i: (0, bi * nt + i))],
        out_shape=[jax.ShapeDtypeStruct((b, n_rows, d), F32),
                   jax.ShapeDtypeStruct((b, n_rows * ROW_CH, LANES), U32),
                   jax.ShapeDtypeStruct((N_EXPERTS, b * n_rows), F32)],
        scratch_shapes=[pltpu.VMEM((TM, d), F32)],
        compiler_params=_cparams(("parallel", "arbitrary")),
        name="outproj_ffnnorm",
    )(*args)


def _router_kernel(lg_ref, rb_ref, tri_ref, idx_ref, gate_ref, pos_ref, cnt_ref, run_ref):
    @pl.when(pl.program_id(0) == 0)
    def _():
        run_ref[...] = jnp.zeros(run_ref.shape, F32)

    scores = jax.nn.sigmoid(lg_ref[...])
    work = scores + rb_ref[...]
    eidx = lax.broadcasted_iota(jnp.int32, work.shape, 0)
    hits, idx_rows, sel_rows = [], [], []
    for _ in range(TOP_K):
        mx = jnp.max(work, axis=0, keepdims=True)
        first = jnp.min(jnp.where(work == mx, eidx, N_EXPERTS), axis=0, keepdims=True)
        hit = eidx == first
        hits.append(hit)
        idx_rows.append(first)
        sel_rows.append(jnp.sum(jnp.where(hit, scores, 0.0), axis=0, keepdims=True))
        work = jnp.where(hit, NEG, work)
    mask = jnp.zeros(work.shape, F32)
    for hit in hits:
        mask = mask + hit.astype(F32)
    denom = sel_rows[0]
    for r in sel_rows[1:]:
        denom = denom + r
    csum = jnp.dot(mask.astype(BF16), tri_ref[...], preferred_element_type=F32)
    tm = mask.shape[1]
    posall = run_ref[:, 0:1] + csum - mask
    pos_rows = [jnp.sum(jnp.where(hit, posall, 0.0), axis=0, keepdims=True) for hit in hits]
    run_ref[...] = run_ref[...] + csum[:, tm - 1:tm]
    idx_ref[...] = jnp.concatenate(idx_rows, axis=0)
    gate_ref[...] = jnp.concatenate(sel_rows, axis=0) / denom * ROUTED_SCALE
    pos_ref[...] = jnp.concatenate(pos_rows, axis=0).astype(jnp.int32)
    cnt_ref[...] = run_ref[...]


def _router(logits_t, rb):
    e, t = logits_t.shape
    tri = jnp.asarray(np.triu(np.ones((TM, TM), np.float32)), BF16)
    return pl.pallas_call(
        _router_kernel,
        grid=(t // TM,),
        in_specs=[pl.BlockSpec((e, TM), lambda i: (0, i)),
                  pl.BlockSpec((e, 1), lambda i: (0, 0)),
                  pl.BlockSpec((TM, TM), lambda i: (0, 0))],
        out_specs=[pl.BlockSpec((TOP_K, TM), lambda i: (0, i)),
                   pl.BlockSpec((TOP_K, TM), lambda i: (0, i)),
                   pl.BlockSpec((TOP_K, TM), lambda i: (0, i)),
                   pl.BlockSpec((e, LANES), lambda i: (0, 0))],
        out_shape=[jax.ShapeDtypeStruct((TOP_K, t), jnp.int32),
                   jax.ShapeDtypeStruct((TOP_K, t), F32),
                   jax.ShapeDtypeStruct((TOP_K, t), jnp.int32),
                   jax.ShapeDtypeStruct((e, LANES), F32)],
        scratch_shapes=[pltpu.VMEM((e, LANES), F32)],
        compiler_params=_cparams(("arbitrary",)),
        name="router_topk",
    )(logits_t, rb, tri)


def _dispatch_kernel(fill_ref, dest_ref, h_ref, xs_hbm, zbuf, fsem, sem):
    i = pl.program_id(0)

    def fill_copy(e):
        start = pl.multiple_of(fill_ref[0, e] * ROW_CH, MOE_BLOCK * ROW_CH)
        return pltpu.make_async_copy(zbuf, xs_hbm.at[pl.ds(start, MOE_BLOCK * ROW_CH), :], fsem)

    @pl.when(i == 0)
    def _():
        zbuf[...] = jnp.zeros(zbuf.shape, zbuf.dtype)
        for e in range(fill_ref.shape[1]):
            pl.when(fill_ref[1, e] != 0)(lambda e=e: fill_copy(e).start())
        for e in range(fill_ref.shape[1]):
            pl.when(fill_ref[1, e] != 0)(lambda e=e: fill_copy(e).wait())

    for tl0 in range(0, MOE_TT, DMA_BATCH):
        ds = [[dest_ref[0, 0, k * MOE_TT + tl0 + u] for k in range(TOP_K)] for u in range(DMA_BATCH)]
        for u in range(DMA_BATCH):
            src = h_ref.at[pl.ds(ROW_CH * (tl0 + u), ROW_CH), :]
            for k in range(TOP_K):
                d = pl.multiple_of(ds[u][k] * ROW_CH, ROW_CH)
                pltpu.make_async_copy(src, xs_hbm.at[pl.ds(d, ROW_CH), :], sem).start(priority=k % 2)
    for k in range(TOP_K):
        pltpu.make_async_copy(h_ref, xs_hbm.at[pl.ds(0, MOE_TT * ROW_CH), :], sem).wait()


def _dispatch(h2p, dest_tiles, fill, n_pad):
    grid_spec = pltpu.PrefetchScalarGridSpec(
        num_scalar_prefetch=1,
        grid=(h2p.shape[0] // (MOE_TT * ROW_CH),),
        in_specs=[pl.BlockSpec((1, 1, TOP_K * MOE_TT), lambda i, f: (i, 0, 0), memory_space=pltpu.SMEM),
                  pl.BlockSpec((MOE_TT * ROW_CH, LANES), lambda i, f: (i, 0))],
        out_specs=pl.BlockSpec(memory_space=pl.ANY),
        scratch_shapes=[pltpu.VMEM((MOE_BLOCK * ROW_CH, LANES), U32),
                        pltpu.SemaphoreType.DMA,
                        pltpu.SemaphoreType.DMA])
    return pl.pallas_call(
        _dispatch_kernel,
        grid_spec=grid_spec,
        out_shape=jax.ShapeDtypeStruct((n_pad * ROW_CH, LANES), U32),
        compiler_params=_cparams(("arbitrary",)),
        name="moe_dispatch",
    )(fill, dest_tiles, h2p)


def _expert_kernel(be_ref, nu_ref, xs_ref, wg_ref, wu_ref, wd_ref, ys_ref, wgb, wub, wdb):
    i = pl.program_id(0)

    @pl.when(i >= nu_ref[0])
    def _():
        ys_ref[...] = jnp.zeros(ys_ref.shape, ys_ref.dtype)

    @pl.when(i < nu_ref[0])
    def _():
        e = be_ref[i]
        prev = be_ref[jnp.maximum(i - 1, 0)]

        @pl.when(jnp.logical_or(i == 0, e != prev))
        def _():
            wgb[...] = wg_ref[0, 0].astype(BF16)
            wub[...] = wu_ref[0, 0].astype(BF16)
            wdb[...] = wd_ref[0, 0].astype(BF16)

        xb = _load_rowchunks(xs_ref, (), MOE_BLOCK, BF16)
        g = jnp.dot(xb, wgb[...], preferred_element_type=F32)
        u = jnp.dot(xb, wub[...], preferred_element_type=F32)
        hid = (g * jax.nn.sigmoid(g)) * u
        y = jnp.dot(hid.astype(BF16), wdb[...], preferred_element_type=F32)
        _store_rowchunks(ys_ref, (), _pack_bf16_pairs(y))


def _experts(xs, blk_expert, n_used, wg, wu, wd, layer):
    rows = xs.shape[0]
    _, _, d, f = wg.shape
    nblk = rows // (MOE_BLOCK * ROW_CH)

    def row_map(i, be, nu):
        return (jnp.minimum(i, nu[0] - 1), 0)

    grid_spec = pltpu.PrefetchScalarGridSpec(
        num_scalar_prefetch=2,
        grid=(nblk,),
        in_specs=[pl.BlockSpec((MOE_BLOCK * ROW_CH, LANES), row_map),
                  pl.BlockSpec((1, 1, d, f), lambda i, be, nu: (layer, be[i], 0, 0)),
                  pl.BlockSpec((1, 1, d, f), lambda i, be, nu: (layer, be[i], 0, 0)),
                  pl.BlockSpec((1, 1, f, d), lambda i, be, nu: (layer, be[i], 0, 0))],
        out_specs=pl.BlockSpec((MOE_BLOCK * ROW_CH, LANES), lambda i, be, nu: (i, 0)),
        scratch_shapes=[pltpu.VMEM((d, f), BF16), pltpu.VMEM((d, f), BF16), pltpu.VMEM((f, d), BF16)])
    return pl.pallas_call(
        _expert_kernel,
        grid_spec=grid_spec,
        out_shape=jax.ShapeDtypeStruct((rows, LANES), U32),
        compiler_params=_cparams(("arbitrary",)),
        name="moe_experts",
    )(blk_expert, n_used, xs, wg, wu, wd)


def _combine_kernel(dest_ref, ys_hbm, gate_ref, hp_ref, x_ref, gf_ref, sg_ref, su_ref, sd_ref, fn_ref,
                    o_ref, buf, ysum, sem, *, final):
    for tl0 in range(0, MOE_TT, DMA_BATCH):
        ds = [[dest_ref[0, 0, k * MOE_TT + tl0 + u] for k in range(TOP_K)] for u in range(DMA_BATCH)]
        for u in range(DMA_BATCH):
            for k in range(TOP_K):
                d = pl.multiple_of(ds[u][k] * ROW_CH, ROW_CH)
                pltpu.make_async_copy(ys_hbm.at[pl.ds(d, ROW_CH), :],
                                      buf.at[k, pl.ds(ROW_CH * (tl0 + u), ROW_CH), :], sem).start(priority=k % 2)

    xb = _load_rowchunks(hp_ref, (), MOE_TT, BF16)
    g = jnp.dot(xb, sg_ref[...], preferred_element_type=F32)
    u = jnp.dot(xb, su_ref[...], preferred_element_type=F32)
    y = jnp.dot(((g * jax.nn.sigmoid(g)) * u).astype(BF16), sd_ref[...], preferred_element_type=F32)

    for k in range(TOP_K):
        pltpu.make_async_copy(ys_hbm.at[pl.ds(0, MOE_TT * ROW_CH), :], buf.at[k], sem).wait()
    gates = gate_ref[...]
    lo_sum = None
    hi_sum = None
    for k in range(TOP_K):
        lo, hi = _unpack_bf16_pairs(buf[k])
        gk = gates[:, k:k + 1]
        lo_sum = gk * lo if lo_sum is None else lo_sum + gk * lo
        hi_sum = gk * hi if hi_sum is None else hi_sum + gk * hi
    ysum[0] = lo_sum
    ysum[1] = hi_sum
    routed = jnp.concatenate([ysum[half, pl.ds(c, MOE_TT, stride=ROW_CH), :]
                              for half in range(2) for c in range(ROW_CH)], axis=1)
    x = x_ref[...] + gf_ref[0] * (y + routed)
    if final:
        ms = jnp.mean(x * x, axis=-1, keepdims=True)
        x = x * lax.rsqrt(ms + EPS) * fn_ref[...]
    o_ref[...] = x


def _combine(ys, dest_tiles, gates_rep, h2p, x_new, mods, sg, su, sd, fn, *, n_batch, n_rows, n_lat, final):
    t, d = x_new.shape
    f = sg.shape[1]
    tiles_b = n_rows // MOE_TT
    lat_tiles = n_lat // MOE_TT

    def mod_map(i):
        return (jnp.where(i % tiles_b >= lat_tiles, n_batch, i // tiles_b) * 6 + 5, 0, 0)

    grid_spec = pltpu.PrefetchScalarGridSpec(
        num_scalar_prefetch=0,
        grid=(t // MOE_TT,),
        in_specs=[pl.BlockSpec((1, 1, TOP_K * MOE_TT), lambda i: (i, 0, 0), memory_space=pltpu.SMEM),
                  pl.BlockSpec(memory_space=pl.ANY),
                  pl.BlockSpec((MOE_TT * ROW_CH, TOP_K), lambda i: (i, 0)),
                  pl.BlockSpec((MOE_TT * ROW_CH, LANES), lambda i: (i, 0)),
                  pl.BlockSpec((MOE_TT, d), lambda i: (i, 0)),
                  pl.BlockSpec((1, 1, d), mod_map),
                  pl.BlockSpec((d, f), lambda i: (0, 0)),
                  pl.BlockSpec((d, f), lambda i: (0, 0)),
                  pl.BlockSpec((f, d), lambda i: (0, 0)),
                  pl.BlockSpec((1, d), lambda i: (0, 0))],
        out_specs=pl.BlockSpec((MOE_TT, d), lambda i: (i, 0)),
        scratch_shapes=[pltpu.VMEM((TOP_K, MOE_TT * ROW_CH, LANES), U32),
                        pltpu.VMEM((2, MOE_TT * ROW_CH, LANES), F32),
                        pltpu.SemaphoreType.DMA])
    return pl.pallas_call(
        functools.partial(_combine_kernel, final=final),
        grid_spec=grid_spec,
        out_shape=jax.ShapeDtypeStruct((t, d), F32),
        compiler_params=_cparams(("arbitrary",)),
        name="moe_combine",
    )(dest_tiles, ys, gates_rep, h2p, x_new, mods, sg, su, sd, fn)


def _rope_tables(n_lat, n_ctx):
    t = np.arange(n_lat)
    row = (t // GRID_W).astype(np.float32)
    col = (t % GRID_W).astype(np.float32)
    npairs = HEAD_DIM // 4
    inv_freq = jnp.asarray(ROPE_THETA, F32) ** (-jnp.arange(npairs, dtype=F32) / npairs)
    ang = jnp.concatenate([jnp.asarray(row)[:, None] * inv_freq, jnp.asarray(col)[:, None] * inv_freq], axis=-1)
    cos = jnp.repeat(jnp.cos(ang), 2, axis=-1)
    sin = jnp.repeat(jnp.sin(ang), 2, axis=-1)
    sign = jnp.asarray(np.tile(np.array([-1.0, 1.0], np.float32), HEAD_DIM // 2))
    cosf = jnp.tile(cos, (1, LANES // HEAD_DIM))
    sins = jnp.tile(sin * sign, (1, LANES // HEAD_DIM))
    cosf = jnp.concatenate([cosf, jnp.ones((n_ctx, LANES), F32)], axis=0)
    sins = jnp.concatenate([sins, jnp.zeros((n_ctx, LANES), F32)], axis=0)
    return cosf, sins


def _moe_block(logits_t, h2p, x_new, mods_l, rb, wg, wu, wd, sg, su, sd, fn, *, layer, n_batch, n_rows, n_lat,
               final):
    t = x_new.shape[0]
    idx, gates, pos, cnt = _router(logits_t, rb.reshape(N_EXPERTS, 1))
    counts = cnt[:, 0].astype(jnp.int32)
    padded = (counts + MOE_BLOCK - 1) // MOE_BLOCK * MOE_BLOCK
    pad_end = jnp.cumsum(padded)
    pad_start = pad_end - padded
    n_blocks = (t * TOP_K + N_EXPERTS * (MOE_BLOCK - 1) + MOE_BLOCK - 1) // MOE_BLOCK
    n_pad = n_blocks * MOE_BLOCK
    experts = jnp.arange(N_EXPERTS, dtype=jnp.int32)
    dest = jnp.sum(jnp.where(idx[:, :, None] == experts, pad_start, 0), axis=-1) + pos
    dest_tiles = dest.reshape(TOP_K, t // MOE_TT, MOE_TT).transpose(1, 0, 2).reshape(t // MOE_TT, 1, TOP_K * MOE_TT)
    blk_start = jnp.arange(n_blocks, dtype=jnp.int32) * MOE_BLOCK
    blk_expert = jnp.minimum(jnp.sum((pad_end[None, :] <= blk_start[:, None]).astype(jnp.int32), axis=1),
                             N_EXPERTS - 1)
    n_used = (pad_end[-1:] // MOE_BLOCK).astype(jnp.int32)
    tail = n_used + jnp.arange(N_EXPERTS + 1, dtype=jnp.int32)
    fill = jnp.stack([jnp.concatenate([jnp.maximum(pad_end - MOE_BLOCK, 0), jnp.minimum(tail, n_blocks - 1) * MOE_BLOCK]),
                      jnp.concatenate([counts > 0, tail < n_blocks]).astype(jnp.int32)]).astype(jnp.int32)
    xs = _dispatch(h2p, dest_tiles, fill, n_pad)
    ys = _experts(xs, blk_expert, n_used, wg, wu, wd, layer)
    gates_rep = jnp.repeat(gates.T, ROW_CH, axis=0)
    return _combine(ys, dest_tiles, gates_rep, h2p, x_new, mods_l, sg.astype(BF16), su.astype(BF16),
                    sd.astype(BF16), fn, n_batch=n_batch, n_rows=n_rows, n_lat=n_lat, final=final)


def kernel(x, c, ctx, c_ctx, ada_w, ada_b, norm_mix, norm_ffn, ab_w_in, ab_w_out, na_rpb, gqa_q_gain,
           gqa_k_gain, diff_w_in, diff_w_out, diff_lq1, diff_lk1, diff_lq2, diff_lk2, diff_sub_gain,
           router_w, router_bias, expert_w_gate, expert_w_up, expert_w_down, shared_w_gate, shared_w_up,
           shared_w_down, final_norm):
    b, n, d = x.shape
    n_ctx = ctx.shape[1]
    depth = ada_w.shape[0]
    assert d == 2 * ROW_CH * LANES
    assert depth == 2 and n_ctx == TM and n % (NA_ROWS * GRID_W) == 0 and d % LANES == 0
    assert (n + n_ctx) % FLASH_TK == 0 and n % FLASH_TQ == 0 and n % MOE_TT == 0 and n_ctx % MOE_TT == 0

    cvec = jnp.concatenate([c, c_ctx[None], jnp.zeros((8 - b - 1, d), F32)], axis=0)
    mods = _mods(cvec, ada_w, ada_b)[:, :b + 1].reshape(depth, (b + 1) * 6, 1, d)
    cosf, sins = _rope_tables(n, n_ctx)
    gm = jnp.asarray(np.kron(np.eye(LANES // HEAD_DIM), np.full((HEAD_DIM, HEAD_DIM), 1.0 / HEAD_DIM)), BF16)
    ones = jnp.ones((1, LANES), F32)
    fn = final_norm.reshape(1, d)

    xa = jnp.concatenate([x, ctx], axis=1)

    w = ab_w_in[0]
    kb = [w[:, 2048 + HEAD_DIM * g: 2048 + HEAD_DIM * (g + 1)] for g in range(2)]
    vb = [w[:, 2176 + HEAD_DIM * g: 2176 + HEAD_DIM * (g + 1)] for g in range(2)]
    w0 = jnp.concatenate([w[:, :2048], kb[0], kb[0], kb[1], kb[1], vb[0], vb[0], vb[1], vb[1]],
                         axis=1).astype(BF16)
    plan0 = ([(None, False, True)] * 4 + [(None, False, False)] * 8 + [("q", True, True)] * 4
             + [("k", True, False)] * 2 + [(None, False, False)] * 2)
    qg = jnp.tile(gqa_q_gain[0].reshape(1, HEAD_DIM), (1, LANES // HEAD_DIM))
    kg = jnp.tile(gqa_k_gain[0].reshape(1, HEAD_DIM), (1, LANES // HEAD_DIM))
    qkv = _proj(xa, norm_mix[0].reshape(1, d), mods[0], 0, 1, w0, cosf, sins, gm, qg, kg, plan0, n)

    tl, tr = _na_bias_tiles(na_rpb[0])
    o_na = _na(qkv, tl, tr, n_lat=n, n_ctx=n_ctx, npairs=4, qc0=0, kc0=4, vc0=8)
    tbl_g = jnp.asarray([[12, 13, 14, 15], [16, 16, 17, 17], [18, 18, 19, 19]], jnp.int32)
    o_gqa = _flash(qkv, tbl_g, n_q=n, q_row0=0, n_keys=n + n_ctx, key_row0=0, tq=FLASH_TQ, tk=FLASH_TK)
    tbl_c = jnp.asarray([[0, 1, 2, 3, 12, 13, 14, 15], [4, 5, 6, 7, 16, 16, 17, 17],
                         [8, 9, 10, 11, 18, 18, 19, 19]], jnp.int32)
    o_ctx = _flash(qkv, tbl_c, n_q=n_ctx, q_row0=n, n_keys=n_ctx, key_row0=n, tq=n_ctx, tk=n_ctx)

    s_all = n + n_ctx
    x_new, h2p, lg = _outproj([o_na, o_gqa], o_ctx, ab_w_out[0].astype(BF16), xa, mods[0],
                              norm_ffn[0].reshape(1, d), router_w[0].T, n_lat=n, n_rows=s_all)
    xa = _moe_block(lg, h2p.reshape(-1, LANES), x_new.reshape(b * s_all, d), mods[0], router_bias[0],
                    expert_w_gate, expert_w_up, expert_w_down, shared_w_gate[0], shared_w_up[0],
                    shared_w_down[0], fn, layer=0, n_batch=b, n_rows=s_all, n_lat=n, final=False).reshape(b, s_all, d)

    lam_init = 0.8 - 0.6 * math.exp(-0.3 * 1)
    plan1 = [(None, True, True)] * 8 + [(None, True, False)] * 8 + [(None, False, False)] * 8
    qkv = _proj(xa, norm_mix[1].reshape(1, d), mods[1], 0, 1, diff_w_in[0].astype(BF16), cosf, sins, gm,
                ones, ones, plan1, n)
    tbl_d = jnp.asarray([list(range(0, 8)), list(range(8, 16)), list(range(16, 24))], jnp.int32)
    dp = [diff_lq1[0].reshape(1, HEAD_DIM), diff_lk1[0].reshape(1, HEAD_DIM),
          diff_lq2[0].reshape(1, HEAD_DIM), diff_lk2[0].reshape(1, HEAD_DIM),
          diff_sub_gain[0].reshape(1, LANES)]
    o_diff = _flash(qkv, tbl_d, n_q=n, q_row0=0, n_keys=n + n_ctx, key_row0=0, tq=FLASH_TQ, tk=FLASH_TK,
                    mode="diff", diff_params=dp, lam_init=lam_init)
    x_new, h2p, lg = _outproj([o_diff], None, diff_w_out[0].astype(BF16), xa, mods[1],
                              norm_ffn[1].reshape(1, d), router_w[1].T, n_lat=n, n_rows=n)
    return _moe_block(lg, h2p.reshape(-1, LANES), x_new.reshape(b * n, d), mods[1], router_bias[1],
                      expert_w_gate, expert_w_up, expert_w_down, shared_w_gate[1], shared_w_up[1],
                      shared_w_down[1], fn, layer=1, n_batch=b, n_rows=n, n_lat=n, final=True).reshape(b, n, d)
```

```python
import functools
import math

import jax
import jax.numpy as jnp
import numpy as np
from jax import lax
from jax.experimental import pallas as pl
from jax.experimental.pallas import tpu as pltpu

F32 = jnp.float32
BF16 = jnp.bfloat16
U32 = jnp.uint32

LANES = 128
HEAD_DIM = 64
GRID_W = 64
WIN_ROWS = 8
WIN_COLS = 16
ROPE_THETA = 10000.0
EPS = 1e-6
N_EXPERTS = 64
TOP_K = 8
ROUTED_SCALE = 2.5
NEG = -1e30
LOG2E = math.log2(math.e)
Q_SCALE = HEAD_DIM ** -0.5 * LOG2E
VMEM_LIMIT = 56 * 1024 * 1024

TM = 256
NA_ROWS = 8
NA_SPAN = 16
FLASH_TQ = 512
FLASH_TK = 768
MOE_BLOCK = 512
MOE_TT = 128
DMA_BATCH = 2
HI_MASK = 0xFFFF0000


def _cparams(sem):
    return pltpu.CompilerParams(dimension_semantics=sem, vmem_limit_bytes=VMEM_LIMIT)


def _pack_bf16_pairs(x):
    w = x.shape[1] // 2
    bits = lax.bitcast_convert_type(x.astype(BF16).astype(F32), U32)
    return (bits[:, w:] & jnp.uint32(HI_MASK)) | (bits[:, :w] >> 16)


def _unpack_bf16_pairs(p):
    lo = lax.bitcast_convert_type(p << 16, F32)
    hi = lax.bitcast_convert_type(p & jnp.uint32(HI_MASK), F32)
    return lo, hi


ROW_CH = 4


def _store_rowchunks(ref, idx, packed):
    m = packed.shape[0]
    for c in range(ROW_CH):
        ref[idx + (pl.ds(c, m, stride=ROW_CH), slice(None))] = packed[:, c * LANES:(c + 1) * LANES]


def _load_rowchunks(ref, idx, m, dtype):
    planes = [_unpack_bf16_pairs(ref[idx + (pl.ds(c, m, stride=ROW_CH), slice(None))]) for c in range(ROW_CH)]
    return jnp.concatenate([lo.astype(dtype) for lo, _ in planes] + [hi.astype(dtype) for _, hi in planes], axis=1)


def _mods_kernel(c_ref, w_ref, b_ref, o_ref):
    c = c_ref[...]
    s = c * jax.nn.sigmoid(c)
    o_ref[0] = jnp.dot(s.astype(BF16), w_ref[0].astype(BF16), preferred_element_type=F32) + b_ref[0]


def _mods(cvec, ada_w, ada_b):
    depth, d, d6 = ada_w.shape
    tn = 1536
    return pl.pallas_call(
        _mods_kernel,
        grid=(depth, d6 // tn),
        in_specs=[pl.BlockSpec((8, d), lambda l, j: (0, 0)),
                  pl.BlockSpec((1, d, tn), lambda l, j: (l, 0, j)),
                  pl.BlockSpec((1, 1, tn), lambda l, j: (l, 0, j))],
        out_specs=pl.BlockSpec((1, 8, tn), lambda l, j: (l, 0, j)),
        out_shape=jax.ShapeDtypeStruct((depth, 8, d6), F32),
        compiler_params=_cparams(("arbitrary", "arbitrary")),
        name="adaln_mods",
    )(cvec, ada_w, ada_b.reshape(depth, 1, d6))


def _proj_kernel(x_ref, g_ref, sh_ref, sc_ref, w_ref, cos_ref, sin_ref, gm_ref, qg_ref, kg_ref,
                 o_ref, *, plan):
    x = x_ref[0]
    ms = jnp.mean(x * x, axis=-1, keepdims=True)
    h = x * lax.rsqrt(ms + EPS) * g_ref[...]
    h = h * (1.0 + sc_ref[0]) + sh_ref[0]
    y = jnp.dot(h.astype(BF16), w_ref[...], preferred_element_type=F32)
    cosf = cos_ref[...]
    sins = sin_ref[...]
    even = (lax.broadcasted_iota(jnp.int32, (1, LANES), 1) % 2) == 0
    for c, (norm, rope, scale) in enumerate(plan):
        yc = y[:, c * LANES:(c + 1) * LANES]
        if norm:
            ms2 = jnp.dot((yc * yc).astype(BF16), gm_ref[...], preferred_element_type=F32)
            gain = qg_ref[...] if norm == "q" else kg_ref[...]
            yc = yc * lax.rsqrt(ms2 + EPS) * gain
        if rope:
            sw = jnp.where(even, pltpu.roll(yc, LANES - 1, 1), pltpu.roll(yc, 1, 1))
            yc = yc * cosf + sw * sins
        if scale:
            yc = yc * Q_SCALE
        o_ref[0, :, c * LANES:(c + 1) * LANES] = yc.astype(BF16)


def _proj(xa, gain, mods, sh_idx, sc_idx, w, cosf, sins, gm, qg, kg, plan, n_lat):
    b, s, d = xa.shape
    wcols = w.shape[1]
    nt = s // TM
    lat_tiles = n_lat // TM

    def mod_map(chunk):
        return lambda bi, i: (jnp.where(i >= lat_tiles, b, bi) * 6 + chunk, 0, 0)

    return pl.pallas_call(
        functools.partial(_proj_kernel, plan=plan),
        grid=(b, nt),
        in_specs=[pl.BlockSpec((1, TM, d), lambda bi, i: (bi, i, 0)),
                  pl.BlockSpec((1, d), lambda bi, i: (0, 0)),
                  pl.BlockSpec((1, 1, d), mod_map(sh_idx)),
                  pl.BlockSpec((1, 1, d), mod_map(sc_idx)),
                  pl.BlockSpec((d, wcols), lambda bi, i: (0, 0)),
                  pl.BlockSpec((TM, LANES), lambda bi, i: (i, 0)),
                  pl.BlockSpec((TM, LANES), lambda bi, i: (i, 0)),
                  pl.BlockSpec((LANES, LANES), lambda bi, i: (0, 0)),
                  pl.BlockSpec((1, LANES), lambda bi, i: (0, 0)),
                  pl.BlockSpec((1, LANES), lambda bi, i: (0, 0))],
        out_specs=pl.BlockSpec((1, TM, wcols), lambda bi, i: (bi, i, 0)),
        out_shape=jax.ShapeDtypeStruct((b, s, wcols), BF16),
        compiler_params=_cparams(("parallel", "arbitrary")),
        name="norm_inproj",
    )(xa, gain, mods, mods, w, cosf, sins, gm, qg, kg)


def _flash_kernel(tbl_ref, *refs, tq, tk, n_keys, n_tiles, mode, lam_init):
    del tbl_ref
    refs = list(refs)
    q_ref, k_ref, v_ref = refs[:3]
    refs = refs[3:]
    if mode == "diff":
        lq1, lk1, lq2, lk2, sg_ref = refs[:5]
        refs = refs[5:]
    o_ref, qs_ref, s_ref, m_ref, acc_ref = refs

    lane = lax.broadcasted_iota(jnp.int32, (1, LANES), 1)
    lo = lane < HEAD_DIM
    for t in range(n_tiles):
        q = q_ref[0, t * tq:(t + 1) * tq, :]
        zero = jnp.zeros_like(q)
        qs_ref[t, 0:tq, :] = jnp.where(lo, q, zero)
        qs_ref[t, tq:2 * tq, :] = jnp.where(lo, zero, q)
    nb = tk // LANES
    nchunks = n_keys // tk
    ones = jnp.ones((tk, LANES), BF16)

    def qk(slot, t, j):
        off = pl.multiple_of(j * tk, tk)
        s_ref[slot] = lax.dot_general(qs_ref[t], k_ref[0, pl.ds(off, tk), :], (((1,), (1,)), ((), ())),
                                      preferred_element_type=F32)

    def softmax_pv(slot, j):
        off = pl.multiple_of(j * tk, tk)
        s = s_ref[slot]
        m_prev = m_ref[...]
        m_next = jnp.maximum(m_prev, jnp.max(s, axis=1, keepdims=True))
        alpha = jnp.exp2(m_prev - m_next)
        p = jnp.exp2(s - jnp.concatenate([m_next] * nb, axis=1))
        v1 = jnp.concatenate([v_ref[0, pl.ds(off, tk), :], ones], axis=1)
        acc_ref[...] = (jnp.concatenate([alpha, alpha], axis=1) * acc_ref[...]
                        + jnp.dot(p.astype(BF16), v1, preferred_element_type=F32))
        m_ref[...] = m_next

    def finalize(t):
        acc = acc_ref[...]
        o = acc[:, 0:LANES] / acc[:, LANES:2 * LANES]
        if mode == "pair":
            out = jnp.where(lo, o[0:tq], o[tq:2 * tq])
        else:
            lam = (jnp.exp(jnp.sum(lq1[...] * lk1[...], axis=1, keepdims=True))
                   - jnp.exp(jnp.sum(lq2[...] * lk2[...], axis=1, keepdims=True)) + lam_init)
            dlt = o[0:tq] - lam * o[tq:2 * tq]
            ms = jnp.mean(dlt * dlt, axis=-1, keepdims=True)
            out = dlt * lax.rsqrt(ms + EPS) * sg_ref[...] * (1.0 - lam_init)
        o_ref[0, pl.ds(pl.multiple_of(t * tq, tq), tq), :] = out.astype(o_ref.dtype)

    def tile(t, par, t_next):
        m_ref[...] = jnp.full(m_ref.shape, NEG, F32)
        acc_ref[...] = jnp.zeros(acc_ref.shape, F32)

        def body(jj, carry):
            j = 2 * jj
            qk(1 - par, t, j + 1)
            softmax_pv(par, j)
            qk(par, t, j + 2)
            softmax_pv(1 - par, j + 1)
            return carry

        lax.fori_loop(0, (nchunks - 1) // 2, body, 0)
        if nchunks % 2 == 0:
            qk(1 - par, t, nchunks - 1)
            softmax_pv(par, nchunks - 2)
            if t_next is not None:
                qk(par, t_next, 0)
            softmax_pv(1 - par, nchunks - 1)
            nxt = par
        else:
            if t_next is not None:
                qk(1 - par, t_next, 0)
            softmax_pv(par, nchunks - 1)
            nxt = 1 - par
        finalize(t)
        return nxt

    qk(0, 0, 0)
    if n_tiles == 1:
        tile(0, 0, None)
    else:
        def outer(ii, carry):
            ta = 2 * ii
            par = tile(ta, 0, ta + 1)
            par = tile(ta + 1, par, jnp.minimum(ta + 2, n_tiles - 1))
            assert par == 0
            return carry

        lax.fori_loop(0, n_tiles // 2, outer, 0)


def _flash(qkv, tbl, *, n_q, q_row0, n_keys, key_row0, tq, tk, mode="pair", diff_params=None, lam_init=0.0):
    b = qkv.shape[0]
    ncols = tbl.shape[1]
    n_tiles = n_q // tq
    assert n_tiles == 1 or n_tiles % 2 == 0
    qb0 = q_row0 // n_q
    kb0 = key_row0 // n_keys
    in_specs = [pl.BlockSpec((1, n_q, LANES), lambda bi, c, t: (bi, qb0, t[0, c])),
                pl.BlockSpec((1, n_keys, LANES), lambda bi, c, t: (bi, kb0, t[1, c])),
                pl.BlockSpec((1, n_keys, LANES), lambda bi, c, t: (bi, kb0, t[2, c]))]
    args = [qkv, qkv, qkv]
    if mode == "diff":
        in_specs += [pl.BlockSpec((1, HEAD_DIM), lambda bi, c, t: (0, 0))] * 4
        in_specs += [pl.BlockSpec((1, LANES), lambda bi, c, t: (0, 0))]
        args += list(diff_params)
    grid_spec = pltpu.PrefetchScalarGridSpec(
        num_scalar_prefetch=1,
        grid=(b, ncols),
        in_specs=in_specs,
        out_specs=pl.BlockSpec((1, n_q, LANES), lambda bi, c, t: (bi, 0, c)),
        scratch_shapes=[pltpu.VMEM((n_tiles, 2 * tq, LANES), BF16),
                        pltpu.VMEM((2, 2 * tq, tk), F32),
                        pltpu.VMEM((2 * tq, LANES), F32),
                        pltpu.VMEM((2 * tq, 2 * LANES), F32)])
    return pl.pallas_call(
        functools.partial(_flash_kernel, tq=tq, tk=tk, n_keys=n_keys, n_tiles=n_tiles, mode=mode,
                          lam_init=lam_init),
        grid_spec=grid_spec,
        out_shape=jax.ShapeDtypeStruct((b, n_q, ncols * LANES), BF16),
        compiler_params=_cparams(("parallel", "parallel")),
        name="flash_%s_%d" % (mode, n_keys),
    )(tbl, *args)


def _na_case_geometry(case, rows):
    r0 = {0: 0, 1: NA_ROWS, 2: rows - NA_ROWS}[case]
    start = min(max(r0 - WIN_ROWS // 2, 0), rows - NA_SPAN)
    return r0, start


def _na_tile_index(case, dr, dk, rows):
    r0, start = _na_case_geometry(case, rows)
    r, kr = r0 + dr, start + dk
    rs = min(max(r - WIN_ROWS // 2, 0), rows - WIN_ROWS)
    if rs <= kr < rs + WIN_ROWS:
        return kr - r + WIN_ROWS
    return 0


def _na_kernel(q_ref, k_ref, v_ref, kc_ref, vc_ref, tl_ref, tr_ref, o_ref, bias_ref, *, rows):
    nblk = rows // NA_ROWS
    bq = NA_ROWS * GRID_W
    bk = NA_SPAN * GRID_W
    for hh in range(2):
        for case in range(3):
            for dr in range(NA_ROWS):
                for dkp in range(NA_SPAN // 2):
                    ia = _na_tile_index(case, dr, 2 * dkp, rows)
                    ib = _na_tile_index(case, dr, 2 * dkp + 1, rows)
                    bias_ref[hh, case, dr * GRID_W:(dr + 1) * GRID_W, dkp * LANES:(dkp + 1) * LANES] = (
                        tl_ref[hh, ia] + tr_ref[hh, ib])

    lane = lax.broadcasted_iota(jnp.int32, (1, LANES), 1)
    lo = lane < HEAD_DIM
    kctx = kc_ref[0]
    vctx = vc_ref[0]
    nt = (((1,), (1,)), ((), ()))

    def body(i, carry):
        r0 = i * NA_ROWS
        start = jnp.clip(r0 - WIN_ROWS // 2, 0, rows - NA_SPAN)
        case = jnp.where(i == 0, 0, jnp.where(i == nblk - 1, 2, 1))
        qoff = pl.multiple_of(i * bq, bq)
        koff = pl.multiple_of(start * GRID_W, GRID_W)
        qb = q_ref[0, pl.ds(qoff, bq), :]
        ks = k_ref[0, pl.ds(koff, bk), :]
        vs = v_ref[0, pl.ds(koff, bk), :]
        zero = jnp.zeros_like(qb)
        outs = []
        for hh in range(2):
            qm = jnp.where(lo, qb, zero) if hh == 0 else jnp.where(lo, zero, qb)
            s_loc = lax.dot_general(qm, ks, nt, preferred_element_type=F32) + bias_ref[hh, case]
            s_ctx = lax.dot_general(qm, kctx, nt, preferred_element_type=F32)
            m = jnp.maximum(jnp.max(s_loc, axis=1, keepdims=True), jnp.max(s_ctx, axis=1, keepdims=True))
            p_loc = jnp.exp2(s_loc - m)
            p_ctx = jnp.exp2(s_ctx - m)
            l = jnp.sum(p_loc, axis=1, keepdims=True) + jnp.sum(p_ctx, axis=1, keepdims=True)
            o = (jnp.dot(p_loc.astype(BF16), vs, preferred_element_type=F32)
                 + jnp.dot(p_ctx.astype(BF16), vctx, preferred_element_type=F32))
            outs.append(o / l)
        o_ref[0, pl.ds(qoff, bq), :] = jnp.where(lo, outs[0], outs[1]).astype(o_ref.dtype)
        return carry

    lax.fori_loop(0, nblk, body, 0)


def _na(qkv, tl, tr, *, n_lat, n_ctx, npairs, qc0, kc0, vc0):
    b = qkv.shape[0]
    rows = n_lat // GRID_W
    cb0 = n_lat // n_ctx
    nt = tl.shape[1]
    return pl.pallas_call(
        functools.partial(_na_kernel, rows=rows),
        grid=(b, npairs),
        in_specs=[pl.BlockSpec((1, n_lat, LANES), lambda bi, j: (bi, 0, qc0 + j)),
                  pl.BlockSpec((1, n_lat, LANES), lambda bi, j: (bi, 0, kc0 + j)),
                  pl.BlockSpec((1, n_lat, LANES), lambda bi, j: (bi, 0, vc0 + j)),
                  pl.BlockSpec((1, n_ctx, LANES), lambda bi, j: (bi, cb0, kc0 + j)),
                  pl.BlockSpec((1, n_ctx, LANES), lambda bi, j: (bi, cb0, vc0 + j)),
                  pl.BlockSpec((2, nt, GRID_W, LANES), lambda bi, j: (j, 0, 0, 0)),
                  pl.BlockSpec((2, nt, GRID_W, LANES), lambda bi, j: (j, 0, 0, 0))],
        out_specs=pl.BlockSpec((1, n_lat, LANES), lambda bi, j: (bi, 0, j)),
        out_shape=jax.ShapeDtypeStruct((b, n_lat, npairs * LANES), BF16),
        scratch_shapes=[pltpu.VMEM((2, 3, NA_ROWS * GRID_W, NA_SPAN * GRID_W), F32)],
        compiler_params=_cparams(("parallel", "arbitrary")),
        name="neighbourhood_attn",
    )(qkv, qkv, qkv, qkv, qkv, tl, tr)


def _na_bias_tiles(rpb):
    h = rpb.shape[0]
    cols = np.arange(GRID_W)
    cs = np.clip(cols - WIN_COLS // 2, 0, GRID_W - WIN_COLS)
    kc = cols[None, :]
    valid = (kc >= cs[:, None]) & (kc < cs[:, None] + WIN_COLS)
    ci = np.clip(kc - cols[:, None] + WIN_COLS - 1, 0, 2 * WIN_COLS - 2)
    t = jnp.where(jnp.asarray(valid)[None, None], rpb[:, :, ci].astype(F32) * LOG2E, NEG)
    t = jnp.concatenate([jnp.full((h, 1, GRID_W, GRID_W), NEG, F32), t], axis=1)
    z = jnp.zeros_like(t)
    return jnp.concatenate([t, z], axis=-1), jnp.concatenate([z, t], axis=-1)


def _outproj_kernel(*refs, n_parts, has_ctx, lat_tiles):
    refs = list(refs)
    parts = [refs.pop(0) for _ in range(n_parts)]
    octx_ref = refs.pop(0) if has_ctx else None
    w_ref, x_ref, gm_ref, ng_ref, sh_ref, sc_ref, rw_ref, xo_ref, hp_ref, lg_ref, proj_ref = refs

    def lat():
        acc = None
        off = 0
        for p in parts:
            wdt = p.shape[-1]
            t = jnp.dot(p[0], w_ref[off:off + wdt, :], preferred_element_type=F32)
            acc = t if acc is None else acc + t
            off += wdt
        proj_ref[...] = acc

    if has_ctx:
        is_ctx = pl.program_id(1) >= lat_tiles
        pl.when(jnp.logical_not(is_ctx))(lat)

        @pl.when(is_ctx)
        def _():
            proj_ref[...] = jnp.dot(octx_ref[0], w_ref[...], preferred_element_type=F32)
    else:
        lat()

    x = x_ref[0] + gm_ref[0] * proj_ref[...]
    xo_ref[0] = x
    ms = jnp.mean(x * x, axis=-1, keepdims=True)
    h = x * lax.rsqrt(ms + EPS) * ng_ref[...]
    h = h * (1.0 + sc_ref[0]) + sh_ref[0]
    _store_rowchunks(hp_ref, (0,), _pack_bf16_pairs(h))
    lg_ref[...] = lax.dot_general(rw_ref[...], h, (((1,), (1,)), ((), ())), preferred_element_type=F32,
                                  precision=lax.Precision.HIGHEST)


def _outproj(parts, octx, w, xa, mods, gain, rw_t, *, n_lat, n_rows):
    b, _, d = xa.shape
    lat_tiles = n_lat // TM
    nt = n_rows // TM
    has_ctx = octx is not None

    def mod_map(chunk):
        return lambda bi, i: (jnp.where(i >= lat_tiles, b, bi) * 6 + chunk, 0, 0)

    in_specs = [pl.BlockSpec((1, TM, p.shape[-1]), lambda bi, i: (bi, jnp.minimum(i, lat_tiles - 1), 0))
                for p in parts]
    args = list(parts)
    if has_ctx:
        in_specs.append(pl.BlockSpec((1, TM, d), lambda bi, i: (bi, 0, 0)))
        args.append(octx)
    in_specs += [pl.BlockSpec((d, d), lambda bi, i: (0, 0)),
                 pl.BlockSpec((1, TM, d), lambda bi, i: (bi, i, 0)),
                 pl.BlockSpec((1, 1, d), mod_map(2)),
                 pl.BlockSpec((1, d), lambda bi, i: (0, 0)),
                 pl.BlockSpec((1, 1, d), mod_map(3)),
                 pl.BlockSpec((1, 1, d), mod_map(4)),
                 pl.BlockSpec((N_EXPERTS, d), lambda bi, i: (0, 0))]
    args += [w, xa, mods, gain, mods, mods, rw_t]
    return pl.pallas_call(
        functools.partial(_outproj_kernel, n_parts=len(parts), has_ctx=has_ctx, lat_tiles=lat_tiles),
        grid=(b, nt),
        in_specs=in_specs,
        out_specs=[pl.BlockSpec((1, TM, d), lambda bi, i: (bi, i, 0)),
                   pl.BlockSpec((1, TM * ROW_CH, LANES), lambda bi, i: (bi, i, 0)),
                   pl.BlockSpec((N_EXPERTS, TM), lambda bi, i: (0, bi * nt + i))],
        out_shape=[jax.ShapeDtypeStruct((b, n_rows, d), F32),
                   jax.ShapeDtypeStruct((b, n_rows * ROW_CH, LANES), U32),
                   jax.ShapeDtypeStruct((N_EXPERTS, b * n_rows), F32)],
        scratch_shapes=[pltpu.VMEM((TM, d), F32)],
        compiler_params=_cparams(("parallel", "arbitrary")),
        name="outproj_ffnnorm",
    )(*args)


def _router_kernel(lg_ref, rb_ref, tri_ref, idx_ref, gate_ref, pos_ref, cnt_ref, run_ref):
    @pl.when(pl.program_id(0) == 0)
    def _():
        run_ref[...] = jnp.zeros(run_ref.shape, F32)

    scores = jax.nn.sigmoid(lg_ref[...])
    work = scores + rb_ref[...]
    eidx = lax.broadcasted_iota(jnp.int32, work.shape, 0)
    hits, idx_rows, sel_rows = [], [], []
    for _ in range(TOP_K):
        mx = jnp.max(work, axis=0, keepdims=True)
        first = jnp.min(jnp.where(work == mx, eidx, N_EXPERTS), axis=0, keepdims=True)
        hit = eidx == first
        hits.append(hit)
        idx_rows.append(first)
        sel_rows.append(jnp.sum(jnp.where(hit, scores, 0.0), axis=0, keepdims=True))
        work = jnp.where(hit, NEG, work)
    mask = jnp.zeros(work.shape, F32)
    for hit in hits:
        mask = mask + hit.astype(F32)
    denom = sel_rows[0]
    for r in sel_rows[1:]:
        denom = denom + r
    csum = jnp.dot(mask.astype(BF16), tri_ref[...], preferred_element_type=F32)
    tm = mask.shape[1]
    posall = run_ref[:, 0:1] + csum - mask
    pos_rows = [jnp.sum(jnp.where(hit, posall, 0.0), axis=0, keepdims=True) for hit in hits]
    run_ref[...] = run_ref[...] + csum[:, tm - 1:tm]
    idx_ref[...] = jnp.concatenate(idx_rows, axis=0)
    gate_ref[...] = jnp.concatenate(sel_rows, axis=0) / denom * ROUTED_SCALE
    pos_ref[...] = jnp.concatenate(pos_rows, axis=0).astype(jnp.int32)
    cnt_ref[...] = run_ref[...]


def _router(logits_t, rb):
    e, t = logits_t.shape
    tri = jnp.asarray(np.triu(np.ones((TM, TM), np.float32)), BF16)
    return pl.pallas_call(
        _router_kernel,
        grid=(t // TM,),
        in_specs=[pl.BlockSpec((e, TM), lambda i: (0, i)),
                  pl.BlockSpec((e, 1), lambda i: (0, 0)),
                  pl.BlockSpec((TM, TM), lambda i: (0, 0))],
        out_specs=[pl.BlockSpec((TOP_K, TM), lambda i: (0, i)),
                   pl.BlockSpec((TOP_K, TM), lambda i: (0, i)),
                   pl.BlockSpec((TOP_K, TM), lambda i: (0, i)),
                   pl.BlockSpec((e, LANES), lambda i: (0, 0))],
        out_shape=[jax.ShapeDtypeStruct((TOP_K, t), jnp.int32),
                   jax.ShapeDtypeStruct((TOP_K, t), F32),
                   jax.ShapeDtypeStruct((TOP_K, t), jnp.int32),
                   jax.ShapeDtypeStruct((e, LANES), F32)],
        scratch_shapes=[pltpu.VMEM((e, LANES), F32)],
        compiler_params=_cparams(("arbitrary",)),
        name="router_topk",
    )(logits_t, rb, tri)


def _dispatch_kernel(fill_ref, dest_ref, h_ref, xs_hbm, zbuf, fsem, sem):
    i = pl.program_id(0)

    def fill_copy(e):
        start = pl.multiple_of(fill_ref[0, e] * ROW_CH, MOE_BLOCK * ROW_CH)
        return pltpu.make_async_copy(zbuf, xs_hbm.at[pl.ds(start, MOE_BLOCK * ROW_CH), :], fsem)

    @pl.when(i == 0)
    def _():
        zbuf[...] = jnp.zeros(zbuf.shape, zbuf.dtype)
        for e in range(fill_ref.shape[1]):
            pl.when(fill_ref[1, e] != 0)(lambda e=e: fill_copy(e).start())
        for e in range(fill_ref.shape[1]):
            pl.when(fill_ref[1, e] != 0)(lambda e=e: fill_copy(e).wait())

    for tl0 in range(0, MOE_TT, DMA_BATCH):
        ds = [[dest_ref[0, 0, k * MOE_TT + tl0 + u] for k in range(TOP_K)] for u in range(DMA_BATCH)]
        for u in range(DMA_BATCH):
            src = h_ref.at[pl.ds(ROW_CH * (tl0 + u), ROW_CH), :]
            for k in range(TOP_K):
                d = pl.multiple_of(ds[u][k] * ROW_CH, ROW_CH)
                pltpu.make_async_copy(src, xs_hbm.at[pl.ds(d, ROW_CH), :], sem).start(priority=k % 2)
    for k in range(TOP_K):
        pltpu.make_async_copy(h_ref, xs_hbm.at[pl.ds(0, MOE_TT * ROW_CH), :], sem).wait()


def _dispatch(h2p, dest_tiles, fill, n_pad):
    grid_spec = pltpu.PrefetchScalarGridSpec(
        num_scalar_prefetch=1,
        grid=(h2p.shape[0] // (MOE_TT * ROW_CH),),
        in_specs=[pl.BlockSpec((1, 1, TOP_K * MOE_TT), lambda i, f: (i, 0, 0), memory_space=pltpu.SMEM),
                  pl.BlockSpec((MOE_TT * ROW_CH, LANES), lambda i, f: (i, 0))],
        out_specs=pl.BlockSpec(memory_space=pl.ANY),
        scratch_shapes=[pltpu.VMEM((MOE_BLOCK * ROW_CH, LANES), U32),
                        pltpu.SemaphoreType.DMA,
                        pltpu.SemaphoreType.DMA])
    return pl.pallas_call(
        _dispatch_kernel,
        grid_spec=grid_spec,
        out_shape=jax.ShapeDtypeStruct((n_pad * ROW_CH, LANES), U32),
        compiler_params=_cparams(("arbitrary",)),
        name="moe_dispatch",
    )(fill, dest_tiles, h2p)


def _expert_kernel(be_ref, nu_ref, xs_ref, wg_ref, wu_ref, wd_ref, ys_ref, wgb, wub, wdb):
    i = pl.program_id(0)

    @pl.when(i >= nu_ref[0])
    def _():
        ys_ref[...] = jnp.zeros(ys_ref.shape, ys_ref.dtype)

    @pl.when(i < nu_ref[0])
    def _():
        e = be_ref[i]
        prev = be_ref[jnp.maximum(i - 1, 0)]

        @pl.when(jnp.logical_or(i == 0, e != prev))
        def _():
            wgb[...] = wg_ref[0, 0].astype(BF16)
            wub[...] = wu_ref[0, 0].astype(BF16)
            wdb[...] = wd_ref[0, 0].astype(BF16)

        xb = _load_rowchunks(xs_ref, (), MOE_BLOCK, BF16)
        g = jnp.dot(xb, wgb[...], preferred_element_type=F32)
        u = jnp.dot(xb, wub[...], preferred_element_type=F32)
        hid = (g * jax.nn.sigmoid(g)) * u
        y = jnp.dot(hid.astype(BF16), wdb[...], preferred_element_type=F32)
        _store_rowchunks(ys_ref, (), _pack_bf16_pairs(y))


def _experts(xs, blk_expert, n_used, wg, wu, wd, layer):
    rows = xs.shape[0]
    _, _, d, f = wg.shape
    nblk = rows // (MOE_BLOCK * ROW_CH)

    def row_map(i, be, nu):
        return (jnp.minimum(i, nu[0] - 1), 0)

    grid_spec = pltpu.PrefetchScalarGridSpec(
        num_scalar_prefetch=2,
        grid=(nblk,),
        in_specs=[pl.BlockSpec((MOE_BLOCK * ROW_CH, LANES), row_map),
                  pl.BlockSpec((1, 1, d, f), lambda i, be, nu: (layer, be[i], 0, 0)),
                  pl.BlockSpec((1, 1, d, f), lambda i, be, nu: (layer, be[i], 0, 0)),
                  pl.BlockSpec((1, 1, f, d), lambda i, be, nu: (layer, be[i], 0, 0))],
        out_specs=pl.BlockSpec((MOE_BLOCK * ROW_CH, LANES), lambda i, be, nu: (i, 0)),
        scratch_shapes=[pltpu.VMEM((d, f), BF16), pltpu.VMEM((d, f), BF16), pltpu.VMEM((f, d), BF16)])
    return pl.pallas_call(
        _expert_kernel,
        grid_spec=grid_spec,
        out_shape=jax.ShapeDtypeStruct((rows, LANES), U32),
        compiler_params=_cparams(("arbitrary",)),
        name="moe_experts",
    )(blk_expert, n_used, xs, wg, wu, wd)


def _combine_kernel(dest_ref, ys_hbm, gate_ref, hp_ref, x_ref, gf_ref, sg_ref, su_ref, sd_ref, fn_ref,
                    o_ref, buf, ysum, sem, *, final):
    for tl0 in range(0, MOE_TT, DMA_BATCH):
        ds = [[dest_ref[0, 0, k * MOE_TT + tl0 + u] for k in range(TOP_K)] for u in range(DMA_BATCH)]
        for u in range(DMA_BATCH):
            for k in range(TOP_K):
                d = pl.multiple_of(ds[u][k] * ROW_CH, ROW_CH)
                pltpu.make_async_copy(ys_hbm.at[pl.ds(d, ROW_CH), :],
                                      buf.at[k, pl.ds(ROW_CH * (tl0 + u), ROW_CH), :], sem).start(priority=k % 2)

    xb = _load_rowchunks(hp_ref, (), MOE_TT, BF16)
    g = jnp.dot(xb, sg_ref[...], preferred_element_type=F32)
    u = jnp.dot(xb, su_ref[...], preferred_element_type=F32)
    y = jnp.dot(((g * jax.nn.sigmoid(g)) * u).astype(BF16), sd_ref[...], preferred_element_type=F32)

    for k in range(TOP_K):
        pltpu.make_async_copy(ys_hbm.at[pl.ds(0, MOE_TT * ROW_CH), :], buf.at[k], sem).wait()
    gates = gate_ref[...]
    lo_sum = None
    hi_sum = None
    for k in range(TOP_K):
        lo, hi = _unpack_bf16_pairs(buf[k])
        gk = gates[:, k:k + 1]
        lo_sum = gk * lo if lo_sum is None else lo_sum + gk * lo
        hi_sum = gk * hi if hi_sum is None else hi_sum + gk * hi
    ysum[0] = lo_sum
    ysum[1] = hi_sum
    routed = jnp.concatenate([ysum[half, pl.ds(c, MOE_TT, stride=ROW_CH), :]
                              for half in range(2) for c in range(ROW_CH)], axis=1)
    x = x_ref[...] + gf_ref[0] * (y + routed)
    if final:
        ms = jnp.mean(x * x, axis=-1, keepdims=True)
        x = x * lax.rsqrt(ms + EPS) * fn_ref[...]
    o_ref[...] = x


def _combine(ys, dest_tiles, gates_rep, h2p, x_new, mods, sg, su, sd, fn, *, n_batch, n_rows, n_lat, final):
    t, d = x_new.shape
    f = sg.shape[1]
    tiles_b = n_rows // MOE_TT
    lat_tiles = n_lat // MOE_TT

    def mod_map(i):
        return (jnp.where(i % tiles_b >= lat_tiles, n_batch, i // tiles_b) * 6 + 5, 0, 0)

    grid_spec = pltpu.PrefetchScalarGridSpec(
        num_scalar_prefetch=0,
        grid=(t // MOE_TT,),
        in_specs=[pl.BlockSpec((1, 1, TOP_K * MOE_TT), lambda i: (i, 0, 0), memory_space=pltpu.SMEM),
                  pl.BlockSpec(memory_space=pl.ANY),
                  pl.BlockSpec((MOE_TT * ROW_CH, TOP_K), lambda i: (i, 0)),
                  pl.BlockSpec((MOE_TT * ROW_CH, LANES), lambda i: (i, 0)),
                  pl.BlockSpec((MOE_TT, d), lambda i: (i, 0)),
                  pl.BlockSpec((1, 1, d), mod_map),
                  pl.BlockSpec((d, f), lambda i: (0, 0)),
                  pl.BlockSpec((d, f), lambda i: (0, 0)),
                  pl.BlockSpec((f, d), lambda i: (0, 0)),
                  pl.BlockSpec((1, d), lambda i: (0, 0))],
        out_specs=pl.BlockSpec((MOE_TT, d), lambda i: (i, 0)),
        scratch_shapes=[pltpu.VMEM((TOP_K, MOE_TT * ROW_CH, LANES), U32),
                        pltpu.VMEM((2, MOE_TT * ROW_CH, LANES), F32),
                        pltpu.SemaphoreType.DMA])
    return pl.pallas_call(
        functools.partial(_combine_kernel, final=final),
        grid_spec=grid_spec,
        out_shape=jax.ShapeDtypeStruct((t, d), F32),
        compiler_params=_cparams(("arbitrary",)),
        name="moe_combine",
    )(dest_tiles, ys, gates_rep, h2p, x_new, mods, sg, su, sd, fn)


def _rope_tables(n_lat, n_ctx):
    t = np.arange(n_lat)
    row = (t // GRID_W).astype(np.float32)
    col = (t % GRID_W).astype(np.float32)
    npairs = HEAD_DIM // 4
    inv_freq = jnp.asarray(ROPE_THETA, F32) ** (-jnp.arange(npairs, dtype=F32) / npairs)
    ang = jnp.concatenate([jnp.asarray(row)[:, None] * inv_freq, jnp.asarray(col)[:, None] * inv_freq], axis=-1)
    cos = jnp.repeat(jnp.cos(ang), 2, axis=-1)
    sin = jnp.repeat(jnp.sin(ang), 2, axis=-1)
    sign = jnp.asarray(np.tile(np.array([-1.0, 1.0], np.float32), HEAD_DIM // 2))
    cosf = jnp.tile(cos, (1, LANES // HEAD_DIM))
    sins = jnp.tile(sin * sign, (1, LANES // HEAD_DIM))
    cosf = jnp.concatenate([cosf, jnp.ones((n_ctx, LANES), F32)], axis=0)
    sins = jnp.concatenate([sins, jnp.zeros((n_ctx, LANES), F32)], axis=0)
    return cosf, sins


def _moe_block(logits_t, h2p, x_new, mods_l, rb, wg, wu, wd, sg, su, sd, fn, *, layer, n_batch, n_rows, n_lat,
               final):
    t = x_new.shape[0]
    idx, gates, pos, cnt = _router(logits_t, rb.reshape(N_EXPERTS, 1))
    counts = cnt[:, 0].astype(jnp.int32)
    padded = (counts + MOE_BLOCK - 1) // MOE_BLOCK * MOE_BLOCK
    pad_end = jnp.cumsum(padded)
    pad_start = pad_end - padded
    n_blocks = (t * TOP_K + N_EXPERTS * (MOE_BLOCK - 1) + MOE_BLOCK - 1) // MOE_BLOCK
    n_pad = n_blocks * MOE_BLOCK
    experts = jnp.arange(N_EXPERTS, dtype=jnp.int32)
    dest = jnp.sum(jnp.where(idx[:, :, None] == experts, pad_start, 0), axis=-1) + pos
    dest_tiles = dest.reshape(TOP_K, t // MOE_TT, MOE_TT).transpose(1, 0, 2).reshape(t // MOE_TT, 1, TOP_K * MOE_TT)
    blk_start = jnp.arange(n_blocks, dtype=jnp.int32) * MOE_BLOCK
    blk_expert = jnp.minimum(jnp.sum((pad_end[None, :] <= blk_start[:, None]).astype(jnp.int32), axis=1),
                             N_EXPERTS - 1)
    n_used = (pad_end[-1:] // MOE_BLOCK).astype(jnp.int32)
    tail = n_used + jnp.arange(N_EXPERTS + 1, dtype=jnp.int32)
    fill = jnp.stack([jnp.concatenate([jnp.maximum(pad_end - MOE_BLOCK, 0), jnp.minimum(tail, n_blocks - 1) * MOE_BLOCK]),
                      jnp.concatenate([counts > 0, tail < n_blocks]).astype(jnp.int32)]).astype(jnp.int32)
    xs = _dispatch(h2p, dest_tiles, fill, n_pad)
    ys = _experts(xs, blk_expert, n_used, wg, wu, wd, layer)
    gates_rep = jnp.repeat(gates.T, ROW_CH, axis=0)
    return _combine(ys, dest_tiles, gates_rep, h2p, x_new, mods_l, sg.astype(BF16), su.astype(BF16),
                    sd.astype(BF16), fn, n_batch=n_batch, n_rows=n_rows, n_lat=n_lat, final=final)


def kernel(x, c, ctx, c_ctx, ada_w, ada_b, norm_mix, norm_ffn, ab_w_in, ab_w_out, na_rpb, gqa_q_gain,
           gqa_k_gain, diff_w_in, diff_w_out, diff_lq1, diff_lk1, diff_lq2, diff_lk2, diff_sub_gain,
           router_w, router_bias, expert_w_gate, expert_w_up, expert_w_down, shared_w_gate, shared_w_up,
           shared_w_down, final_norm):
    b, n, d = x.shape
    n_ctx = ctx.shape[1]
    depth = ada_w.shape[0]
    assert d == 2 * ROW_CH * LANES
    assert depth == 2 and n_ctx == TM and n % (NA_ROWS * GRID_W) == 0 and d % LANES == 0
    assert (n + n_ctx) % FLASH_TK == 0 and n % FLASH_TQ == 0 and n % MOE_TT == 0 and n_ctx % MOE_TT == 0

    cvec = jnp.concatenate([c, c_ctx[None], jnp.zeros((8 - b - 1, d), F32)], axis=0)
    mods = _mods(cvec, ada_w, ada_b)[:, :b + 1].reshape(depth, (b + 1) * 6, 1, d)
    cosf, sins = _rope_tables(n, n_ctx)
    gm = jnp.asarray(np.kron(np.eye(LANES // HEAD_DIM), np.full((HEAD_DIM, HEAD_DIM), 1.0 / HEAD_DIM)), BF16)
    ones = jnp.ones((1, LANES), F32)
    fn = final_norm.reshape(1, d)

    xa = jnp.concatenate([x, ctx], axis=1)

    w = ab_w_in[0]
    kb = [w[:, 2048 + HEAD_DIM * g: 2048 + HEAD_DIM * (g + 1)] for g in range(2)]
    vb = [w[:, 2176 + HEAD_DIM * g: 2176 + HEAD_DIM * (g + 1)] for g in range(2)]
    w0 = jnp.concatenate([w[:, :2048], kb[0], kb[0], kb[1], kb[1], vb[0], vb[0], vb[1], vb[1]],
                         axis=1).astype(BF16)
    plan0 = ([(None, False, True)] * 4 + [(None, False, False)] * 8 + [("q", True, True)] * 4
             + [("k", True, False)] * 2 + [(None, False, False)] * 2)
    qg = jnp.tile(gqa_q_gain[0].reshape(1, HEAD_DIM), (1, LANES // HEAD_DIM))
    kg = jnp.tile(gqa_k_gain[0].reshape(1, HEAD_DIM), (1, LANES // HEAD_DIM))
    qkv = _proj(xa, norm_mix[0].reshape(1, d), mods[0], 0, 1, w0, cosf, sins, gm, qg, kg, plan0, n)

    tl, tr = _na_bias_tiles(na_rpb[0])
    o_na = _na(qkv, tl, tr, n_lat=n, n_ctx=n_ctx, npairs=4, qc0=0, kc0=4, vc0=8)
    tbl_g = jnp.asarray([[12, 13, 14, 15], [16, 16, 17, 17], [18, 18, 19, 19]], jnp.int32)
    o_gqa = _flash(qkv, tbl_g, n_q=n, q_row0=0, n_keys=n + n_ctx, key_row0=0, tq=FLASH_TQ, tk=FLASH_TK)
    tbl_c = jnp.asarray([[0, 1, 2, 3, 12, 13, 14, 15], [4, 5, 6, 7, 16, 16, 17, 17],
                         [8, 9, 10, 11, 18, 18, 19, 19]], jnp.int32)
    o_ctx = _flash(qkv, tbl_c, n_q=n_ctx, q_row0=n, n_keys=n_ctx, key_row0=n, tq=n_ctx, tk=n_ctx)

    s_all = n + n_ctx
    x_new, h2p, lg = _outproj([o_na, o_gqa], o_ctx, ab_w_out[0].astype(BF16), xa, mods[0],
                              norm_ffn[0].reshape(1, d), router_w[0].T, n_lat=n, n_rows=s_all)
    xa = _moe_block(lg, h2p.reshape(-1, LANES), x_new.reshape(b * s_all, d), mods[0], router_bias[0],
                    expert_w_gate, expert_w_up, expert_w_down, shared_w_gate[0], shared_w_up[0],
                    shared_w_down[0], fn, layer=0, n_batch=b, n_rows=s_all, n_lat=n, final=False).reshape(b, s_all, d)

    lam_init = 0.8 - 0.6 * math.exp(-0.3 * 1)
    plan1 = [(None, True, True)] * 8 + [(None, True, False)] * 8 + [(None, False, False)] * 8
    qkv = _proj(xa, norm_mix[1].reshape(1, d), mods[1], 0, 1, diff_w_in[0].astype(BF16), cosf, sins, gm,
                ones, ones, plan1, n)
    tbl_d = jnp.asarray([list(range(0, 8)), list(range(8, 16)), list(range(16, 24))], jnp.int32)
    dp = [diff_lq1[0].reshape(1, HEAD_DIM), diff_lk1[0].reshape(1, HEAD_DIM),
          diff_lq2[0].reshape(1, HEAD_DIM), diff_lk2[0].reshape(1, HEAD_DIM),
          diff_sub_gain[0].reshape(1, LANES)]
    o_diff = _flash(qkv, tbl_d, n_q=n, q_row0=0, n_keys=n + n_ctx, key_row0=0, tq=FLASH_TQ, tk=FLASH_TK,
                    mode="diff", diff_params=dp, lam_init=lam_init)
    x_new, h2p, lg = _outproj([o_diff], None, diff_w_out[0].astype(BF16), xa, mods[1],
                              norm_ffn[1].reshape(1, d), router_w[1].T, n_lat=n, n_rows=n)
    return _moe_block(lg, h2p.reshape(-1, LANES), x_new.reshape(b * n, d), mods[1], router_bias[1],
                      expert_w_gate, expert_w_up, expert_w_down, shared_w_gate[1], shared_w_up[1],
                      shared_w_down[1], fn, layer=1, n_batch=b, n_rows=n, n_lat=n, final=True).reshape(b, n, d)
```

```python
import functools
import math

import jax
import jax.numpy as jnp
import numpy as np
from jax import lax
from jax.experimental import pallas as pl
from jax.experimental.pallas import tpu as pltpu

F32 = jnp.float32
BF16 = jnp.bfloat16
U32 = jnp.uint32

LANES = 128
HEAD_DIM = 64
GRID_W = 64
WIN_ROWS = 8
WIN_COLS = 16
ROPE_THETA = 10000.0
EPS = 1e-6
N_EXPERTS = 64
TOP_K = 8
ROUTED_SCALE = 2.5
NEG = -1e30
LOG2E = math.log2(math.e)
Q_SCALE = HEAD_DIM ** -0.5 * LOG2E
VMEM_LIMIT = 56 * 1024 * 1024

TM = 256
NA_ROWS = 8
NA_SPAN = 16
FLASH_TQ = 256
FLASH_TK = 2816
MOE_BLOCK = 512
MOE_TT = 128
DMA_BATCH = 2
HI_MASK = 0xFFFF0000


def _cparams(sem):
    return pltpu.CompilerParams(dimension_semantics=sem, vmem_limit_bytes=VMEM_LIMIT)


def _pack_bf16_pairs(x):
    w = x.shape[1] // 2
    bits = lax.bitcast_convert_type(x.astype(BF16).astype(F32), U32)
    return (bits[:, w:] & jnp.uint32(HI_MASK)) | (bits[:, :w] >> 16)


def _unpack_bf16_pairs(p):
    lo = lax.bitcast_convert_type(p << 16, F32)
    hi = lax.bitcast_convert_type(p & jnp.uint32(HI_MASK), F32)
    return lo, hi


ROW_CH = 4


def _store_rowchunks(ref, idx, packed):
    m = packed.shape[0]
    for c in range(ROW_CH):
        ref[idx + (pl.ds(c, m, stride=ROW_CH), slice(None))] = packed[:, c * LANES:(c + 1) * LANES]


def _load_rowchunks(ref, idx, m, dtype):
    planes = [_unpack_bf16_pairs(ref[idx + (pl.ds(c, m, stride=ROW_CH), slice(None))]) for c in range(ROW_CH)]
    return jnp.concatenate([lo.astype(dtype) for lo, _ in planes] + [hi.astype(dtype) for _, hi in planes], axis=1)


def _mods_kernel(c_ref, w_ref, b_ref, o_ref):
    c = c_ref[...]
    s = c * jax.nn.sigmoid(c)
    o_ref[0] = jnp.dot(s.astype(BF16), w_ref[0].astype(BF16), preferred_element_type=F32) + b_ref[0]


def _mods(cvec, ada_w, ada_b):
    depth, d, d6 = ada_w.shape
    tn = 1536
    return pl.pallas_call(
        _mods_kernel,
        grid=(depth, d6 // tn),
        in_specs=[pl.BlockSpec((8, d), lambda l, j: (0, 0)),
                  pl.BlockSpec((1, d, tn), lambda l, j: (l, 0, j)),
                  pl.BlockSpec((1, 1, tn), lambda l, j: (l, 0, j))],
        out_specs=pl.BlockSpec((1, 8, tn), lambda l, j: (l, 0, j)),
        out_shape=jax.ShapeDtypeStruct((depth, 8, d6), F32),
        compiler_params=_cparams(("arbitrary", "arbitrary")),
        name="adaln_mods",
    )(cvec, ada_w, ada_b.reshape(depth, 1, d6))


def _proj_kernel(x_ref, g_ref, sh_ref, sc_ref, w_ref, cos_ref, sin_ref, gm_ref, qg_ref, kg_ref,
                 o_ref, *, plan):
    x = x_ref[0]
    ms = jnp.mean(x * x, axis=-1, keepdims=True)
    h = x * lax.rsqrt(ms + EPS) * g_ref[...]
    h = h * (1.0 + sc_ref[0]) + sh_ref[0]
    y = jnp.dot(h.astype(BF16), w_ref[...], preferred_element_type=F32)
    cosf = cos_ref[...]
    sins = sin_ref[...]
    even = (lax.broadcasted_iota(jnp.int32, (1, LANES), 1) % 2) == 0
    for c, (norm, rope, scale) in enumerate(plan):
        yc = y[:, c * LANES:(c + 1) * LANES]
        if norm:
            ms2 = jnp.dot((yc * yc).astype(BF16), gm_ref[...], preferred_element_type=F32)
            gain = qg_ref[...] if norm == "q" else kg_ref[...]
            yc = yc * lax.rsqrt(ms2 + EPS) * gain
        if rope:
            sw = jnp.where(even, pltpu.roll(yc, LANES - 1, 1), pltpu.roll(yc, 1, 1))
            yc = yc * cosf + sw * sins
        if scale:
            yc = yc * Q_SCALE
        o_ref[0, :, c * LANES:(c + 1) * LANES] = yc.astype(BF16)


def _proj(xa, gain, mods, sh_idx, sc_idx, w, cosf, sins, gm, qg, kg, plan, n_lat):
    b, s, d = xa.shape
    wcols = w.shape[1]
    nt = s // TM
    lat_tiles = n_lat // TM

    def mod_map(chunk):
        return lambda bi, i: (jnp.where(i >= lat_tiles, b, bi) * 6 + chunk, 0, 0)

    return pl.pallas_call(
        functools.partial(_proj_kernel, plan=plan),
        grid=(b, nt),
        in_specs=[pl.BlockSpec((1, TM, d), lambda bi, i: (bi, i, 0)),
                  pl.BlockSpec((1, d), lambda bi, i: (0, 0)),
                  pl.BlockSpec((1, 1, d), mod_map(sh_idx)),
                  pl.BlockSpec((1, 1, d), mod_map(sc_idx)),
                  pl.BlockSpec((d, wcols), lambda bi, i: (0, 0)),
                  pl.BlockSpec((TM, LANES), lambda bi, i: (i, 0)),
                  pl.BlockSpec((TM, LANES), lambda bi, i: (i, 0)),
                  pl.BlockSpec((LANES, LANES), lambda bi, i: (0, 0)),
                  pl.BlockSpec((1, LANES), lambda bi, i: (0, 0)),
                  pl.BlockSpec((1, LANES), lambda bi, i: (0, 0))],
        out_specs=pl.BlockSpec((1, TM, wcols), lambda bi, i: (bi, i, 0)),
        out_shape=jax.ShapeDtypeStruct((b, s, wcols), BF16),
        compiler_params=_cparams(("parallel", "arbitrary")),
        name="norm_inproj",
    )(xa, gain, mods, mods, w, cosf, sins, gm, qg, kg)


def _flash_kernel(tbl_ref, *refs, tq, tk, n_keys, n_tiles, mode, lam_init):
    del tbl_ref
    refs = list(refs)
    q_ref, k_ref, v_ref = refs[:3]
    refs = refs[3:]
    if mode == "diff":
        lq1, lk1, lq2, lk2, sg_ref = refs[:5]
        refs = refs[5:]
    o_ref, qs_ref, s_ref, m_ref, acc_ref = refs

    lane = lax.broadcasted_iota(jnp.int32, (1, LANES), 1)
    lo = lane < HEAD_DIM
    for t in range(n_tiles):
        q = q_ref[0, t * tq:(t + 1) * tq, :]
        zero = jnp.zeros_like(q)
        qs_ref[t, 0:tq, :] = jnp.where(lo, q, zero)
        qs_ref[t, tq:2 * tq, :] = jnp.where(lo, zero, q)
    nb = tk // LANES
    nchunks = n_keys // tk
    ones = jnp.ones((tk, LANES), BF16)

    def qk(slot, t, j):
        off = pl.multiple_of(j * tk, tk)
        s_ref[slot] = lax.dot_general(qs_ref[t], k_ref[0, pl.ds(off, tk), :], (((1,), (1,)), ((), ())),
                                      preferred_element_type=F32)

    def softmax_pv(slot, j):
        off = pl.multiple_of(j * tk, tk)
        s = s_ref[slot]
        m_prev = m_ref[...]
        m_next = jnp.maximum(m_prev, jnp.max(s, axis=1, keepdims=True))
        alpha = jnp.exp2(m_prev - m_next)
        p = jnp.exp2(s - jnp.concatenate([m_next] * nb, axis=1))
        v1 = jnp.concatenate([v_ref[0, pl.ds(off, tk), :], ones], axis=1)
        acc_ref[...] = (jnp.concatenate([alpha, alpha], axis=1) * acc_ref[...]
                        + jnp.dot(p.astype(BF16), v1, preferred_element_type=F32))
        m_ref[...] = m_next

    def finalize(t):
        acc = acc_ref[...]
        o = acc[:, 0:LANES] / acc[:, LANES:2 * LANES]
        if mode == "pair":
            out = jnp.where(lo, o[0:tq], o[tq:2 * tq])
        else:
            lam = (jnp.exp(jnp.sum(lq1[...] * lk1[...], axis=1, keepdims=True))
                   - jnp.exp(jnp.sum(lq2[...] * lk2[...], axis=1, keepdims=True)) + lam_init)
            dlt = o[0:tq] - lam * o[tq:2 * tq]
            ms = jnp.mean(dlt * dlt, axis=-1, keepdims=True)
            out = dlt * lax.rsqrt(ms + EPS) * sg_ref[...] * (1.0 - lam_init)
        o_ref[0, pl.ds(pl.multiple_of(t * tq, tq), tq), :] = out.astype(o_ref.dtype)

    def tile(t, par, t_next):
        m_ref[...] = jnp.full(m_ref.shape, NEG, F32)
        acc_ref[...] = jnp.zeros(acc_ref.shape, F32)

        def body(jj, carry):
            j = 2 * jj
            qk(1 - par, t, j + 1)
            softmax_pv(par, j)
            qk(par, t, j + 2)
            softmax_pv(1 - par, j + 1)
            return carry

        lax.fori_loop(0, (nchunks - 1) // 2, body, 0)
        if nchunks % 2 == 0:
            qk(1 - par, t, nchunks - 1)
            softmax_pv(par, nchunks - 2)
            if t_next is not None:
                qk(par, t_next, 0)
            softmax_pv(1 - par, nchunks - 1)
            nxt = par
        else:
            if t_next is not None:
                qk(1 - par, t_next, 0)
            softmax_pv(par, nchunks - 1)
            nxt = 1 - par
        finalize(t)
        return nxt

    qk(0, 0, 0)
    if n_tiles == 1:
        tile(0, 0, None)
    else:
        def outer(ii, carry):
            ta = 2 * ii
            par = tile(ta, 0, ta + 1)
            par = tile(ta + 1, par, jnp.minimum(ta + 2, n_tiles - 1))
            assert par == 0
            return carry

        lax.fori_loop(0, n_tiles // 2, outer, 0)


def _flash(qkv, tbl, *, n_q, q_row0, n_keys, key_row0, tq, tk, mode="pair", diff_params=None, lam_init=0.0):
    b = qkv.shape[0]
    ncols = tbl.shape[1]
    n_tiles = n_q // tq
    assert n_tiles == 1 or n_tiles % 2 == 0
    qb0 = q_row0 // n_q
    kb0 = key_row0 // n_keys
    in_specs = [pl.BlockSpec((1, n_q, LANES), lambda bi, c, t: (bi, qb0, t[0, c])),
                pl.BlockSpec((1, n_keys, LANES), lambda bi, c, t: (bi, kb0, t[1, c])),
                pl.BlockSpec((1, n_keys, LANES), lambda bi, c, t: (bi, kb0, t[2, c]))]
    args = [qkv, qkv, qkv]
    if mode == "diff":
        in_specs += [pl.BlockSpec((1, HEAD_DIM), lambda bi, c, t: (0, 0))] * 4
        in_specs += [pl.BlockSpec((1, LANES), lambda bi, c, t: (0, 0))]
        args += list(diff_params)
    grid_spec = pltpu.PrefetchScalarGridSpec(
        num_scalar_prefetch=1,
        grid=(b, ncols),
        in_specs=in_specs,
        out_specs=pl.BlockSpec((1, n_q, LANES), lambda bi, c, t: (bi, 0, c)),
        scratch_shapes=[pltpu.VMEM((n_tiles, 2 * tq, LANES), BF16),
                        pltpu.VMEM((2, 2 * tq, tk), F32),
                        pltpu.VMEM((2 * tq, LANES), F32),
                        pltpu.VMEM((2 * tq, 2 * LANES), F32)])
    return pl.pallas_call(
        functools.partial(_flash_kernel, tq=tq, tk=tk, n_keys=n_keys, n_tiles=n_tiles, mode=mode,
                          lam_init=lam_init),
        grid_spec=grid_spec,
        out_shape=jax.ShapeDtypeStruct((b, n_q, ncols * LANES), BF16),
        compiler_params=_cparams(("parallel", "parallel")),
        name="flash_%s_%d" % (mode, n_keys),
    )(tbl, *args)


def _na_case_geometry(case, rows):
    r0 = {0: 0, 1: NA_ROWS, 2: rows - NA_ROWS}[case]
    start = min(max(r0 - WIN_ROWS // 2, 0), rows - NA_SPAN)
    return r0, start


def _na_tile_index(case, dr, dk, rows):
    r0, start = _na_case_geometry(case, rows)
    r, kr = r0 + dr, start + dk
    rs = min(max(r - WIN_ROWS // 2, 0), rows - WIN_ROWS)
    if rs <= kr < rs + WIN_ROWS:
        return kr - r + WIN_ROWS
    return 0


def _na_kernel(q_ref, k_ref, v_ref, kc_ref, vc_ref, tl_ref, tr_ref, o_ref, bias_ref, *, rows):
    nblk = rows // NA_ROWS
    bq = NA_ROWS * GRID_W
    bk = NA_SPAN * GRID_W
    for hh in range(2):
        for case in range(3):
            for dr in range(NA_ROWS):
                for dkp in range(NA_SPAN // 2):
                    ia = _na_tile_index(case, dr, 2 * dkp, rows)
                    ib = _na_tile_index(case, dr, 2 * dkp + 1, rows)
                    bias_ref[hh, case, dr * GRID_W:(dr + 1) * GRID_W, dkp * LANES:(dkp + 1) * LANES] = (
                        tl_ref[hh, ia] + tr_ref[hh, ib])

    lane = lax.broadcasted_iota(jnp.int32, (1, LANES), 1)
    lo = lane < HEAD_DIM
    kctx = kc_ref[0]
    vctx = vc_ref[0]
    nt = (((1,), (1,)), ((), ()))

    def body(i, carry):
        r0 = i * NA_ROWS
        start = jnp.clip(r0 - WIN_ROWS // 2, 0, rows - NA_SPAN)
        case = jnp.where(i == 0, 0, jnp.where(i == nblk - 1, 2, 1))
        qoff = pl.multiple_of(i * bq, bq)
        koff = pl.multiple_of(start * GRID_W, GRID_W)
        qb = q_ref[0, pl.ds(qoff, bq), :]
        ks = k_ref[0, pl.ds(koff, bk), :]
        vs = v_ref[0, pl.ds(koff, bk), :]
        zero = jnp.zeros_like(qb)
        outs = []
        for hh in range(2):
            qm = jnp.where(lo, qb, zero) if hh == 0 else jnp.where(lo, zero, qb)
            s_loc = lax.dot_general(qm, ks, nt, preferred_element_type=F32) + bias_ref[hh, case]
            s_ctx = lax.dot_general(qm, kctx, nt, preferred_element_type=F32)
            m = jnp.maximum(jnp.max(s_loc, axis=1, keepdims=True), jnp.max(s_ctx, axis=1, keepdims=True))
            p_loc = jnp.exp2(s_loc - m)
            p_ctx = jnp.exp2(s_ctx - m)
            l = jnp.sum(p_loc, axis=1, keepdims=True) + jnp.sum(p_ctx, axis=1, keepdims=True)
            o = (jnp.dot(p_loc.astype(BF16), vs, preferred_element_type=F32)
                 + jnp.dot(p_ctx.astype(BF16), vctx, preferred_element_type=F32))
            outs.append(o / l)
        o_ref[0, pl.ds(qoff, bq), :] = jnp.where(lo, outs[0], outs[1]).astype(o_ref.dtype)
        return carry

    lax.fori_loop(0, nblk, body, 0)


def _na(qkv, tl, tr, *, n_lat, n_ctx, npairs, qc0, kc0, vc0):
    b = qkv.shape[0]
    rows = n_lat // GRID_W
    cb0 = n_lat // n_ctx
    nt = tl.shape[1]
    return pl.pallas_call(
        functools.partial(_na_kernel, rows=rows),
        grid=(b, npairs),
        in_specs=[pl.BlockSpec((1, n_lat, LANES), lambda bi, j: (bi, 0, qc0 + j)),
                  pl.BlockSpec((1, n_lat, LANES), lambda bi, j: (bi, 0, kc0 + j)),
                  pl.BlockSpec((1, n_lat, LANES), lambda bi, j: (bi, 0, vc0 + j)),
                  pl.BlockSpec((1, n_ctx, LANES), lambda bi, j: (bi, cb0, kc0 + j)),
                  pl.BlockSpec((1, n_ctx, LANES), lambda bi, j: (bi, cb0, vc0 + j)),
                  pl.BlockSpec((2, nt, GRID_W, LANES), lambda bi, j: (j, 0, 0, 0)),
                  pl.BlockSpec((2, nt, GRID_W, LANES), lambda bi, j: (j, 0, 0, 0))],
        out_specs=pl.BlockSpec((1, n_lat, LANES), lambda bi, j: (bi, 0, j)),
        out_shape=jax.ShapeDtypeStruct((b, n_lat, npairs * LANES), BF16),
        scratch_shapes=[pltpu.VMEM((2, 3, NA_ROWS * GRID_W, NA_SPAN * GRID_W), F32)],
        compiler_params=_cparams(("parallel", "arbitrary")),
        name="neighbourhood_attn",
    )(qkv, qkv, qkv, qkv, qkv, tl, tr)


def _na_bias_tiles(rpb):
    h = rpb.shape[0]
    cols = np.arange(GRID_W)
    cs = np.clip(cols - WIN_COLS // 2, 0, GRID_W - WIN_COLS)
    kc = cols[None, :]
    valid = (kc >= cs[:, None]) & (kc < cs[:, None] + WIN_COLS)
    ci = np.clip(kc - cols[:, None] + WIN_COLS - 1, 0, 2 * WIN_COLS - 2)
    t = jnp.where(jnp.asarray(valid)[None, None], rpb[:, :, ci].astype(F32) * LOG2E, NEG)
    t = jnp.concatenate([jnp.full((h, 1, GRID_W, GRID_W), NEG, F32), t], axis=1)
    z = jnp.zeros_like(t)
    return jnp.concatenate([t, z], axis=-1), jnp.concatenate([z, t], axis=-1)


def _outproj_kernel(*refs, n_parts, has_ctx, lat_tiles):
    refs = list(refs)
    parts = [refs.pop(0) for _ in range(n_parts)]
    octx_ref = refs.pop(0) if has_ctx else None
    w_ref, x_ref, gm_ref, ng_ref, sh_ref, sc_ref, rw_ref, xo_ref, hp_ref, lg_ref, proj_ref = refs

    def lat():
        acc = None
        off = 0
        for p in parts:
            wdt = p.shape[-1]
            t = jnp.dot(p[0], w_ref[off:off + wdt, :], preferred_element_type=F32)
            acc = t if acc is None else acc + t
            off += wdt
        proj_ref[...] = acc

    if has_ctx:
        is_ctx = pl.program_id(1) >= lat_tiles
        pl.when(jnp.logical_not(is_ctx))(lat)

        @pl.when(is_ctx)
        def _():
            proj_ref[...] = jnp.dot(octx_ref[0], w_ref[...], preferred_element_type=F32)
    else:
        lat()

    x = x_ref[0] + gm_ref[0] * proj_ref[...]
    xo_ref[0] = x
    ms = jnp.mean(x * x, axis=-1, keepdims=True)
    h = x * lax.rsqrt(ms + EPS) * ng_ref[...]
    h = h * (1.0 + sc_ref[0]) + sh_ref[0]
    _store_rowchunks(hp_ref, (0,), _pack_bf16_pairs(h))
    lg_ref[...] = lax.dot_general(rw_ref[...], h, (((1,), (1,)), ((), ())), preferred_element_type=F32,
                                  precision=lax.Precision.HIGHEST)


def _outproj(parts, octx, w, xa, mods, gain, rw_t, *, n_lat, n_rows):
    b, _, d = xa.shape
    lat_tiles = n_lat // TM
    nt = n_rows // TM
    has_ctx = octx is not None

    def mod_map(chunk):
        return lambda bi, i: (jnp.where(i >= lat_tiles, b, bi) * 6 + chunk, 0, 0)

    in_specs = [pl.BlockSpec((1, TM, p.shape[-1]), lambda bi, i: (bi, jnp.minimum(i, lat_tiles - 1), 0))
                for p in parts]
    args = list(parts)
    if has_ctx:
        in_specs.append(pl.BlockSpec((1, TM, d), lambda bi, i: (bi, 0, 0)))
        args.append(octx)
    in_specs += [pl.BlockSpec((d, d), lambda bi, i: (0, 0)),
                 pl.BlockSpec((1, TM, d), lambda bi, i: (bi, i, 0)),
                 pl.BlockSpec((1, 1, d), mod_map(2)),
                 pl.BlockSpec((1, d), lambda bi, i: (0, 0)),
                 pl.BlockSpec((1, 1, d), mod_map(3)),
                 pl.BlockSpec((1, 1, d), mod_map(4)),
                 pl.BlockSpec((N_EXPERTS, d), lambda bi, i: (0, 0))]
    args += [w, xa, mods, gain, mods, mods, rw_t]
    return pl.pallas_call(
        functools.partial(_outproj_kernel, n_parts=len(parts), has_ctx=has_ctx, lat_tiles=lat_tiles),
        grid=(b, nt),
        in_specs=in_specs,
        out_specs=[pl.BlockSpec((1, TM, d), lambda bi, i: (bi, i, 0)),
                   pl.BlockSpec((1, TM * ROW_CH, LANES), lambda bi, i: (bi, i, 0)),
                   pl.BlockSpec((N_EXPERTS, TM), lambda bi, i: (0, bi * nt + i))],
        out_shape=[jax.ShapeDtypeStruct((b, n_rows, d), F32),
                   jax.ShapeDtypeStruct((b, n_rows * ROW_CH, LANES), U32),
                   jax.ShapeDtypeStruct((N_EXPERTS, b * n_rows), F32)],
        scratch_shapes=[pltpu.VMEM((TM, d), F32)],
        compiler_params=_cparams(("parallel", "arbitrary")),
        name="outproj_ffnnorm",
    )(*args)


def _router_kernel(lg_ref, rb_ref, tri_ref, idx_ref, gate_ref, pos_ref, cnt_ref, run_ref):
    @pl.when(pl.program_id(0) == 0)
    def _():
        run_ref[...] = jnp.zeros(run_ref.shape, F32)

    scores = jax.nn.sigmoid(lg_ref[...])
    work = scores + rb_ref[...]
    eidx = lax.broadcasted_iota(jnp.int32, work.shape, 0)
    hits, idx_rows, sel_rows = [], [], []
    for _ in range(TOP_K):
        mx = jnp.max(work, axis=0, keepdims=True)
        first = jnp.min(jnp.where(work == mx, eidx, N_EXPERTS), axis=0, keepdims=True)
        hit = eidx == first
        hits.append(hit)
        idx_rows.append(first)
        sel_rows.append(jnp.sum(jnp.where(hit, scores, 0.0), axis=0, keepdims=True))
        work = jnp.where(hit, NEG, work)
    mask = jnp.zeros(work.shape, F32)
    for hit in hits:
        mask = mask + hit.astype(F32)
    denom = sel_rows[0]
    for r in sel_rows[1:]:
        denom = denom + r
    csum = jnp.dot(mask.astype(BF16), tri_ref[...], preferred_element_type=F32)
    tm = mask.shape[1]
    posall = run_ref[:, 0:1] + csum - mask
    pos_rows = [jnp.sum(jnp.where(hit, posall, 0.0), axis=0, keepdims=True) for hit in hits]
    run_ref[...] = run_ref[...] + csum[:, tm - 1:tm]
    idx_ref[...] = jnp.concatenate(idx_rows, axis=0)
    gate_ref[...] = jnp.concatenate(sel_rows, axis=0) / denom * ROUTED_SCALE
    pos_ref[...] = jnp.concatenate(pos_rows, axis=0).astype(jnp.int32)
    cnt_ref[...] = run_ref[...]


def _router(logits_t, rb):
    e, t = logits_t.shape
    tri = jnp.asarray(np.triu(np.ones((TM, TM), np.float32)), BF16)
    return pl.pallas_call(
        _router_kernel,
        grid=(t // TM,),
        in_specs=[pl.BlockSpec((e, TM), lambda i: (0, i)),
                  pl.BlockSpec((e, 1), lambda i: (0, 0)),
                  pl.BlockSpec((TM, TM), lambda i: (0, 0))],
        out_specs=[pl.BlockSpec((TOP_K, TM), lambda i: (0, i)),
                   pl.BlockSpec((TOP_K, TM), lambda i: (0, i)),
                   pl.BlockSpec((TOP_K, TM), lambda i: (0, i)),
                   pl.BlockSpec((e, LANES), lambda i: (0, 0))],
        out_shape=[jax.ShapeDtypeStruct((TOP_K, t), jnp.int32),
                   jax.ShapeDtypeStruct((TOP_K, t), F32),
                   jax.ShapeDtypeStruct((TOP_K, t), jnp.int32),
                   jax.ShapeDtypeStruct((e, LANES), F32)],
        scratch_shapes=[pltpu.VMEM((e, LANES), F32)],
        compiler_params=_cparams(("arbitrary",)),
        name="router_topk",
    )(logits_t, rb, tri)


def _dispatch_kernel(fill_ref, dest_ref, h_ref, xs_hbm, zbuf, fsem, sem):
    i = pl.program_id(0)

    def fill_copy(e):
        start = pl.multiple_of(fill_ref[0, e] * ROW_CH, MOE_BLOCK * ROW_CH)
        return pltpu.make_async_copy(zbuf, xs_hbm.at[pl.ds(start, MOE_BLOCK * ROW_CH), :], fsem)

    @pl.when(i == 0)
    def _():
        zbuf[...] = jnp.zeros(zbuf.shape, zbuf.dtype)
        for e in range(fill_ref.shape[1]):
            pl.when(fill_ref[1, e] != 0)(lambda e=e: fill_copy(e).start())
        for e in range(fill_ref.shape[1]):
            pl.when(fill_ref[1, e] != 0)(lambda e=e: fill_copy(e).wait())

    for tl0 in range(0, MOE_TT, DMA_BATCH):
        ds = [[dest_ref[0, 0, k * MOE_TT + tl0 + u] for k in range(TOP_K)] for u in range(DMA_BATCH)]
        for u in range(DMA_BATCH):
            src = h_ref.at[pl.ds(ROW_CH * (tl0 + u), ROW_CH), :]
            for k in range(TOP_K):
                d = pl.multiple_of(ds[u][k] * ROW_CH, ROW_CH)
                pltpu.make_async_copy(src, xs_hbm.at[pl.ds(d, ROW_CH), :], sem).start(priority=k % 2)
    for k in range(TOP_K):
        pltpu.make_async_copy(h_ref, xs_hbm.at[pl.ds(0, MOE_TT * ROW_CH), :], sem).wait()


def _dispatch(h2p, dest_tiles, fill, n_pad):
    grid_spec = pltpu.PrefetchScalarGridSpec(
        num_scalar_prefetch=1,
        grid=(h2p.shape[0] // (MOE_TT * ROW_CH),),
        in_specs=[pl.BlockSpec((1, 1, TOP_K * MOE_TT), lambda i, f: (i, 0, 0), memory_space=pltpu.SMEM),
                  pl.BlockSpec((MOE_TT * ROW_CH, LANES), lambda i, f: (i, 0))],
        out_specs=pl.BlockSpec(memory_space=pl.ANY),
        scratch_shapes=[pltpu.VMEM((MOE_BLOCK * ROW_CH, LANES), U32),
                        pltpu.SemaphoreType.DMA,
                        pltpu.SemaphoreType.DMA])
    return pl.pallas_call(
        _dispatch_kernel,
        grid_spec=grid_spec,
        out_shape=jax.ShapeDtypeStruct((n_pad * ROW_CH, LANES), U32),
        compiler_params=_cparams(("arbitrary",)),
        name="moe_dispatch",
    )(fill, dest_tiles, h2p)


def _expert_kernel(be_ref, nu_ref, xs_ref, wg_ref, wu_ref, wd_ref, ys_ref, wgb, wub, wdb):
    i = pl.program_id(0)

    @pl.when(i >= nu_ref[0])
    def _():
        ys_ref[...] = jnp.zeros(ys_ref.shape, ys_ref.dtype)

    @pl.when(i < nu_ref[0])
    def _():
        e = be_ref[i]
        prev = be_ref[jnp.maximum(i - 1, 0)]

        @pl.when(jnp.logical_or(i == 0, e != prev))
        def _():
            wgb[...] = wg_ref[0, 0].astype(BF16)
            wub[...] = wu_ref[0, 0].astype(BF16)
            wdb[...] = wd_ref[0, 0].astype(BF16)

        xb = _load_rowchunks(xs_ref, (), MOE_BLOCK, BF16)
        g = jnp.dot(xb, wgb[...], preferred_element_type=F32)
        u = jnp.dot(xb, wub[...], preferred_element_type=F32)
        hid = (g * jax.nn.sigmoid(g)) * u
        y = jnp.dot(hid.astype(BF16), wdb[...], preferred_element_type=F32)
        _store_rowchunks(ys_ref, (), _pack_bf16_pairs(y))


def _experts(xs, blk_expert, n_used, wg, wu, wd, layer):
    rows = xs.shape[0]
    _, _, d, f = wg.shape
    nblk = rows // (MOE_BLOCK * ROW_CH)

    def row_map(i, be, nu):
        return (jnp.minimum(i, nu[0] - 1), 0)

    grid_spec = pltpu.PrefetchScalarGridSpec(
        num_scalar_prefetch=2,
        grid=(nblk,),
        in_specs=[pl.BlockSpec((MOE_BLOCK * ROW_CH, LANES), row_map),
                  pl.BlockSpec((1, 1, d, f), lambda i, be, nu: (layer, be[i], 0, 0)),
                  pl.BlockSpec((1, 1, d, f), lambda i, be, nu: (layer, be[i], 0, 0)),
                  pl.BlockSpec((1, 1, f, d), lambda i, be, nu: (layer, be[i], 0, 0))],
        out_specs=pl.BlockSpec((MOE_BLOCK * ROW_CH, LANES), lambda i, be, nu: (i, 0)),
        scratch_shapes=[pltpu.VMEM((d, f), BF16), pltpu.VMEM((d, f), BF16), pltpu.VMEM((f, d), BF16)])
    return pl.pallas_call(
        _expert_kernel,
        grid_spec=grid_spec,
        out_shape=jax.ShapeDtypeStruct((rows, LANES), U32),
        compiler_params=_cparams(("arbitrary",)),
        name="moe_experts",
    )(blk_expert, n_used, xs, wg, wu, wd)


def _combine_kernel(dest_ref, dnext_ref, ys_hbm, gate_ref, hp_ref, x_ref, gf_ref, sg_ref, su_ref, sd_ref, fn_ref,
                    o_ref, buf, ysum, sem, *, final):
    i = pl.program_id(0)
    slot = i % 2

    def issue(dref, s):
        for tl0 in range(0, MOE_TT, DMA_BATCH):
            ds = [[dref[0, 0, k * MOE_TT + tl0 + u] for k in range(TOP_K)] for u in range(DMA_BATCH)]
            for u in range(DMA_BATCH):
                for k in range(TOP_K):
                    d = pl.multiple_of(ds[u][k] * ROW_CH, ROW_CH)
                    pltpu.make_async_copy(ys_hbm.at[pl.ds(d, ROW_CH), :],
                                          buf.at[s, k, pl.ds(ROW_CH * (tl0 + u), ROW_CH), :],
                                          sem.at[s]).start(priority=k % 2)

    pl.when(i == 0)(lambda: issue(dest_ref, 0))
    pl.when(i + 1 < pl.num_programs(0))(lambda: issue(dnext_ref, 1 - slot))

    xb = _load_rowchunks(hp_ref, (), MOE_TT, BF16)
    g = jnp.dot(xb, sg_ref[...], preferred_element_type=F32)
    u = jnp.dot(xb, su_ref[...], preferred_element_type=F32)
    y = jnp.dot(((g * jax.nn.sigmoid(g)) * u).astype(BF16), sd_ref[...], preferred_element_type=F32)

    for k in range(TOP_K):
        pltpu.make_async_copy(ys_hbm.at[pl.ds(0, MOE_TT * ROW_CH), :], buf.at[slot, k], sem.at[slot]).wait()
    gates = gate_ref[...]
    lo_sum = None
    hi_sum = None
    for k in range(TOP_K):
        lo, hi = _unpack_bf16_pairs(buf[slot, k])
        gk = gates[:, k:k + 1]
        lo_sum = gk * lo if lo_sum is None else lo_sum + gk * lo
        hi_sum = gk * hi if hi_sum is None else hi_sum + gk * hi
    ysum[0] = lo_sum
    ysum[1] = hi_sum
    routed = jnp.concatenate([ysum[half, pl.ds(c, MOE_TT, stride=ROW_CH), :]
                              for half in range(2) for c in range(ROW_CH)], axis=1)
    x = x_ref[...] + gf_ref[0] * (y + routed)
    if final:
        ms = jnp.mean(x * x, axis=-1, keepdims=True)
        x = x * lax.rsqrt(ms + EPS) * fn_ref[...]
    o_ref[...] = x


def _combine(ys, dest_tiles, gates_rep, h2p, x_new, mods, sg, su, sd, fn, *, n_batch, n_rows, n_lat, final):
    t, d = x_new.shape
    f = sg.shape[1]
    tiles_b = n_rows // MOE_TT
    lat_tiles = n_lat // MOE_TT
    n_tiles = t // MOE_TT

    def mod_map(i):
        return (jnp.where(i % tiles_b >= lat_tiles, n_batch, i // tiles_b) * 6 + 5, 0, 0)

    grid_spec = pltpu.PrefetchScalarGridSpec(
        num_scalar_prefetch=0,
        grid=(t // MOE_TT,),
        in_specs=[pl.BlockSpec((1, 1, TOP_K * MOE_TT), lambda i: (i, 0, 0), memory_space=pltpu.SMEM),
                  pl.BlockSpec((1, 1, TOP_K * MOE_TT), lambda i: (jnp.minimum(i + 1, n_tiles - 1), 0, 0),
                               memory_space=pltpu.SMEM),
                  pl.BlockSpec(memory_space=pl.ANY),
                  pl.BlockSpec((MOE_TT * ROW_CH, TOP_K), lambda i: (i, 0)),
                  pl.BlockSpec((MOE_TT * ROW_CH, LANES), lambda i: (i, 0)),
                  pl.BlockSpec((MOE_TT, d), lambda i: (i, 0)),
                  pl.BlockSpec((1, 1, d), mod_map),
                  pl.BlockSpec((d, f), lambda i: (0, 0)),
                  pl.BlockSpec((d, f), lambda i: (0, 0)),
                  pl.BlockSpec((f, d), lambda i: (0, 0)),
                  pl.BlockSpec((1, d), lambda i: (0, 0))],
        out_specs=pl.BlockSpec((MOE_TT, d), lambda i: (i, 0)),
        scratch_shapes=[pltpu.VMEM((2, TOP_K, MOE_TT * ROW_CH, LANES), U32),
                        pltpu.VMEM((2, MOE_TT * ROW_CH, LANES), F32),
                        pltpu.SemaphoreType.DMA((2,))])
    return pl.pallas_call(
        functools.partial(_combine_kernel, final=final),
        grid_spec=grid_spec,
        out_shape=jax.ShapeDtypeStruct((t, d), F32),
        compiler_params=_cparams(("arbitrary",)),
        name="moe_combine",
    )(dest_tiles, dest_tiles, ys, gates_rep, h2p, x_new, mods, sg, su, sd, fn)


def _rope_tables(n_lat, n_ctx):
    t = np.arange(n_lat)
    row = (t // GRID_W).astype(np.float32)
    col = (t % GRID_W).astype(np.float32)
    npairs = HEAD_DIM // 4
    inv_freq = jnp.asarray(ROPE_THETA, F32) ** (-jnp.arange(npairs, dtype=F32) / npairs)
    ang = jnp.concatenate([jnp.asarray(row)[:, None] * inv_freq, jnp.asarray(col)[:, None] * inv_freq], axis=-1)
    cos = jnp.repeat(jnp.cos(ang), 2, axis=-1)
    sin = jnp.repeat(jnp.sin(ang), 2, axis=-1)
    sign = jnp.asarray(np.tile(np.array([-1.0, 1.0], np.float32), HEAD_DIM // 2))
    cosf = jnp.tile(cos, (1, LANES // HEAD_DIM))
    sins = jnp.tile(sin * sign, (1, LANES // HEAD_DIM))
    cosf = jnp.concatenate([cosf, jnp.ones((n_ctx, LANES), F32)], axis=0)
    sins = jnp.concatenate([sins, jnp.zeros((n_ctx, LANES), F32)], axis=0)
    return cosf, sins


def _moe_block(logits_t, h2p, x_new, mods_l, rb, wg, wu, wd, sg, su, sd, fn, *, layer, n_batch, n_rows, n_lat,
               final):
    t = x_new.shape[0]
    idx, gates, pos, cnt = _router(logits_t, rb.reshape(N_EXPERTS, 1))
    counts = cnt[:, 0].astype(jnp.int32)
    padded = (counts + MOE_BLOCK - 1) // MOE_BLOCK * MOE_BLOCK
    pad_end = jnp.cumsum(padded)
    pad_start = pad_end - padded
    n_blocks = (t * TOP_K + N_EXPERTS * (MOE_BLOCK - 1) + MOE_BLOCK - 1) // MOE_BLOCK
    n_pad = n_blocks * MOE_BLOCK
    experts = jnp.arange(N_EXPERTS, dtype=jnp.int32)
    dest = jnp.sum(jnp.where(idx[:, :, None] == experts, pad_start, 0), axis=-1) + pos
    dest_tiles = dest.reshape(TOP_K, t // MOE_TT, MOE_TT).transpose(1, 0, 2).reshape(t // MOE_TT, 1, TOP_K * MOE_TT)
    blk_start = jnp.arange(n_blocks, dtype=jnp.int32) * MOE_BLOCK
    blk_expert = jnp.minimum(jnp.sum((pad_end[None, :] <= blk_start[:, None]).astype(jnp.int32), axis=1),
                             N_EXPERTS - 1)
    n_used = (pad_end[-1:] // MOE_BLOCK).astype(jnp.int32)
    tail = n_used + jnp.arange(N_EXPERTS + 1, dtype=jnp.int32)
    fill = jnp.stack([jnp.concatenate([jnp.maximum(pad_end - MOE_BLOCK, 0), jnp.minimum(tail, n_blocks - 1) * MOE_BLOCK]),
                      jnp.concatenate([counts > 0, tail < n_blocks]).astype(jnp.int32)]).astype(jnp.int32)
    xs = _dispatch(h2p, dest_tiles, fill, n_pad)
    ys = _experts(xs, blk_expert, n_used, wg, wu, wd, layer)
    gates_rep = jnp.repeat(gates.T, ROW_CH, axis=0)
    return _combine(ys, dest_tiles, gates_rep, h2p, x_new, mods_l, sg.astype(BF16), su.astype(BF16),
                    sd.astype(BF16), fn, n_batch=n_batch, n_rows=n_rows, n_lat=n_lat, final=final)


def kernel(x, c, ctx, c_ctx, ada_w, ada_b, norm_mix, norm_ffn, ab_w_in, ab_w_out, na_rpb, gqa_q_gain,
           gqa_k_gain, diff_w_in, diff_w_out, diff_lq1, diff_lk1, diff_lq2, diff_lk2, diff_sub_gain,
           router_w, router_bias, expert_w_gate, expert_w_up, expert_w_down, shared_w_gate, shared_w_up,
           shared_w_down, final_norm):
    b, n, d = x.shape
    n_ctx = ctx.shape[1]
    depth = ada_w.shape[0]
    assert d == 2 * ROW_CH * LANES
    assert depth == 2 and n_ctx == TM and n % (NA_ROWS * GRID_W) == 0 and d % LANES == 0
    assert (n + n_ctx) % FLASH_TK == 0 and n % FLASH_TQ == 0 and n % MOE_TT == 0 and n_ctx % MOE_TT == 0

    cvec = jnp.concatenate([c, c_ctx[None], jnp.zeros((8 - b - 1, d), F32)], axis=0)
    mods = _mods(cvec, ada_w, ada_b)[:, :b + 1].reshape(depth, (b + 1) * 6, 1, d)
    cosf, sins = _rope_tables(n, n_ctx)
    gm = jnp.asarray(np.kron(np.eye(LANES // HEAD_DIM), np.full((HEAD_DIM, HEAD_DIM), 1.0 / HEAD_DIM)), BF16)
    ones = jnp.ones((1, LANES), F32)
    fn = final_norm.reshape(1, d)

    xa = jnp.concatenate([x, ctx], axis=1)

    w = ab_w_in[0]
    kb = [w[:, 2048 + HEAD_DIM * g: 2048 + HEAD_DIM * (g + 1)] for g in range(2)]
    vb = [w[:, 2176 + HEAD_DIM * g: 2176 + HEAD_DIM * (g + 1)] for g in range(2)]
    w0 = jnp.concatenate([w[:, :2048], kb[0], kb[0], kb[1], kb[1], vb[0], vb[0], vb[1], vb[1]],
                         axis=1).astype(BF16)
    plan0 = ([(None, False, True)] * 4 + [(None, False, False)] * 8 + [("q", True, True)] * 4
             + [("k", True, False)] * 2 + [(None, False, False)] * 2)
    qg = jnp.tile(gqa_q_gain[0].reshape(1, HEAD_DIM), (1, LANES // HEAD_DIM))
    kg = jnp.tile(gqa_k_gain[0].reshape(1, HEAD_DIM), (1, LANES // HEAD_DIM))
    qkv = _proj(xa, norm_mix[0].reshape(1, d), mods[0], 0, 1, w0, cosf, sins, gm, qg, kg, plan0, n)

    tl, tr = _na_bias_tiles(na_rpb[0])
    o_na = _na(qkv, tl, tr, n_lat=n, n_ctx=n_ctx, npairs=4, qc0=0, kc0=4, vc0=8)
    tbl_g = jnp.asarray([[12, 13, 14, 15], [16, 16, 17, 17], [18, 18, 19, 19]], jnp.int32)
    o_gqa = _flash(qkv, tbl_g, n_q=n, q_row0=0, n_keys=n + n_ctx, key_row0=0, tq=FLASH_TQ, tk=FLASH_TK)
    tbl_c = jnp.asarray([[0, 1, 2, 3, 12, 13, 14, 15], [4, 5, 6, 7, 16, 16, 17, 17],
                         [8, 9, 10, 11, 18, 18, 19, 19]], jnp.int32)
    o_ctx = _flash(qkv, tbl_c, n_q=n_ctx, q_row0=n, n_keys=n_ctx, key_row0=n, tq=n_ctx, tk=n_ctx)

    s_all = n + n_ctx
    x_new, h2p, lg = _outproj([o_na, o_gqa], o_ctx, ab_w_out[0].astype(BF16), xa, mods[0],
                              norm_ffn[0].reshape(1, d), router_w[0].T, n_lat=n, n_rows=s_all)
    xa = _moe_block(lg, h2p.reshape(-1, LANES), x_new.reshape(b * s_all, d), mods[0], router_bias[0],
                    expert_w_gate, expert_w_up, expert_w_down, shared_w_gate[0], shared_w_up[0],
                    shared_w_down[0], fn, layer=0, n_batch=b, n_rows=s_all, n_lat=n, final=False).reshape(b, s_all, d)

    lam_init = 0.8 - 0.6 * math.exp(-0.3 * 1)
    plan1 = [(None, True, True)] * 8 + [(None, True, False)] * 8 + [(None, False, False)] * 8
    qkv = _proj(xa, norm_mix[1].reshape(1, d), mods[1], 0, 1, diff_w_in[0].astype(BF16), cosf, sins, gm,
                ones, ones, plan1, n)
    tbl_d = jnp.asarray([list(range(0, 8)), list(range(8, 16)), list(range(16, 24))], jnp.int32)
    dp = [diff_lq1[0].reshape(1, HEAD_DIM), diff_lk1[0].reshape(1, HEAD_DIM),
          diff_lq2[0].reshape(1, HEAD_DIM), diff_lk2[0].reshape(1, HEAD_DIM),
          diff_sub_gain[0].reshape(1, LANES)]
    o_diff = _flash(qkv, tbl_d, n_q=n, q_row0=0, n_keys=n + n_ctx, key_row0=0, tq=FLASH_TQ, tk=FLASH_TK,
                    mode="diff", diff_params=dp, lam_init=lam_init)
    x_new, h2p, lg = _outproj([o_diff], None, diff_w_out[0].astype(BF16), xa, mods[1],
                              norm_ffn[1].reshape(1, d), router_w[1].T, n_lat=n, n_rows=n)
    return _moe_block(lg, h2p.reshape(-1, LANES), x_new.reshape(b * n, d), mods[1], router_bias[1],
                      expert_w_gate, expert_w_up, expert_w_down, shared_w_gate[1], shared_w_up[1],
                      shared_w_down[1], fn, layer=1, n_batch=b, n_rows=n, n_lat=n, final=True).reshape(b, n, d)
```

```python
import functools
import math

import jax
import jax.numpy as jnp
import numpy as np
from jax import lax
from jax.experimental import pallas as pl
from jax.experimental.pallas import tpu as pltpu

F32 = jnp.float32
BF16 = jnp.bfloat16
U32 = jnp.uint32

LANES = 128
HEAD_DIM = 64
GRID_W = 64
WIN_ROWS = 8
WIN_COLS = 16
ROPE_THETA = 10000.0
EPS = 1e-6
N_EXPERTS = 64
TOP_K = 8
ROUTED_SCALE = 2.5
NEG = -1e30
LOG2E = math.log2(math.e)
Q_SCALE = HEAD_DIM ** -0.5 * LOG2E
VMEM_LIMIT = 56 * 1024 * 1024

TM = 256
NA_ROWS = 8
NA_SPAN = 16
FLASH_TQ = 512
FLASH_TK = 2816
MOE_BLOCK = 512
MOE_TT = 128
DMA_BATCH = 2
HI_MASK = 0xFFFF0000


def _cparams(sem):
    return pltpu.CompilerParams(dimension_semantics=sem, vmem_limit_bytes=VMEM_LIMIT)


def _pack_bf16_pairs(x):
    w = x.shape[1] // 2
    bits = lax.bitcast_convert_type(x.astype(BF16).astype(F32), U32)
    return (bits[:, w:] & jnp.uint32(HI_MASK)) | (bits[:, :w] >> 16)


def _unpack_bf16_pairs(p):
    lo = lax.bitcast_convert_type(p << 16, F32)
    hi = lax.bitcast_convert_type(p & jnp.uint32(HI_MASK), F32)
    return lo, hi


ROW_CH = 4


def _store_rowchunks(ref, idx, packed):
    m = packed.shape[0]
    for c in range(ROW_CH):
        ref[idx + (pl.ds(c, m, stride=ROW_CH), slice(None))] = packed[:, c * LANES:(c + 1) * LANES]


def _load_rowchunks(ref, idx, m, dtype):
    planes = [_unpack_bf16_pairs(ref[idx + (pl.ds(c, m, stride=ROW_CH), slice(None))]) for c in range(ROW_CH)]
    return jnp.concatenate([lo.astype(dtype) for lo, _ in planes] + [hi.astype(dtype) for _, hi in planes], axis=1)


def _mods_kernel(c_ref, w_ref, b_ref, o_ref):
    c = c_ref[...]
    s = c * jax.nn.sigmoid(c)
    o_ref[0] = jnp.dot(s.astype(BF16), w_ref[0].astype(BF16), preferred_element_type=F32) + b_ref[0]


def _mods(cvec, ada_w, ada_b):
    depth, d, d6 = ada_w.shape
    tn = 1536
    return pl.pallas_call(
        _mods_kernel,
        grid=(depth, d6 // tn),
        in_specs=[pl.BlockSpec((8, d), lambda l, j: (0, 0)),
                  pl.BlockSpec((1, d, tn), lambda l, j: (l, 0, j)),
                  pl.BlockSpec((1, 1, tn), lambda l, j: (l, 0, j))],
        out_specs=pl.BlockSpec((1, 8, tn), lambda l, j: (l, 0, j)),
        out_shape=jax.ShapeDtypeStruct((depth, 8, d6), F32),
        compiler_params=_cparams(("arbitrary", "arbitrary")),
        name="adaln_mods",
    )(cvec, ada_w, ada_b.reshape(depth, 1, d6))


def _proj_kernel(x_ref, g_ref, sh_ref, sc_ref, w_ref, cos_ref, sin_ref, gm_ref, qg_ref, kg_ref,
                 o_ref, *, plan):
    x = x_ref[0]
    ms = jnp.mean(x * x, axis=-1, keepdims=True)
    h = x * lax.rsqrt(ms + EPS) * g_ref[...]
    h = h * (1.0 + sc_ref[0]) + sh_ref[0]
    y = jnp.dot(h.astype(BF16), w_ref[...], preferred_element_type=F32)
    cosf = cos_ref[...]
    sins = sin_ref[...]
    even = (lax.broadcasted_iota(jnp.int32, (1, LANES), 1) % 2) == 0
    for c, (norm, rope, scale) in enumerate(plan):
        yc = y[:, c * LANES:(c + 1) * LANES]
        if norm:
            ms2 = jnp.dot((yc * yc).astype(BF16), gm_ref[...], preferred_element_type=F32)
            gain = qg_ref[...] if norm == "q" else kg_ref[...]
            yc = yc * lax.rsqrt(ms2 + EPS) * gain
        if rope:
            sw = jnp.where(even, pltpu.roll(yc, LANES - 1, 1), pltpu.roll(yc, 1, 1))
            yc = yc * cosf + sw * sins
        if scale:
            yc = yc * Q_SCALE
        o_ref[0, :, c * LANES:(c + 1) * LANES] = yc.astype(BF16)


def _proj(xa, gain, mods, sh_idx, sc_idx, w, cosf, sins, gm, qg, kg, plan, n_lat):
    b, s, d = xa.shape
    wcols = w.shape[1]
    nt = s // TM
    lat_tiles = n_lat // TM

    def mod_map(chunk):
        return lambda bi, i: (jnp.where(i >= lat_tiles, b, bi) * 6 + chunk, 0, 0)

    return pl.pallas_call(
        functools.partial(_proj_kernel, plan=plan),
        grid=(b, nt),
        in_specs=[pl.BlockSpec((1, TM, d), lambda bi, i: (bi, i, 0)),
                  pl.BlockSpec((1, d), lambda bi, i: (0, 0)),
                  pl.BlockSpec((1, 1, d), mod_map(sh_idx)),
                  pl.BlockSpec((1, 1, d), mod_map(sc_idx)),
                  pl.BlockSpec((d, wcols), lambda bi, i: (0, 0)),
                  pl.BlockSpec((TM, LANES), lambda bi, i: (i, 0)),
                  pl.BlockSpec((TM, LANES), lambda bi, i: (i, 0)),
                  pl.BlockSpec((LANES, LANES), lambda bi, i: (0, 0)),
                  pl.BlockSpec((1, LANES), lambda bi, i: (0, 0)),
                  pl.BlockSpec((1, LANES), lambda bi, i: (0, 0))],
        out_specs=pl.BlockSpec((1, TM, wcols), lambda bi, i: (bi, i, 0)),
        out_shape=jax.ShapeDtypeStruct((b, s, wcols), BF16),
        compiler_params=_cparams(("parallel", "arbitrary")),
        name="norm_inproj",
    )(xa, gain, mods, mods, w, cosf, sins, gm, qg, kg)


def _flash_kernel(tbl_ref, *refs, tq, tk, n_keys, n_tiles, mode, lam_init):
    del tbl_ref
    refs = list(refs)
    q_ref, k_ref, v_ref = refs[:3]
    refs = refs[3:]
    if mode == "diff":
        lq1, lk1, lq2, lk2, sg_ref = refs[:5]
        refs = refs[5:]
    o_ref, qs_ref, s_ref, m_ref, acc_ref = refs

    lane = lax.broadcasted_iota(jnp.int32, (1, LANES), 1)
    lo = lane < HEAD_DIM
    for t in range(n_tiles):
        q = q_ref[0, t * tq:(t + 1) * tq, :]
        zero = jnp.zeros_like(q)
        qs_ref[t, 0:tq, :] = jnp.where(lo, q, zero)
        qs_ref[t, tq:2 * tq, :] = jnp.where(lo, zero, q)
    nb = tk // LANES
    nchunks = n_keys // tk
    ones = jnp.ones((tk, LANES), BF16)

    def qk(slot, t, j):
        off = pl.multiple_of(j * tk, tk)
        s_ref[slot] = lax.dot_general(qs_ref[t], k_ref[0, pl.ds(off, tk), :], (((1,), (1,)), ((), ())),
                                      preferred_element_type=F32)

    def softmax_pv(slot, j):
        off = pl.multiple_of(j * tk, tk)
        s = s_ref[slot]
        m_prev = m_ref[...]
        m_next = jnp.maximum(m_prev, jnp.max(s, axis=1, keepdims=True))
        alpha = jnp.exp2(m_prev - m_next)
        p = jnp.exp2(s - jnp.concatenate([m_next] * nb, axis=1))
        v1 = jnp.concatenate([v_ref[0, pl.ds(off, tk), :], ones], axis=1)
        acc_ref[...] = (jnp.concatenate([alpha, alpha], axis=1) * acc_ref[...]
                        + jnp.dot(p.astype(BF16), v1, preferred_element_type=F32))
        m_ref[...] = m_next

    def finalize(t):
        acc = acc_ref[...]
        o = acc[:, 0:LANES] / acc[:, LANES:2 * LANES]
        if mode == "pair":
            out = jnp.where(lo, o[0:tq], o[tq:2 * tq])
        else:
            lam = (jnp.exp(jnp.sum(lq1[...] * lk1[...], axis=1, keepdims=True))
                   - jnp.exp(jnp.sum(lq2[...] * lk2[...], axis=1, keepdims=True)) + lam_init)
            dlt = o[0:tq] - lam * o[tq:2 * tq]
            ms = jnp.mean(dlt * dlt, axis=-1, keepdims=True)
            out = dlt * lax.rsqrt(ms + EPS) * sg_ref[...] * (1.0 - lam_init)
        o_ref[0, pl.ds(pl.multiple_of(t * tq, tq), tq), :] = out.astype(o_ref.dtype)

    def tile(t, par, t_next):
        m_ref[...] = jnp.full(m_ref.shape, NEG, F32)
        acc_ref[...] = jnp.zeros(acc_ref.shape, F32)

        def body(jj, carry):
            j = 2 * jj
            qk(1 - par, t, j + 1)
            softmax_pv(par, j)
            qk(par, t, j + 2)
            softmax_pv(1 - par, j + 1)
            return carry

        lax.fori_loop(0, (nchunks - 1) // 2, body, 0)
        if nchunks % 2 == 0:
            qk(1 - par, t, nchunks - 1)
            softmax_pv(par, nchunks - 2)
            if t_next is not None:
                qk(par, t_next, 0)
            softmax_pv(1 - par, nchunks - 1)
            nxt = par
        else:
            if t_next is not None:
                qk(1 - par, t_next, 0)
            softmax_pv(par, nchunks - 1)
            nxt = 1 - par
        finalize(t)
        return nxt

    qk(0, 0, 0)
    if n_tiles == 1:
        tile(0, 0, None)
    else:
        def outer(ii, carry):
            ta = 2 * ii
            par = tile(ta, 0, ta + 1)
            par = tile(ta + 1, par, jnp.minimum(ta + 2, n_tiles - 1))
            assert par == 0
            return carry

        lax.fori_loop(0, n_tiles // 2, outer, 0)


def _flash(qkv, tbl, *, n_q, q_row0, n_keys, key_row0, tq, tk, mode="pair", diff_params=None, lam_init=0.0):
    b = qkv.shape[0]
    ncols = tbl.shape[1]
    n_tiles = n_q // tq
    assert n_tiles == 1 or n_tiles % 2 == 0
    qb0 = q_row0 // n_q
    kb0 = key_row0 // n_keys
    in_specs = [pl.BlockSpec((1, n_q, LANES), lambda bi, c, t: (bi, qb0, t[0, c])),
                pl.BlockSpec((1, n_keys, LANES), lambda bi, c, t: (bi, kb0, t[1, c])),
                pl.BlockSpec((1, n_keys, LANES), lambda bi, c, t: (bi, kb0, t[2, c]))]
    args = [qkv, qkv, qkv]
    if mode == "diff":
        in_specs += [pl.BlockSpec((1, HEAD_DIM), lambda bi, c, t: (0, 0))] * 4
        in_specs += [pl.BlockSpec((1, LANES), lambda bi, c, t: (0, 0))]
        args += list(diff_params)
    grid_spec = pltpu.PrefetchScalarGridSpec(
        num_scalar_prefetch=1,
        grid=(b, ncols),
        in_specs=in_specs,
        out_specs=pl.BlockSpec((1, n_q, LANES), lambda bi, c, t: (bi, 0, c)),
        scratch_shapes=[pltpu.VMEM((n_tiles, 2 * tq, LANES), BF16),
                        pltpu.VMEM((2, 2 * tq, tk), F32),
                        pltpu.VMEM((2 * tq, LANES), F32),
                        pltpu.VMEM((2 * tq, 2 * LANES), F32)])
    return pl.pallas_call(
        functools.partial(_flash_kernel, tq=tq, tk=tk, n_keys=n_keys, n_tiles=n_tiles, mode=mode,
                          lam_init=lam_init),
        grid_spec=grid_spec,
        out_shape=jax.ShapeDtypeStruct((b, n_q, ncols * LANES), BF16),
        compiler_params=_cparams(("parallel", "parallel")),
        name="flash_%s_%d" % (mode, n_keys),
    )(tbl, *args)


def _na_case_geometry(case, rows):
    r0 = {0: 0, 1: NA_ROWS, 2: rows - NA_ROWS}[case]
    start = min(max(r0 - WIN_ROWS // 2, 0), rows - NA_SPAN)
    return r0, start


def _na_tile_index(case, dr, dk, rows):
    r0, start = _na_case_geometry(case, rows)
    r, kr = r0 + dr, start + dk
    rs = min(max(r - WIN_ROWS // 2, 0), rows - WIN_ROWS)
    if rs <= kr < rs + WIN_ROWS:
        return kr - r + WIN_ROWS
    return 0


def _na_kernel(q_ref, k_ref, v_ref, kc_ref, vc_ref, tl_ref, tr_ref, o_ref, bias_ref, *, rows):
    nblk = rows // NA_ROWS
    bq = NA_ROWS * GRID_W
    bk = NA_SPAN * GRID_W
    for hh in range(2):
        for case in range(3):
            for dr in range(NA_ROWS):
                for dkp in range(NA_SPAN // 2):
                    ia = _na_tile_index(case, dr, 2 * dkp, rows)
                    ib = _na_tile_index(case, dr, 2 * dkp + 1, rows)
                    bias_ref[hh, case, dr * GRID_W:(dr + 1) * GRID_W, dkp * LANES:(dkp + 1) * LANES] = (
                        tl_ref[hh, ia] + tr_ref[hh, ib])

    lane = lax.broadcasted_iota(jnp.int32, (1, LANES), 1)
    lo = lane < HEAD_DIM
    kctx = kc_ref[0]
    vctx = vc_ref[0]
    nt = (((1,), (1,)), ((), ()))

    def body(i, carry):
        r0 = i * NA_ROWS
        start = jnp.clip(r0 - WIN_ROWS // 2, 0, rows - NA_SPAN)
        case = jnp.where(i == 0, 0, jnp.where(i == nblk - 1, 2, 1))
        qoff = pl.multiple_of(i * bq, bq)
        koff = pl.multiple_of(start * GRID_W, GRID_W)
        qb = q_ref[0, pl.ds(qoff, bq), :]
        ks = k_ref[0, pl.ds(koff, bk), :]
        vs = v_ref[0, pl.ds(koff, bk), :]
        zero = jnp.zeros_like(qb)
        outs = []
        for hh in range(2):
            qm = jnp.where(lo, qb, zero) if hh == 0 else jnp.where(lo, zero, qb)
            s_loc = lax.dot_general(qm, ks, nt, preferred_element_type=F32) + bias_ref[hh, case]
            s_ctx = lax.dot_general(qm, kctx, nt, preferred_element_type=F32)
            m = jnp.maximum(jnp.max(s_loc, axis=1, keepdims=True), jnp.max(s_ctx, axis=1, keepdims=True))
            p_loc = jnp.exp2(s_loc - m)
            p_ctx = jnp.exp2(s_ctx - m)
            l = jnp.sum(p_loc, axis=1, keepdims=True) + jnp.sum(p_ctx, axis=1, keepdims=True)
            o = (jnp.dot(p_loc.astype(BF16), vs, preferred_element_type=F32)
                 + jnp.dot(p_ctx.astype(BF16), vctx, preferred_element_type=F32))
            outs.append(o / l)
        o_ref[0, pl.ds(qoff, bq), :] = jnp.where(lo, outs[0], outs[1]).astype(o_ref.dtype)
        return carry

    lax.fori_loop(0, nblk, body, 0)


def _na(qkv, tl, tr, *, n_lat, n_ctx, npairs, qc0, kc0, vc0):
    b = qkv.shape[0]
    rows = n_lat // GRID_W
    cb0 = n_lat // n_ctx
    nt = tl.shape[1]
    return pl.pallas_call(
        functools.partial(_na_kernel, rows=rows),
        grid=(b, npairs),
        in_specs=[pl.BlockSpec((1, n_lat, LANES), lambda bi, j: (bi, 0, qc0 + j)),
                  pl.BlockSpec((1, n_lat, LANES), lambda bi, j: (bi, 0, kc0 + j)),
                  pl.BlockSpec((1, n_lat, LANES), lambda bi, j: (bi, 0, vc0 + j)),
                  pl.BlockSpec((1, n_ctx, LANES), lambda bi, j: (bi, cb0, kc0 + j)),
                  pl.BlockSpec((1, n_ctx, LANES), lambda bi, j: (bi, cb0, vc0 + j)),
                  pl.BlockSpec((2, nt, GRID_W, LANES), lambda bi, j: (j, 0, 0, 0)),
                  pl.BlockSpec((2, nt, GRID_W, LANES), lambda bi, j: (j, 0, 0, 0))],
        out_specs=pl.BlockSpec((1, n_lat, LANES), lambda bi, j: (bi, 0, j)),
        out_shape=jax.ShapeDtypeStruct((b, n_lat, npairs * LANES), BF16),
        scratch_shapes=[pltpu.VMEM((2, 3, NA_ROWS * GRID_W, NA_SPAN * GRID_W), F32)],
        compiler_params=_cparams(("parallel", "arbitrary")),
        name="neighbourhood_attn",
    )(qkv, qkv, qkv, qkv, qkv, tl, tr)


def _na_bias_tiles(rpb):
    h = rpb.shape[0]
    cols = np.arange(GRID_W)
    cs = np.clip(cols - WIN_COLS // 2, 0, GRID_W - WIN_COLS)
    kc = cols[None, :]
    valid = (kc >= cs[:, None]) & (kc < cs[:, None] + WIN_COLS)
    ci = np.clip(kc - cols[:, None] + WIN_COLS - 1, 0, 2 * WIN_COLS - 2)
    onehot = jnp.asarray((ci[None] == np.arange(2 * WIN_COLS - 1)[:, None, None]).astype(np.float32))
    vals = jnp.einsum("hrj,jck->hrck", rpb.astype(F32), onehot, precision=lax.Precision.HIGHEST)
    t = jnp.where(jnp.asarray(valid)[None, None], vals * LOG2E, NEG)
    t = jnp.concatenate([jnp.full((h, 1, GRID_W, GRID_W), NEG, F32), t], axis=1)
    z = jnp.zeros_like(t)
    return jnp.concatenate([t, z], axis=-1), jnp.concatenate([z, t], axis=-1)


def _outproj_kernel(*refs, n_parts, has_ctx, lat_tiles):
    refs = list(refs)
    parts = [refs.pop(0) for _ in range(n_parts)]
    octx_ref = refs.pop(0) if has_ctx else None
    w_ref, x_ref, gm_ref, ng_ref, sh_ref, sc_ref, rw_ref, xo_ref, hp_ref, lg_ref, proj_ref = refs

    def lat():
        acc = None
        off = 0
        for p in parts:
            wdt = p.shape[-1]
            t = jnp.dot(p[0], w_ref[off:off + wdt, :], preferred_element_type=F32)
            acc = t if acc is None else acc + t
            off += wdt
        proj_ref[...] = acc

    if has_ctx:
        is_ctx = pl.program_id(1) >= lat_tiles
        pl.when(jnp.logical_not(is_ctx))(lat)

        @pl.when(is_ctx)
        def _():
            proj_ref[...] = jnp.dot(octx_ref[0], w_ref[...], preferred_element_type=F32)
    else:
        lat()

    x = x_ref[0] + gm_ref[0] * proj_ref[...]
    xo_ref[0] = x
    ms = jnp.mean(x * x, axis=-1, keepdims=True)
    h = x * lax.rsqrt(ms + EPS) * ng_ref[...]
    h = h * (1.0 + sc_ref[0]) + sh_ref[0]
    _store_rowchunks(hp_ref, (0,), _pack_bf16_pairs(h))
    lg_ref[...] = lax.dot_general(rw_ref[...], h, (((1,), (1,)), ((), ())), preferred_element_type=F32,
                                  precision=lax.Precision.HIGHEST)


def _outproj(parts, octx, w, xa, mods, gain, rw_t, *, n_lat, n_rows):
    b, _, d = xa.shape
    lat_tiles = n_lat // TM
    nt = n_rows // TM
    has_ctx = octx is not None

    def mod_map(chunk):
        return lambda bi, i: (jnp.where(i >= lat_tiles, b, bi) * 6 + chunk, 0, 0)

    in_specs = [pl.BlockSpec((1, TM, p.shape[-1]), lambda bi, i: (bi, jnp.minimum(i, lat_tiles - 1), 0))
                for p in parts]
    args = list(parts)
    if has_ctx:
        in_specs.append(pl.BlockSpec((1, TM, d), lambda bi, i: (bi, 0, 0)))
        args.append(octx)
    in_specs += [pl.BlockSpec((d, d), lambda bi, i: (0, 0)),
                 pl.BlockSpec((1, TM, d), lambda bi, i: (bi, i, 0)),
                 pl.BlockSpec((1, 1, d), mod_map(2)),
                 pl.BlockSpec((1, d), lambda bi, i: (0, 0)),
                 pl.BlockSpec((1, 1, d), mod_map(3)),
                 pl.BlockSpec((1, 1, d), mod_map(4)),
                 pl.BlockSpec((N_EXPERTS, d), lambda bi, i: (0, 0))]
    args += [w, xa, mods, gain, mods, mods, rw_t]
    return pl.pallas_call(
        functools.partial(_outproj_kernel, n_parts=len(parts), has_ctx=has_ctx, lat_tiles=lat_tiles),
        grid=(b, nt),
        in_specs=in_specs,
        out_specs=[pl.BlockSpec((1, TM, d), lambda bi, i: (bi, i, 0)),
                   pl.BlockSpec((1, TM * ROW_CH, LANES), lambda bi, i: (bi, i, 0)),
                   pl.BlockSpec((N_EXPERTS, TM), lambda bi, i: (0, bi * nt + i))],
        out_shape=[jax.ShapeDtypeStruct((b, n_rows, d), F32),
                   jax.ShapeDtypeStruct((b, n_rows * ROW_CH, LANES), U32),
                   jax.ShapeDtypeStruct((N_EXPERTS, b * n_rows), F32)],
        scratch_shapes=[pltpu.VMEM((TM, d), F32)],
        compiler_params=_cparams(("parallel", "arbitrary")),
        name="outproj_ffnnorm",
    )(*args)


def _router_kernel(lg_ref, rb_ref, tri_ref, idx_ref, gate_ref, pos_ref, cnt_ref, run_ref):
    @pl.when(pl.program_id(0) == 0)
    def _():
        run_ref[...] = jnp.zeros(run_ref.shape, F32)

    scores = jax.nn.sigmoid(lg_ref[...])
    work = scores + rb_ref[...]
    eidx = lax.broadcasted_iota(jnp.int32, work.shape, 0)
    hits, idx_rows, sel_rows = [], [], []
    for _ in range(TOP_K):
        mx = jnp.max(work, axis=0, keepdims=True)
        first = jnp.min(jnp.where(work == mx, eidx, N_EXPERTS), axis=0, keepdims=True)
        hit = eidx == first
        hits.append(hit)
        idx_rows.append(first)
        sel_rows.append(jnp.sum(jnp.where(hit, scores, 0.0), axis=0, keepdims=True))
        work = jnp.where(hit, NEG, work)
    mask = jnp.zeros(work.shape, F32)
    for hit in hits:
        mask = mask + hit.astype(F32)
    denom = sel_rows[0]
    for r in sel_rows[1:]:
        denom = denom + r
    csum = jnp.dot(mask.astype(BF16), tri_ref[...], preferred_element_type=F32)
    tm = mask.shape[1]
    posall = run_ref[:, 0:1] + csum - mask
    pos_rows = [jnp.sum(jnp.where(hit, posall, 0.0), axis=0, keepdims=True) for hit in hits]
    run_ref[...] = run_ref[...] + csum[:, tm - 1:tm]
    idx_ref[...] = jnp.concatenate(idx_rows, axis=0)
    gate_ref[...] = jnp.concatenate(sel_rows, axis=0) / denom * ROUTED_SCALE
    pos_ref[...] = jnp.concatenate(pos_rows, axis=0).astype(jnp.int32)
    cnt_ref[...] = run_ref[...]


def _router(logits_t, rb):
    e, t = logits_t.shape
    tri = jnp.asarray(np.triu(np.ones((TM, TM), np.float32)), BF16)
    return pl.pallas_call(
        _router_kernel,
        grid=(t // TM,),
        in_specs=[pl.BlockSpec((e, TM), lambda i: (0, i)),
                  pl.BlockSpec((e, 1), lambda i: (0, 0)),
                  pl.BlockSpec((TM, TM), lambda i: (0, 0))],
        out_specs=[pl.BlockSpec((TOP_K, TM), lambda i: (0, i)),
                   pl.BlockSpec((TOP_K, TM), lambda i: (0, i)),
                   pl.BlockSpec((TOP_K, TM), lambda i: (0, i)),
                   pl.BlockSpec((e, LANES), lambda i: (0, 0))],
        out_shape=[jax.ShapeDtypeStruct((TOP_K, t), jnp.int32),
                   jax.ShapeDtypeStruct((TOP_K, t), F32),
                   jax.ShapeDtypeStruct((TOP_K, t), jnp.int32),
                   jax.ShapeDtypeStruct((e, LANES), F32)],
        scratch_shapes=[pltpu.VMEM((e, LANES), F32)],
        compiler_params=_cparams(("arbitrary",)),
        name="router_topk",
    )(logits_t, rb, tri)


def _dispatch_kernel(fill_ref, dest_ref, h_ref, xs_hbm, zbuf, fsem, sem):
    i = pl.program_id(0)

    def fill_copy(e):
        start = pl.multiple_of(fill_ref[0, e] * ROW_CH, MOE_BLOCK * ROW_CH)
        return pltpu.make_async_copy(zbuf, xs_hbm.at[pl.ds(start, MOE_BLOCK * ROW_CH), :], fsem)

    @pl.when(i == 0)
    def _():
        zbuf[...] = jnp.zeros(zbuf.shape, zbuf.dtype)
        for e in range(fill_ref.shape[1]):
            pl.when(fill_ref[1, e] != 0)(lambda e=e: fill_copy(e).start())
        for e in range(fill_ref.shape[1]):
            pl.when(fill_ref[1, e] != 0)(lambda e=e: fill_copy(e).wait())

    for tl0 in range(0, MOE_TT, DMA_BATCH):
        ds = [[dest_ref[0, 0, k * MOE_TT + tl0 + u] for k in range(TOP_K)] for u in range(DMA_BATCH)]
        for u in range(DMA_BATCH):
            src = h_ref.at[pl.ds(ROW_CH * (tl0 + u), ROW_CH), :]
            for k in range(TOP_K):
                d = pl.multiple_of(ds[u][k] * ROW_CH, ROW_CH)
                pltpu.make_async_copy(src, xs_hbm.at[pl.ds(d, ROW_CH), :], sem).start(priority=k % 2)
    for k in range(TOP_K):
        pltpu.make_async_copy(h_ref, xs_hbm.at[pl.ds(0, MOE_TT * ROW_CH), :], sem).wait()


def _dispatch(h2p, dest_tiles, fill, n_pad):
    grid_spec = pltpu.PrefetchScalarGridSpec(
        num_scalar_prefetch=1,
        grid=(h2p.shape[0] // (MOE_TT * ROW_CH),),
        in_specs=[pl.BlockSpec((1, 1, TOP_K * MOE_TT), lambda i, f: (i, 0, 0), memory_space=pltpu.SMEM),
                  pl.BlockSpec((MOE_TT * ROW_CH, LANES), lambda i, f: (i, 0))],
        out_specs=pl.BlockSpec(memory_space=pl.ANY),
        scratch_shapes=[pltpu.VMEM((MOE_BLOCK * ROW_CH, LANES), U32),
                        pltpu.SemaphoreType.DMA,
                        pltpu.SemaphoreType.DMA])
    return pl.pallas_call(
        _dispatch_kernel,
        grid_spec=grid_spec,
        out_shape=jax.ShapeDtypeStruct((n_pad * ROW_CH, LANES), U32),
        compiler_params=_cparams(("arbitrary",)),
        name="moe_dispatch",
    )(fill, dest_tiles, h2p)


def _expert_kernel(be_ref, nu_ref, xs_ref, wg_ref, wu_ref, wd_ref, ys_ref, wgb, wub, wdb):
    i = pl.program_id(0)

    @pl.when(i >= nu_ref[0])
    def _():
        ys_ref[...] = jnp.zeros(ys_ref.shape, ys_ref.dtype)

    @pl.when(i < nu_ref[0])
    def _():
        e = be_ref[i]
        prev = be_ref[jnp.maximum(i - 1, 0)]

        @pl.when(jnp.logical_or(i == 0, e != prev))
        def _():
            wgb[...] = wg_ref[0, 0].astype(BF16)
            wub[...] = wu_ref[0, 0].astype(BF16)
            wdb[...] = wd_ref[0, 0].astype(BF16)

        xb = _load_rowchunks(xs_ref, (), MOE_BLOCK, BF16)
        g = jnp.dot(xb, wgb[...], preferred_element_type=F32)
        u = jnp.dot(xb, wub[...], preferred_element_type=F32)
        hid = (g * jax.nn.sigmoid(g)) * u
        y = jnp.dot(hid.astype(BF16), wdb[...], preferred_element_type=F32)
        _store_rowchunks(ys_ref, (), _pack_bf16_pairs(y))


def _experts(xs, blk_expert, n_used, wg, wu, wd, layer):
    rows = xs.shape[0]
    _, _, d, f = wg.shape
    nblk = rows // (MOE_BLOCK * ROW_CH)

    def row_map(i, be, nu):
        return (jnp.minimum(i, nu[0] - 1), 0)

    grid_spec = pltpu.PrefetchScalarGridSpec(
        num_scalar_prefetch=2,
        grid=(nblk,),
        in_specs=[pl.BlockSpec((MOE_BLOCK * ROW_CH, LANES), row_map),
                  pl.BlockSpec((1, 1, d, f), lambda i, be, nu: (layer, be[i], 0, 0)),
                  pl.BlockSpec((1, 1, d, f), lambda i, be, nu: (layer, be[i], 0, 0)),
                  pl.BlockSpec((1, 1, f, d), lambda i, be, nu: (layer, be[i], 0, 0))],
        out_specs=pl.BlockSpec((MOE_BLOCK * ROW_CH, LANES), lambda i, be, nu: (i, 0)),
        scratch_shapes=[pltpu.VMEM((d, f), BF16), pltpu.VMEM((d, f), BF16), pltpu.VMEM((f, d), BF16)])
    return pl.pallas_call(
        _expert_kernel,
        grid_spec=grid_spec,
        out_shape=jax.ShapeDtypeStruct((rows, LANES), U32),
        compiler_params=_cparams(("arbitrary",)),
        name="moe_experts",
    )(blk_expert, n_used, xs, wg, wu, wd)


def _combine_kernel(dest_ref, dnext_ref, ys_hbm, gate_ref, hp_ref, x_ref, gf_ref, sg_ref, su_ref, sd_ref, fn_ref,
                    o_ref, buf, ysum, grep, sem, *, final):
    i = pl.program_id(0)
    slot = i % 2

    def issue(dref, s):
        for tl0 in range(0, MOE_TT, DMA_BATCH):
            ds = [[dref[0, 0, k * MOE_TT + tl0 + u] for k in range(TOP_K)] for u in range(DMA_BATCH)]
            for u in range(DMA_BATCH):
                for k in range(TOP_K):
                    d = pl.multiple_of(ds[u][k] * ROW_CH, ROW_CH)
                    pltpu.make_async_copy(ys_hbm.at[pl.ds(d, ROW_CH), :],
                                          buf.at[s, k, pl.ds(ROW_CH * (tl0 + u), ROW_CH), :],
                                          sem.at[s]).start(priority=k % 2)

    pl.when(i == 0)(lambda: issue(dest_ref, 0))
    pl.when(i + 1 < pl.num_programs(0))(lambda: issue(dnext_ref, 1 - slot))

    xb = _load_rowchunks(hp_ref, (), MOE_TT, BF16)
    g = jnp.dot(xb, sg_ref[...], preferred_element_type=F32)
    u = jnp.dot(xb, su_ref[...], preferred_element_type=F32)
    y = jnp.dot(((g * jax.nn.sigmoid(g)) * u).astype(BF16), sd_ref[...], preferred_element_type=F32)

    for k in range(TOP_K):
        pltpu.make_async_copy(ys_hbm.at[pl.ds(0, MOE_TT * ROW_CH), :], buf.at[slot, k], sem.at[slot]).wait()
    for c in range(ROW_CH):
        grep[pl.ds(c, MOE_TT, stride=ROW_CH), :] = gate_ref[...]
    gates = grep[...]
    lo_sum = None
    hi_sum = None
    for k in range(TOP_K):
        lo, hi = _unpack_bf16_pairs(buf[slot, k])
        gk = gates[:, k:k + 1]
        lo_sum = gk * lo if lo_sum is None else lo_sum + gk * lo
        hi_sum = gk * hi if hi_sum is None else hi_sum + gk * hi
    ysum[0] = lo_sum
    ysum[1] = hi_sum
    routed = jnp.concatenate([ysum[half, pl.ds(c, MOE_TT, stride=ROW_CH), :]
                              for half in range(2) for c in range(ROW_CH)], axis=1)
    x = x_ref[...] + gf_ref[0] * (y + routed)
    if final:
        ms = jnp.mean(x * x, axis=-1, keepdims=True)
        x = x * lax.rsqrt(ms + EPS) * fn_ref[...]
    o_ref[...] = x


def _combine(ys, dest_tiles, gates_t, h2p, x_new, mods, sg, su, sd, fn, *, n_batch, n_rows, n_lat, final):
    t, d = x_new.shape
    f = sg.shape[1]
    tiles_b = n_rows // MOE_TT
    lat_tiles = n_lat // MOE_TT
    n_tiles = t // MOE_TT

    def mod_map(i):
        return (jnp.where(i % tiles_b >= lat_tiles, n_batch, i // tiles_b) * 6 + 5, 0, 0)

    grid_spec = pltpu.PrefetchScalarGridSpec(
        num_scalar_prefetch=0,
        grid=(t // MOE_TT,),
        in_specs=[pl.BlockSpec((1, 1, TOP_K * MOE_TT), lambda i: (i, 0, 0), memory_space=pltpu.SMEM),
                  pl.BlockSpec((1, 1, TOP_K * MOE_TT), lambda i: (jnp.minimum(i + 1, n_tiles - 1), 0, 0),
                               memory_space=pltpu.SMEM),
                  pl.BlockSpec(memory_space=pl.ANY),
                  pl.BlockSpec((MOE_TT, TOP_K), lambda i: (i, 0)),
                  pl.BlockSpec((MOE_TT * ROW_CH, LANES), lambda i: (i, 0)),
                  pl.BlockSpec((MOE_TT, d), lambda i: (i, 0)),
                  pl.BlockSpec((1, 1, d), mod_map),
                  pl.BlockSpec((d, f), lambda i: (0, 0)),
                  pl.BlockSpec((d, f), lambda i: (0, 0)),
                  pl.BlockSpec((f, d), lambda i: (0, 0)),
                  pl.BlockSpec((1, d), lambda i: (0, 0))],
        out_specs=pl.BlockSpec((MOE_TT, d), lambda i: (i, 0)),
        scratch_shapes=[pltpu.VMEM((2, TOP_K, MOE_TT * ROW_CH, LANES), U32),
                        pltpu.VMEM((2, MOE_TT * ROW_CH, LANES), F32),
                        pltpu.VMEM((MOE_TT * ROW_CH, TOP_K), F32),
                        pltpu.SemaphoreType.DMA((2,))])
    return pl.pallas_call(
        functools.partial(_combine_kernel, final=final),
        grid_spec=grid_spec,
        out_shape=jax.ShapeDtypeStruct((t, d), F32),
        compiler_params=_cparams(("arbitrary",)),
        name="moe_combine",
    )(dest_tiles, dest_tiles, ys, gates_t, h2p, x_new, mods, sg, su, sd, fn)


def _rope_tables(n_lat, n_ctx):
    t = np.arange(n_lat)
    row = (t // GRID_W).astype(np.float32)
    col = (t % GRID_W).astype(np.float32)
    npairs = HEAD_DIM // 4
    inv_freq = jnp.asarray(ROPE_THETA, F32) ** (-jnp.arange(npairs, dtype=F32) / npairs)
    ang = jnp.concatenate([jnp.asarray(row)[:, None] * inv_freq, jnp.asarray(col)[:, None] * inv_freq], axis=-1)
    cos = jnp.repeat(jnp.cos(ang), 2, axis=-1)
    sin = jnp.repeat(jnp.sin(ang), 2, axis=-1)
    sign = jnp.asarray(np.tile(np.array([-1.0, 1.0], np.float32), HEAD_DIM // 2))
    cosf = jnp.tile(cos, (1, LANES // HEAD_DIM))
    sins = jnp.tile(sin * sign, (1, LANES // HEAD_DIM))
    cosf = jnp.concatenate([cosf, jnp.ones((n_ctx, LANES), F32)], axis=0)
    sins = jnp.concatenate([sins, jnp.zeros((n_ctx, LANES), F32)], axis=0)
    return cosf, sins


def _moe_block(logits_t, h2p, x_new, mods_l, rb, wg, wu, wd, sg, su, sd, fn, *, layer, n_batch, n_rows, n_lat,
               final):
    t = x_new.shape[0]
    idx, gates, pos, cnt = _router(logits_t, rb.reshape(N_EXPERTS, 1))
    counts = cnt[:, 0].astype(jnp.int32)
    padded = (counts + MOE_BLOCK - 1) // MOE_BLOCK * MOE_BLOCK
    pad_end = jnp.cumsum(padded)
    pad_start = pad_end - padded
    n_blocks = (t * TOP_K + N_EXPERTS * (MOE_BLOCK - 1) + MOE_BLOCK - 1) // MOE_BLOCK
    n_pad = n_blocks * MOE_BLOCK
    experts = jnp.arange(N_EXPERTS, dtype=jnp.int32)
    dest = jnp.sum(jnp.where(idx[:, :, None] == experts, pad_start, 0), axis=-1) + pos
    dest_tiles = dest.reshape(TOP_K, t // MOE_TT, MOE_TT).transpose(1, 0, 2).reshape(t // MOE_TT, 1, TOP_K * MOE_TT)
    blk_start = jnp.arange(n_blocks, dtype=jnp.int32) * MOE_BLOCK
    blk_expert = jnp.minimum(jnp.sum((pad_end[None, :] <= blk_start[:, None]).astype(jnp.int32), axis=1),
                             N_EXPERTS - 1)
    n_used = (pad_end[-1:] // MOE_BLOCK).astype(jnp.int32)
    tail = n_used + jnp.arange(N_EXPERTS + 1, dtype=jnp.int32)
    fill = jnp.stack([jnp.concatenate([jnp.maximum(pad_end - MOE_BLOCK, 0), jnp.minimum(tail, n_blocks - 1) * MOE_BLOCK]),
                      jnp.concatenate([counts > 0, tail < n_blocks]).astype(jnp.int32)]).astype(jnp.int32)
    xs = _dispatch(h2p, dest_tiles, fill, n_pad)
    ys = _experts(xs, blk_expert, n_used, wg, wu, wd, layer)
    return _combine(ys, dest_tiles, gates.T, h2p, x_new, mods_l, sg.astype(BF16), su.astype(BF16),
                    sd.astype(BF16), fn, n_batch=n_batch, n_rows=n_rows, n_lat=n_lat, final=final)


def kernel(x, c, ctx, c_ctx, ada_w, ada_b, norm_mix, norm_ffn, ab_w_in, ab_w_out, na_rpb, gqa_q_gain,
           gqa_k_gain, diff_w_in, diff_w_out, diff_lq1, diff_lk1, diff_lq2, diff_lk2, diff_sub_gain,
           router_w, router_bias, expert_w_gate, expert_w_up, expert_w_down, shared_w_gate, shared_w_up,
           shared_w_down, final_norm):
    b, n, d = x.shape
    n_ctx = ctx.shape[1]
    depth = ada_w.shape[0]
    assert d == 2 * ROW_CH * LANES
    assert depth == 2 and n_ctx == TM and n % (NA_ROWS * GRID_W) == 0 and d % LANES == 0
    assert (n + n_ctx) % FLASH_TK == 0 and n % FLASH_TQ == 0 and n % MOE_TT == 0 and n_ctx % MOE_TT == 0

    cvec = jnp.concatenate([c, c_ctx[None], jnp.zeros((8 - b - 1, d), F32)], axis=0)
    mods = _mods(cvec, ada_w, ada_b)[:, :b + 1].reshape(depth, (b + 1) * 6, 1, d)
    cosf, sins = _rope_tables(n, n_ctx)
    gm = jnp.asarray(np.kron(np.eye(LANES // HEAD_DIM), np.full((HEAD_DIM, HEAD_DIM), 1.0 / HEAD_DIM)), BF16)
    ones = jnp.ones((1, LANES), F32)
    fn = final_norm.reshape(1, d)

    xa = jnp.concatenate([x, ctx], axis=1)

    w = ab_w_in[0]
    kb = [w[:, 2048 + HEAD_DIM * g: 2048 + HEAD_DIM * (g + 1)] for g in range(2)]
    vb = [w[:, 2176 + HEAD_DIM * g: 2176 + HEAD_DIM * (g + 1)] for g in range(2)]
    w0 = jnp.concatenate([w[:, :2048], kb[0], kb[0], kb[1], kb[1], vb[0], vb[0], vb[1], vb[1]],
                         axis=1).astype(BF16)
    plan0 = ([(None, False, True)] * 4 + [(None, False, False)] * 8 + [("q", True, True)] * 4
             + [("k", True, False)] * 2 + [(None, False, False)] * 2)
    qg = jnp.tile(gqa_q_gain[0].reshape(1, HEAD_DIM), (1, LANES // HEAD_DIM))
    kg = jnp.tile(gqa_k_gain[0].reshape(1, HEAD_DIM), (1, LANES // HEAD_DIM))
    qkv = _proj(xa, norm_mix[0].reshape(1, d), mods[0], 0, 1, w0, cosf, sins, gm, qg, kg, plan0, n)

    tl, tr = _na_bias_tiles(na_rpb[0])
    o_na = _na(qkv, tl, tr, n_lat=n, n_ctx=n_ctx, npairs=4, qc0=0, kc0=4, vc0=8)
    tbl_g = jnp.asarray([[12, 13, 14, 15], [16, 16, 17, 17], [18, 18, 19, 19]], jnp.int32)
    o_gqa = _flash(qkv, tbl_g, n_q=n, q_row0=0, n_keys=n + n_ctx, key_row0=0, tq=FLASH_TQ, tk=FLASH_TK)
    tbl_c = jnp.asarray([[0, 1, 2, 3, 12, 13, 14, 15], [4, 5, 6, 7, 16, 16, 17, 17],
                         [8, 9, 10, 11, 18, 18, 19, 19]], jnp.int32)
    o_ctx = _flash(qkv, tbl_c, n_q=n_ctx, q_row0=n, n_keys=n_ctx, key_row0=n, tq=n_ctx, tk=n_ctx)

    s_all = n + n_ctx
    x_new, h2p, lg = _outproj([o_na, o_gqa], o_ctx, ab_w_out[0].astype(BF16), xa, mods[0],
                              norm_ffn[0].reshape(1, d), router_w[0].T, n_lat=n, n_rows=s_all)
    xa = _moe_block(lg, h2p.reshape(-1, LANES), x_new.reshape(b * s_all, d), mods[0], router_bias[0],
                    expert_w_gate, expert_w_up, expert_w_down, shared_w_gate[0], shared_w_up[0],
                    shared_w_down[0], fn, layer=0, n_batch=b, n_rows=s_all, n_lat=n, final=False).reshape(b, s_all, d)

    lam_init = 0.8 - 0.6 * math.exp(-0.3 * 1)
    plan1 = [(None, True, True)] * 8 + [(None, True, False)] * 8 + [(None, False, False)] * 8
    qkv = _proj(xa, norm_mix[1].reshape(1, d), mods[1], 0, 1, diff_w_in[0].astype(BF16), cosf, sins, gm,
                ones, ones, plan1, n)
    tbl_d = jnp.asarray([list(range(0, 8)), list(range(8, 16)), list(range(16, 24))], jnp.int32)
    dp = [diff_lq1[0].reshape(1, HEAD_DIM), diff_lk1[0].reshape(1, HEAD_DIM),
          diff_lq2[0].reshape(1, HEAD_DIM), diff_lk2[0].reshape(1, HEAD_DIM),
          diff_sub_gain[0].reshape(1, LANES)]
    o_diff = _flash(qkv, tbl_d, n_q=n, q_row0=0, n_keys=n + n_ctx, key_row0=0, tq=FLASH_TQ, tk=FLASH_TK,
                    mode="diff", diff_params=dp, lam_init=lam_init)
    x_new, h2p, lg = _outproj([o_diff], None, diff_w_out[0].astype(BF16), xa, mods[1],
                              norm_ffn[1].reshape(1, d), router_w[1].T, n_lat=n, n_rows=n)
    return _moe_block(lg, h2p.reshape(-1, LANES), x_new.reshape(b * n, d), mods[1], router_bias[1],
                      expert_w_gate, expert_w_up, expert_w_down, shared_w_gate[1], shared_w_up[1],
                      shared_w_down[1], fn, layer=1, n_batch=b, n_rows=n, n_lat=n, final=True).reshape(b, n, d)
```

```python
import functools
import math

import jax
import jax.numpy as jnp
import numpy as np
from jax import lax
from jax.experimental import pallas as pl
from jax.experimental.pallas import tpu as pltpu

F32 = jnp.float32
BF16 = jnp.bfloat16
U32 = jnp.uint32

LANES = 128
HEAD_DIM = 64
GRID_W = 64
WIN_ROWS = 8
WIN_COLS = 16
ROPE_THETA = 10000.0
EPS = 1e-6
N_EXPERTS = 64
TOP_K = 8
ROUTED_SCALE = 2.5
NEG = -1e30
LOG2E = math.log2(math.e)
Q_SCALE = HEAD_DIM ** -0.5 * LOG2E
VMEM_LIMIT = 56 * 1024 * 1024

TM = 256
NA_ROWS = 8
NA_SPAN = 16
FLASH_TQ = 512
FLASH_TK = 2816
MOE_BLOCK = 512
MOE_TT = 128
DISPATCH_TT = 256
DMA_BATCH = 2
HI_MASK = 0xFFFF0000


def _cparams(sem):
    return pltpu.CompilerParams(dimension_semantics=sem, vmem_limit_bytes=VMEM_LIMIT)


def _pack_bf16_pairs(x):
    w = x.shape[1] // 2
    bits = lax.bitcast_convert_type(x.astype(BF16).astype(F32), U32)
    return (bits[:, w:] & jnp.uint32(HI_MASK)) | (bits[:, :w] >> 16)


def _unpack_bf16_pairs(p):
    lo = lax.bitcast_convert_type(p << 16, F32)
    hi = lax.bitcast_convert_type(p & jnp.uint32(HI_MASK), F32)
    return lo, hi


ROW_CH = 4


def _store_rowchunks(ref, idx, packed):
    m = packed.shape[0]
    for c in range(ROW_CH):
        ref[idx + (pl.ds(c, m, stride=ROW_CH), slice(None))] = packed[:, c * LANES:(c + 1) * LANES]


def _load_rowchunks(ref, idx, m, dtype):
    planes = [_unpack_bf16_pairs(ref[idx + (pl.ds(c, m, stride=ROW_CH), slice(None))]) for c in range(ROW_CH)]
    return jnp.concatenate([lo.astype(dtype) for lo, _ in planes] + [hi.astype(dtype) for _, hi in planes], axis=1)


def _mods_kernel(c_ref, w_ref, b_ref, o_ref):
    c = c_ref[...]
    s = c * jax.nn.sigmoid(c)
    o_ref[0] = jnp.dot(s.astype(BF16), w_ref[0].astype(BF16), preferred_element_type=F32) + b_ref[0]


def _mods(cvec, ada_w, ada_b):
    depth, d, d6 = ada_w.shape
    tn = 1536
    return pl.pallas_call(
        _mods_kernel,
        grid=(depth, d6 // tn),
        in_specs=[pl.BlockSpec((8, d), lambda l, j: (0, 0)),
                  pl.BlockSpec((1, d, tn), lambda l, j: (l, 0, j)),
                  pl.BlockSpec((1, 1, tn), lambda l, j: (l, 0, j))],
        out_specs=pl.BlockSpec((1, 8, tn), lambda l, j: (l, 0, j)),
        out_shape=jax.ShapeDtypeStruct((depth, 8, d6), F32),
        compiler_params=_cparams(("arbitrary", "arbitrary")),
        name="adaln_mods",
    )(cvec, ada_w, ada_b.reshape(depth, 1, d6))


def _proj_kernel(x_ref, g_ref, sh_ref, sc_ref, w_ref, cos_ref, sin_ref, gm_ref, qg_ref, kg_ref,
                 o_ref, *, plan):
    x = x_ref[0]
    ms = jnp.mean(x * x, axis=-1, keepdims=True)
    h = x * lax.rsqrt(ms + EPS) * g_ref[...]
    h = h * (1.0 + sc_ref[0]) + sh_ref[0]
    y = jnp.dot(h.astype(BF16), w_ref[...], preferred_element_type=F32)
    cosf = cos_ref[...]
    sins = sin_ref[...]
    even = (lax.broadcasted_iota(jnp.int32, (1, LANES), 1) % 2) == 0
    for c, (norm, rope, scale) in enumerate(plan):
        yc = y[:, c * LANES:(c + 1) * LANES]
        if norm:
            ms2 = jnp.dot((yc * yc).astype(BF16), gm_ref[...], preferred_element_type=F32)
            gain = qg_ref[...] if norm == "q" else kg_ref[...]
            yc = yc * lax.rsqrt(ms2 + EPS) * gain
        if rope:
            sw = jnp.where(even, pltpu.roll(yc, LANES - 1, 1), pltpu.roll(yc, 1, 1))
            yc = yc * cosf + sw * sins
        if scale:
            yc = yc * Q_SCALE
        o_ref[0, :, c * LANES:(c + 1) * LANES] = yc.astype(BF16)


def _proj(xa, gain, mods, sh_idx, sc_idx, w, cosf, sins, gm, qg, kg, plan, n_lat):
    b, s, d = xa.shape
    wcols = w.shape[1]
    nt = s // TM
    lat_tiles = n_lat // TM

    def mod_map(chunk):
        return lambda bi, i: (jnp.where(i >= lat_tiles, b, bi) * 6 + chunk, 0, 0)

    return pl.pallas_call(
        functools.partial(_proj_kernel, plan=plan),
        grid=(b, nt),
        in_specs=[pl.BlockSpec((1, TM, d), lambda bi, i: (bi, i, 0)),
                  pl.BlockSpec((1, d), lambda bi, i: (0, 0)),
                  pl.BlockSpec((1, 1, d), mod_map(sh_idx)),
                  pl.BlockSpec((1, 1, d), mod_map(sc_idx)),
                  pl.BlockSpec((d, wcols), lambda bi, i: (0, 0)),
                  pl.BlockSpec((TM, LANES), lambda bi, i: (i, 0)),
                  pl.BlockSpec((TM, LANES), lambda bi, i: (i, 0)),
                  pl.BlockSpec((LANES, LANES), lambda bi, i: (0, 0)),
                  pl.BlockSpec((1, LANES), lambda bi, i: (0, 0)),
                  pl.BlockSpec((1, LANES), lambda bi, i: (0, 0))],
        out_specs=pl.BlockSpec((1, TM, wcols), lambda bi, i: (bi, i, 0)),
        out_shape=jax.ShapeDtypeStruct((b, s, wcols), BF16),
        compiler_params=_cparams(("parallel", "arbitrary")),
        name="norm_inproj",
    )(xa, gain, mods, mods, w, cosf, sins, gm, qg, kg)


def _flash_kernel(tbl_ref, *refs, tq, tk, n_keys, n_tiles, mode, lam_init):
    del tbl_ref
    refs = list(refs)
    q_ref, k_ref, v_ref = refs[:3]
    refs = refs[3:]
    if mode == "diff":
        lq1, lk1, lq2, lk2, sg_ref = refs[:5]
        refs = refs[5:]
    o_ref, qs_ref, s_ref, m_ref, acc_ref = refs

    lane = lax.broadcasted_iota(jnp.int32, (1, LANES), 1)
    lo = lane < HEAD_DIM
    for t in range(n_tiles):
        q = q_ref[0, t * tq:(t + 1) * tq, :]
        zero = jnp.zeros_like(q)
        qs_ref[t, 0:tq, :] = jnp.where(lo, q, zero)
        qs_ref[t, tq:2 * tq, :] = jnp.where(lo, zero, q)
    nb = tk // LANES
    nchunks = n_keys // tk
    ones = jnp.ones((tk, LANES), BF16)

    def qk(slot, t, j):
        off = pl.multiple_of(j * tk, tk)
        s_ref[slot] = lax.dot_general(qs_ref[t], k_ref[0, pl.ds(off, tk), :], (((1,), (1,)), ((), ())),
                                      preferred_element_type=F32)

    def softmax_pv(slot, j):
        off = pl.multiple_of(j * tk, tk)
        s = s_ref[slot]
        m_prev = m_ref[...]
        m_next = jnp.maximum(m_prev, jnp.max(s, axis=1, keepdims=True))
        alpha = jnp.exp2(m_prev - m_next)
        p = jnp.exp2(s - jnp.concatenate([m_next] * nb, axis=1))
        v1 = jnp.concatenate([v_ref[0, pl.ds(off, tk), :], ones], axis=1)
        acc_ref[...] = (jnp.concatenate([alpha, alpha], axis=1) * acc_ref[...]
                        + jnp.dot(p.astype(BF16), v1, preferred_element_type=F32))
        m_ref[...] = m_next

    def finalize(t):
        acc = acc_ref[...]
        o = acc[:, 0:LANES] / acc[:, LANES:2 * LANES]
        if mode == "pair":
            out = jnp.where(lo, o[0:tq], o[tq:2 * tq])
        else:
            lam = (jnp.exp(jnp.sum(lq1[...] * lk1[...], axis=1, keepdims=True))
                   - jnp.exp(jnp.sum(lq2[...] * lk2[...], axis=1, keepdims=True)) + lam_init)
            dlt = o[0:tq] - lam * o[tq:2 * tq]
            ms = jnp.mean(dlt * dlt, axis=-1, keepdims=True)
            out = dlt * lax.rsqrt(ms + EPS) * sg_ref[...] * (1.0 - lam_init)
        o_ref[0, pl.ds(pl.multiple_of(t * tq, tq), tq), :] = out.astype(o_ref.dtype)

    def tile(t, par, t_next):
        m_ref[...] = jnp.full(m_ref.shape, NEG, F32)
        acc_ref[...] = jnp.zeros(acc_ref.shape, F32)

        def body(jj, carry):
            j = 2 * jj
            qk(1 - par, t, j + 1)
            softmax_pv(par, j)
            qk(par, t, j + 2)
            softmax_pv(1 - par, j + 1)
            return carry

        lax.fori_loop(0, (nchunks - 1) // 2, body, 0)
        if nchunks % 2 == 0:
            qk(1 - par, t, nchunks - 1)
            softmax_pv(par, nchunks - 2)
            if t_next is not None:
                qk(par, t_next, 0)
            softmax_pv(1 - par, nchunks - 1)
            nxt = par
        else:
            if t_next is not None:
                qk(1 - par, t_next, 0)
            softmax_pv(par, nchunks - 1)
            nxt = 1 - par
        finalize(t)
        return nxt

    qk(0, 0, 0)
    if n_tiles == 1:
        tile(0, 0, None)
    else:
        def outer(ii, carry):
            ta = 2 * ii
            par = tile(ta, 0, ta + 1)
            par = tile(ta + 1, par, jnp.minimum(ta + 2, n_tiles - 1))
            assert par == 0
            return carry

        lax.fori_loop(0, n_tiles // 2, outer, 0)


def _flash(qkv, tbl, *, n_q, q_row0, n_keys, key_row0, tq, tk, mode="pair", diff_params=None, lam_init=0.0):
    b = qkv.shape[0]
    ncols = tbl.shape[1]
    n_tiles = n_q // tq
    assert n_tiles == 1 or n_tiles % 2 == 0
    qb0 = q_row0 // n_q
    kb0 = key_row0 // n_keys
    in_specs = [pl.BlockSpec((1, n_q, LANES), lambda bi, c, t: (bi, qb0, t[0, c])),
                pl.BlockSpec((1, n_keys, LANES), lambda bi, c, t: (bi, kb0, t[1, c])),
                pl.BlockSpec((1, n_keys, LANES), lambda bi, c, t: (bi, kb0, t[2, c]))]
    args = [qkv, qkv, qkv]
    if mode == "diff":
        in_specs += [pl.BlockSpec((1, HEAD_DIM), lambda bi, c, t: (0, 0))] * 4
        in_specs += [pl.BlockSpec((1, LANES), lambda bi, c, t: (0, 0))]
        args += list(diff_params)
    grid_spec = pltpu.PrefetchScalarGridSpec(
        num_scalar_prefetch=1,
        grid=(b, ncols),
        in_specs=in_specs,
        out_specs=pl.BlockSpec((1, n_q, LANES), lambda bi, c, t: (bi, 0, c)),
        scratch_shapes=[pltpu.VMEM((n_tiles, 2 * tq, LANES), BF16),
                        pltpu.VMEM((2, 2 * tq, tk), F32),
                        pltpu.VMEM((2 * tq, LANES), F32),
                        pltpu.VMEM((2 * tq, 2 * LANES), F32)])
    return pl.pallas_call(
        functools.partial(_flash_kernel, tq=tq, tk=tk, n_keys=n_keys, n_tiles=n_tiles, mode=mode,
                          lam_init=lam_init),
        grid_spec=grid_spec,
        out_shape=jax.ShapeDtypeStruct((b, n_q, ncols * LANES), BF16),
        compiler_params=_cparams(("parallel", "parallel")),
        name="flash_%s_%d" % (mode, n_keys),
    )(tbl, *args)


def _na_case_geometry(case, rows):
    r0 = {0: 0, 1: NA_ROWS, 2: rows - NA_ROWS}[case]
    start = min(max(r0 - WIN_ROWS // 2, 0), rows - NA_SPAN)
    return r0, start


def _na_tile_index(case, dr, dk, rows):
    r0, start = _na_case_geometry(case, rows)
    r, kr = r0 + dr, start + dk
    rs = min(max(r - WIN_ROWS // 2, 0), rows - WIN_ROWS)
    if rs <= kr < rs + WIN_ROWS:
        return kr - r + WIN_ROWS
    return 0


def _na_kernel(q_ref, k_ref, v_ref, kc_ref, vc_ref, tl_ref, tr_ref, o_ref, bias_ref, *, rows):
    nblk = rows // NA_ROWS
    bq = NA_ROWS * GRID_W
    bk = NA_SPAN * GRID_W
    for hh in range(2):
        for case in range(3):
            for dr in range(NA_ROWS):
                for dkp in range(NA_SPAN // 2):
                    ia = _na_tile_index(case, dr, 2 * dkp, rows)
                    ib = _na_tile_index(case, dr, 2 * dkp + 1, rows)
                    bias_ref[hh, case, dr * GRID_W:(dr + 1) * GRID_W, dkp * LANES:(dkp + 1) * LANES] = (
                        tl_ref[hh, ia] + tr_ref[hh, ib])

    lane = lax.broadcasted_iota(jnp.int32, (1, LANES), 1)
    lo = lane < HEAD_DIM
    kctx = kc_ref[0]
    vctx = vc_ref[0]
    nt = (((1,), (1,)), ((), ()))

    def body(i, carry):
        r0 = i * NA_ROWS
        start = jnp.clip(r0 - WIN_ROWS // 2, 0, rows - NA_SPAN)
        case = jnp.where(i == 0, 0, jnp.where(i == nblk - 1, 2, 1))
        qoff = pl.multiple_of(i * bq, bq)
        koff = pl.multiple_of(start * GRID_W, GRID_W)
        qb = q_ref[0, pl.ds(qoff, bq), :]
        ks = k_ref[0, pl.ds(koff, bk), :]
        vs = v_ref[0, pl.ds(koff, bk), :]
        zero = jnp.zeros_like(qb)
        outs = []
        for hh in range(2):
            qm = jnp.where(lo, qb, zero) if hh == 0 else jnp.where(lo, zero, qb)
            s_loc = lax.dot_general(qm, ks, nt, preferred_element_type=F32) + bias_ref[hh, case]
            s_ctx = lax.dot_general(qm, kctx, nt, preferred_element_type=F32)
            m = jnp.maximum(jnp.max(s_loc, axis=1, keepdims=True), jnp.max(s_ctx, axis=1, keepdims=True))
            p_loc = jnp.exp2(s_loc - m)
            p_ctx = jnp.exp2(s_ctx - m)
            l = jnp.sum(p_loc, axis=1, keepdims=True) + jnp.sum(p_ctx, axis=1, keepdims=True)
            o = (jnp.dot(p_loc.astype(BF16), vs, preferred_element_type=F32)
                 + jnp.dot(p_ctx.astype(BF16), vctx, preferred_element_type=F32))
            outs.append(o / l)
        o_ref[0, pl.ds(qoff, bq), :] = jnp.where(lo, outs[0], outs[1]).astype(o_ref.dtype)
        return carry

    lax.fori_loop(0, nblk, body, 0)


def _na(qkv, tl, tr, *, n_lat, n_ctx, npairs, qc0, kc0, vc0):
    b = qkv.shape[0]
    rows = n_lat // GRID_W
    cb0 = n_lat // n_ctx
    nt = tl.shape[1]
    return pl.pallas_call(
        functools.partial(_na_kernel, rows=rows),
        grid=(b, npairs),
        in_specs=[pl.BlockSpec((1, n_lat, LANES), lambda bi, j: (bi, 0, qc0 + j)),
                  pl.BlockSpec((1, n_lat, LANES), lambda bi, j: (bi, 0, kc0 + j)),
                  pl.BlockSpec((1, n_lat, LANES), lambda bi, j: (bi, 0, vc0 + j)),
                  pl.BlockSpec((1, n_ctx, LANES), lambda bi, j: (bi, cb0, kc0 + j)),
                  pl.BlockSpec((1, n_ctx, LANES), lambda bi, j: (bi, cb0, vc0 + j)),
                  pl.BlockSpec((2, nt, GRID_W, LANES), lambda bi, j: (j, 0, 0, 0)),
                  pl.BlockSpec((2, nt, GRID_W, LANES), lambda bi, j: (j, 0, 0, 0))],
        out_specs=pl.BlockSpec((1, n_lat, LANES), lambda bi, j: (bi, 0, j)),
        out_shape=jax.ShapeDtypeStruct((b, n_lat, npairs * LANES), BF16),
        scratch_shapes=[pltpu.VMEM((2, 3, NA_ROWS * GRID_W, NA_SPAN * GRID_W), F32)],
        compiler_params=_cparams(("parallel", "arbitrary")),
        name="neighbourhood_attn",
    )(qkv, qkv, qkv, qkv, qkv, tl, tr)


def _na_bias_tiles(rpb):
    h = rpb.shape[0]
    cols = np.arange(GRID_W)
    cs = np.clip(cols - WIN_COLS // 2, 0, GRID_W - WIN_COLS)
    kc = cols[None, :]
    valid = (kc >= cs[:, None]) & (kc < cs[:, None] + WIN_COLS)
    ci = np.clip(kc - cols[:, None] + WIN_COLS - 1, 0, 2 * WIN_COLS - 2)
    onehot = jnp.asarray((ci[None] == np.arange(2 * WIN_COLS - 1)[:, None, None]).astype(np.float32))
    vals = jnp.einsum("hrj,jck->hrck", rpb.astype(F32), onehot, precision=lax.Precision.HIGHEST)
    t = jnp.where(jnp.asarray(valid)[None, None], vals * LOG2E, NEG)
    t = jnp.concatenate([jnp.full((h, 1, GRID_W, GRID_W), NEG, F32), t], axis=1)
    z = jnp.zeros_like(t)
    return jnp.concatenate([t, z], axis=-1), jnp.concatenate([z, t], axis=-1)


def _outproj_kernel(*refs, n_parts, has_ctx, lat_tiles):
    refs = list(refs)
    parts = [refs.pop(0) for _ in range(n_parts)]
    octx_ref = refs.pop(0) if has_ctx else None
    w_ref, x_ref, gm_ref, ng_ref, sh_ref, sc_ref, rw_ref, xo_ref, hp_ref, lg_ref, proj_ref = refs

    def lat():
        acc = None
        off = 0
        for p in parts:
            wdt = p.shape[-1]
            t = jnp.dot(p[0], w_ref[off:off + wdt, :], preferred_element_type=F32)
            acc = t if acc is None else acc + t
            off += wdt
        proj_ref[...] = acc

    if has_ctx:
        is_ctx = pl.program_id(1) >= lat_tiles
        pl.when(jnp.logical_not(is_ctx))(lat)

        @pl.when(is_ctx)
        def _():
            proj_ref[...] = jnp.dot(octx_ref[0], w_ref[...], preferred_element_type=F32)
    else:
        lat()

    x = x_ref[0] + gm_ref[0] * proj_ref[...]
    xo_ref[0] = x
    ms = jnp.mean(x * x, axis=-1, keepdims=True)
    h = x * lax.rsqrt(ms + EPS) * ng_ref[...]
    h = h * (1.0 + sc_ref[0]) + sh_ref[0]
    _store_rowchunks(hp_ref, (0,), _pack_bf16_pairs(h))
    lg_ref[...] = lax.dot_general(rw_ref[...], h, (((1,), (1,)), ((), ())), preferred_element_type=F32,
                                  precision=lax.Precision.HIGHEST)


def _outproj(parts, octx, w, xa, mods, gain, rw_t, *, n_lat, n_rows):
    b, _, d = xa.shape
    lat_tiles = n_lat // TM
    nt = n_rows // TM
    has_ctx = octx is not None

    def mod_map(chunk):
        return lambda bi, i: (jnp.where(i >= lat_tiles, b, bi) * 6 + chunk, 0, 0)

    in_specs = [pl.BlockSpec((1, TM, p.shape[-1]), lambda bi, i: (bi, jnp.minimum(i, lat_tiles - 1), 0))
                for p in parts]
    args = list(parts)
    if has_ctx:
        in_specs.append(pl.BlockSpec((1, TM, d), lambda bi, i: (bi, 0, 0)))
        args.append(octx)
    in_specs += [pl.BlockSpec((d, d), lambda bi, i: (0, 0)),
                 pl.BlockSpec((1, TM, d), lambda bi, i: (bi, i, 0)),
                 pl.BlockSpec((1, 1, d), mod_map(2)),
                 pl.BlockSpec((1, d), lambda bi, i: (0, 0)),
                 pl.BlockSpec((1, 1, d), mod_map(3)),
                 pl.BlockSpec((1, 1, d), mod_map(4)),
                 pl.BlockSpec((N_EXPERTS, d), lambda bi, i: (0, 0))]
    args += [w, xa, mods, gain, mods, mods, rw_t]
    return pl.pallas_call(
        functools.partial(_outproj_kernel, n_parts=len(parts), has_ctx=has_ctx, lat_tiles=lat_tiles),
        grid=(b, nt),
        in_specs=in_specs,
        out_specs=[pl.BlockSpec((1, TM, d), lambda bi, i: (bi, i, 0)),
                   pl.BlockSpec((1, TM * ROW_CH, LANES), lambda bi, i: (bi, i, 0)),
                   pl.BlockSpec((N_EXPERTS, TM), lambda bi, i: (0, bi * nt + i))],
        out_shape=[jax.ShapeDtypeStruct((b, n_rows, d), F32),
                   jax.ShapeDtypeStruct((b, n_rows * ROW_CH, LANES), U32),
                   jax.ShapeDtypeStruct((N_EXPERTS, b * n_rows), F32)],
        scratch_shapes=[pltpu.VMEM((TM, d), F32)],
        compiler_params=_cparams(("parallel", "arbitrary")),
        name="outproj_ffnnorm",
    )(*args)


def _router_kernel(lg_ref, rb_ref, tri_ref, idx_ref, gate_ref, pos_ref, cnt_ref, run_ref):
    @pl.when(pl.program_id(0) == 0)
    def _():
        run_ref[...] = jnp.zeros(run_ref.shape, F32)

    scores = jax.nn.sigmoid(lg_ref[...])
    work = scores + rb_ref[...]
    eidx = lax.broadcasted_iota(jnp.int32, work.shape, 0)
    hits, idx_rows, sel_rows = [], [], []
    for _ in range(TOP_K):
        mx = jnp.max(work, axis=0, keepdims=True)
        first = jnp.min(jnp.where(work == mx, eidx, N_EXPERTS), axis=0, keepdims=True)
        hit = eidx == first
        hits.append(hit)
        idx_rows.append(first)
        sel_rows.append(jnp.sum(jnp.where(hit, scores, 0.0), axis=0, keepdims=True))
        work = jnp.where(hit, NEG, work)
    mask = jnp.zeros(work.shape, F32)
    for hit in hits:
        mask = mask + hit.astype(F32)
    denom = sel_rows[0]
    for r in sel_rows[1:]:
        denom = denom + r
    csum = jnp.dot(mask.astype(BF16), tri_ref[...], preferred_element_type=F32)
    tm = mask.shape[1]
    posall = run_ref[:, 0:1] + csum - mask
    pos_rows = [jnp.sum(jnp.where(hit, posall, 0.0), axis=0, keepdims=True) for hit in hits]
    run_ref[...] = run_ref[...] + csum[:, tm - 1:tm]
    idx_ref[...] = jnp.concatenate(idx_rows, axis=0)
    gate_ref[...] = jnp.concatenate(sel_rows, axis=0) / denom * ROUTED_SCALE
    pos_ref[...] = jnp.concatenate(pos_rows, axis=0).astype(jnp.int32)
    cnt_ref[...] = run_ref[...]


def _router(logits_t, rb):
    e, t = logits_t.shape
    tri = jnp.asarray(np.triu(np.ones((TM, TM), np.float32)), BF16)
    return pl.pallas_call(
        _router_kernel,
        grid=(t // TM,),
        in_specs=[pl.BlockSpec((e, TM), lambda i: (0, i)),
                  pl.BlockSpec((e, 1), lambda i: (0, 0)),
                  pl.BlockSpec((TM, TM), lambda i: (0, 0))],
        out_specs=[pl.BlockSpec((TOP_K, TM), lambda i: (0, i)),
                   pl.BlockSpec((TOP_K, TM), lambda i: (0, i)),
                   pl.BlockSpec((TOP_K, TM), lambda i: (0, i)),
                   pl.BlockSpec((e, LANES), lambda i: (0, 0))],
        out_shape=[jax.ShapeDtypeStruct((TOP_K, t), jnp.int32),
                   jax.ShapeDtypeStruct((TOP_K, t), F32),
                   jax.ShapeDtypeStruct((TOP_K, t), jnp.int32),
                   jax.ShapeDtypeStruct((e, LANES), F32)],
        scratch_shapes=[pltpu.VMEM((e, LANES), F32)],
        compiler_params=_cparams(("arbitrary",)),
        name="router_topk",
    )(logits_t, rb, tri)


def _dispatch_kernel(fill_ref, dest_ref, h_ref, xs_hbm, zbuf, fsem, sem):
    i = pl.program_id(0)

    def fill_copy(e):
        start = pl.multiple_of(fill_ref[0, e] * ROW_CH, MOE_BLOCK * ROW_CH)
        return pltpu.make_async_copy(zbuf, xs_hbm.at[pl.ds(start, MOE_BLOCK * ROW_CH), :], fsem)

    @pl.when(i == 0)
    def _():
        zbuf[...] = jnp.zeros(zbuf.shape, zbuf.dtype)
        for e in range(fill_ref.shape[1]):
            pl.when(fill_ref[1, e] != 0)(lambda e=e: fill_copy(e).start())
        for e in range(fill_ref.shape[1]):
            pl.when(fill_ref[1, e] != 0)(lambda e=e: fill_copy(e).wait())

    for tl0 in range(0, DISPATCH_TT, DMA_BATCH):
        ds = [[dest_ref[0, 0, k * DISPATCH_TT + tl0 + u] for k in range(TOP_K)] for u in range(DMA_BATCH)]
        for u in range(DMA_BATCH):
            src = h_ref.at[pl.ds(ROW_CH * (tl0 + u), ROW_CH), :]
            for k in range(TOP_K):
                d = pl.multiple_of(ds[u][k] * ROW_CH, ROW_CH)
                pltpu.make_async_copy(src, xs_hbm.at[pl.ds(d, ROW_CH), :], sem).start(priority=k % 2)
    for k in range(TOP_K):
        pltpu.make_async_copy(h_ref, xs_hbm.at[pl.ds(0, DISPATCH_TT * ROW_CH), :], sem).wait()


def _dispatch(h2p, dest_tiles, fill, n_pad):
    grid_spec = pltpu.PrefetchScalarGridSpec(
        num_scalar_prefetch=1,
        grid=(h2p.shape[0] // (DISPATCH_TT * ROW_CH),),
        in_specs=[pl.BlockSpec((1, 1, TOP_K * DISPATCH_TT), lambda i, f: (i, 0, 0), memory_space=pltpu.SMEM),
                  pl.BlockSpec((DISPATCH_TT * ROW_CH, LANES), lambda i, f: (i, 0))],
        out_specs=pl.BlockSpec(memory_space=pl.ANY),
        scratch_shapes=[pltpu.VMEM((MOE_BLOCK * ROW_CH, LANES), U32),
                        pltpu.SemaphoreType.DMA,
                        pltpu.SemaphoreType.DMA])
    return pl.pallas_call(
        _dispatch_kernel,
        grid_spec=grid_spec,
        out_shape=jax.ShapeDtypeStruct((n_pad * ROW_CH, LANES), U32),
        compiler_params=_cparams(("arbitrary",)),
        name="moe_dispatch",
    )(fill, dest_tiles, h2p)


def _expert_kernel(be_ref, nu_ref, xs_ref, wg_ref, wu_ref, wd_ref, ys_ref, wgb, wub, wdb):
    i = pl.program_id(0)

    @pl.when(i >= nu_ref[0])
    def _():
        ys_ref[...] = jnp.zeros(ys_ref.shape, ys_ref.dtype)

    @pl.when(i < nu_ref[0])
    def _():
        e = be_ref[i]
        prev = be_ref[jnp.maximum(i - 1, 0)]

        @pl.when(jnp.logical_or(i == 0, e != prev))
        def _():
            wgb[...] = wg_ref[0, 0].astype(BF16)
            wub[...] = wu_ref[0, 0].astype(BF16)
            wdb[...] = wd_ref[0, 0].astype(BF16)

        xb = _load_rowchunks(xs_ref, (), MOE_BLOCK, BF16)
        g = jnp.dot(xb, wgb[...], preferred_element_type=F32)
        u = jnp.dot(xb, wub[...], preferred_element_type=F32)
        hid = (g * jax.nn.sigmoid(g)) * u
        y = jnp.dot(hid.astype(BF16), wdb[...], preferred_element_type=F32)
        _store_rowchunks(ys_ref, (), _pack_bf16_pairs(y))


def _experts(xs, blk_expert, n_used, wg, wu, wd, layer):
    rows = xs.shape[0]
    _, _, d, f = wg.shape
    nblk = rows // (MOE_BLOCK * ROW_CH)

    def row_map(i, be, nu):
        return (jnp.minimum(i, nu[0] - 1), 0)

    grid_spec = pltpu.PrefetchScalarGridSpec(
        num_scalar_prefetch=2,
        grid=(nblk,),
        in_specs=[pl.BlockSpec((MOE_BLOCK * ROW_CH, LANES), row_map),
                  pl.BlockSpec((1, 1, d, f), lambda i, be, nu: (layer, be[i], 0, 0)),
                  pl.BlockSpec((1, 1, d, f), lambda i, be, nu: (layer, be[i], 0, 0)),
                  pl.BlockSpec((1, 1, f, d), lambda i, be, nu: (layer, be[i], 0, 0))],
        out_specs=pl.BlockSpec((MOE_BLOCK * ROW_CH, LANES), lambda i, be, nu: (i, 0)),
        scratch_shapes=[pltpu.VMEM((d, f), BF16), pltpu.VMEM((d, f), BF16), pltpu.VMEM((f, d), BF16)])
    return pl.pallas_call(
        _expert_kernel,
        grid_spec=grid_spec,
        out_shape=jax.ShapeDtypeStruct((rows, LANES), U32),
        compiler_params=_cparams(("arbitrary",)),
        name="moe_experts",
    )(blk_expert, n_used, xs, wg, wu, wd)


def _combine_kernel(dest_ref, dnext_ref, ys_hbm, gate_ref, hp_ref, x_ref, gf_ref, sg_ref, su_ref, sd_ref, fn_ref,
                    o_ref, buf, ysum, grep, sem, *, final):
    i = pl.program_id(0)
    slot = i % 2

    def issue(dref, s):
        for tl0 in range(0, MOE_TT, DMA_BATCH):
            ds = [[dref[0, 0, k * MOE_TT + tl0 + u] for k in range(TOP_K)] for u in range(DMA_BATCH)]
            for u in range(DMA_BATCH):
                for k in range(TOP_K):
                    d = pl.multiple_of(ds[u][k] * ROW_CH, ROW_CH)
                    pltpu.make_async_copy(ys_hbm.at[pl.ds(d, ROW_CH), :],
                                          buf.at[s, k, pl.ds(ROW_CH * (tl0 + u), ROW_CH), :],
                                          sem.at[s]).start(priority=k % 2)

    pl.when(i == 0)(lambda: issue(dest_ref, 0))
    pl.when(i + 1 < pl.num_programs(0))(lambda: issue(dnext_ref, 1 - slot))

    xb = _load_rowchunks(hp_ref, (), MOE_TT, BF16)
    g = jnp.dot(xb, sg_ref[...], preferred_element_type=F32)
    u = jnp.dot(xb, su_ref[...], preferred_element_type=F32)
    y = jnp.dot(((g * jax.nn.sigmoid(g)) * u).astype(BF16), sd_ref[...], preferred_element_type=F32)

    for k in range(TOP_K):
        pltpu.make_async_copy(ys_hbm.at[pl.ds(0, MOE_TT * ROW_CH), :], buf.at[slot, k], sem.at[slot]).wait()
    for c in range(ROW_CH):
        grep[pl.ds(c, MOE_TT, stride=ROW_CH), :] = gate_ref[...]
    gates = grep[...]
    lo_sum = None
    hi_sum = None
    for k in range(TOP_K):
        lo, hi = _unpack_bf16_pairs(buf[slot, k])
        gk = gates[:, k:k + 1]
        lo_sum = gk * lo if lo_sum is None else lo_sum + gk * lo
        hi_sum = gk * hi if hi_sum is None else hi_sum + gk * hi
    ysum[0] = lo_sum
    ysum[1] = hi_sum
    routed = jnp.concatenate([ysum[half, pl.ds(c, MOE_TT, stride=ROW_CH), :]
                              for half in range(2) for c in range(ROW_CH)], axis=1)
    x = x_ref[...] + gf_ref[0] * (y + routed)
    if final:
        ms = jnp.mean(x * x, axis=-1, keepdims=True)
        x = x * lax.rsqrt(ms + EPS) * fn_ref[...]
    o_ref[...] = x


def _combine(ys, dest_tiles, gates_t, h2p, x_new, mods, sg, su, sd, fn, *, n_batch, n_rows, n_lat, final):
    t, d = x_new.shape
    f = sg.shape[1]
    tiles_b = n_rows // MOE_TT
    lat_tiles = n_lat // MOE_TT
    n_tiles = t // MOE_TT

    def mod_map(i):
        return (jnp.where(i % tiles_b >= lat_tiles, n_batch, i // tiles_b) * 6 + 5, 0, 0)

    grid_spec = pltpu.PrefetchScalarGridSpec(
        num_scalar_prefetch=0,
        grid=(t // MOE_TT,),
        in_specs=[pl.BlockSpec((1, 1, TOP_K * MOE_TT), lambda i: (i, 0, 0), memory_space=pltpu.SMEM),
                  pl.BlockSpec((1, 1, TOP_K * MOE_TT), lambda i: (jnp.minimum(i + 1, n_tiles - 1), 0, 0),
                               memory_space=pltpu.SMEM),
                  pl.BlockSpec(memory_space=pl.ANY),
                  pl.BlockSpec((MOE_TT, TOP_K), lambda i: (i, 0)),
                  pl.BlockSpec((MOE_TT * ROW_CH, LANES), lambda i: (i, 0)),
                  pl.BlockSpec((MOE_TT, d), lambda i: (i, 0)),
                  pl.BlockSpec((1, 1, d), mod_map),
                  pl.BlockSpec((d, f), lambda i: (0, 0)),
                  pl.BlockSpec((d, f), lambda i: (0, 0)),
                  pl.BlockSpec((f, d), lambda i: (0, 0)),
                  pl.BlockSpec((1, d), lambda i: (0, 0))],
        out_specs=pl.BlockSpec((MOE_TT, d), lambda i: (i, 0)),
        scratch_shapes=[pltpu.VMEM((2, TOP_K, MOE_TT * ROW_CH, LANES), U32),
                        pltpu.VMEM((2, MOE_TT * ROW_CH, LANES), F32),
                        pltpu.VMEM((MOE_TT * ROW_CH, TOP_K), F32),
                        pltpu.SemaphoreType.DMA((2,))])
    return pl.pallas_call(
        functools.partial(_combine_kernel, final=final),
        grid_spec=grid_spec,
        out_shape=jax.ShapeDtypeStruct((t, d), F32),
        compiler_params=_cparams(("arbitrary",)),
        name="moe_combine",
    )(dest_tiles, dest_tiles, ys, gates_t, h2p, x_new, mods, sg, su, sd, fn)


def _rope_tables(n_lat, n_ctx):
    t = np.arange(n_lat)
    row = (t // GRID_W).astype(np.float32)
    col = (t % GRID_W).astype(np.float32)
    npairs = HEAD_DIM // 4
    inv_freq = jnp.asarray(ROPE_THETA, F32) ** (-jnp.arange(npairs, dtype=F32) / npairs)
    ang = jnp.concatenate([jnp.asarray(row)[:, None] * inv_freq, jnp.asarray(col)[:, None] * inv_freq], axis=-1)
    cos = jnp.repeat(jnp.cos(ang), 2, axis=-1)
    sin = jnp.repeat(jnp.sin(ang), 2, axis=-1)
    sign = jnp.asarray(np.tile(np.array([-1.0, 1.0], np.float32), HEAD_DIM // 2))
    cosf = jnp.tile(cos, (1, LANES // HEAD_DIM))
    sins = jnp.tile(sin * sign, (1, LANES // HEAD_DIM))
    cosf = jnp.concatenate([cosf, jnp.ones((n_ctx, LANES), F32)], axis=0)
    sins = jnp.concatenate([sins, jnp.zeros((n_ctx, LANES), F32)], axis=0)
    return cosf, sins


def _moe_block(logits_t, h2p, x_new, mods_l, rb, wg, wu, wd, sg, su, sd, fn, *, layer, n_batch, n_rows, n_lat,
               final):
    t = x_new.shape[0]
    idx, gates, pos, cnt = _router(logits_t, rb.reshape(N_EXPERTS, 1))
    counts = cnt[:, 0].astype(jnp.int32)
    padded = (counts + MOE_BLOCK - 1) // MOE_BLOCK * MOE_BLOCK
    pad_end = jnp.cumsum(padded)
    pad_start = pad_end - padded
    n_blocks = (t * TOP_K + N_EXPERTS * (MOE_BLOCK - 1) + MOE_BLOCK - 1) // MOE_BLOCK
    n_pad = n_blocks * MOE_BLOCK
    experts = jnp.arange(N_EXPERTS, dtype=jnp.int32)
    dest = jnp.sum(jnp.where(idx[:, :, None] == experts, pad_start, 0), axis=-1) + pos

    def tiles(tt):
        return dest.reshape(TOP_K, t // tt, tt).transpose(1, 0, 2).reshape(t // tt, 1, TOP_K * tt)

    blk_start = jnp.arange(n_blocks, dtype=jnp.int32) * MOE_BLOCK
    blk_expert = jnp.minimum(jnp.sum((pad_end[None, :] <= blk_start[:, None]).astype(jnp.int32), axis=1),
                             N_EXPERTS - 1)
    n_used = (pad_end[-1:] // MOE_BLOCK).astype(jnp.int32)
    tail = n_used + jnp.arange(N_EXPERTS + 1, dtype=jnp.int32)
    fill = jnp.stack([jnp.concatenate([jnp.maximum(pad_end - MOE_BLOCK, 0), jnp.minimum(tail, n_blocks - 1) * MOE_BLOCK]),
                      jnp.concatenate([counts > 0, tail < n_blocks]).astype(jnp.int32)]).astype(jnp.int32)
    xs = _dispatch(h2p, tiles(DISPATCH_TT), fill, n_pad)
    ys = _experts(xs, blk_expert, n_used, wg, wu, wd, layer)
    return _combine(ys, tiles(MOE_TT), gates.T, h2p, x_new, mods_l, sg.astype(BF16), su.astype(BF16),
                    sd.astype(BF16), fn, n_batch=n_batch, n_rows=n_rows, n_lat=n_lat, final=final)


def kernel(x, c, ctx, c_ctx, ada_w, ada_b, norm_mix, norm_ffn, ab_w_in, ab_w_out, na_rpb, gqa_q_gain,
           gqa_k_gain, diff_w_in, diff_w_out, diff_lq1, diff_lk1, diff_lq2, diff_lk2, diff_sub_gain,
           router_w, router_bias, expert_w_gate, expert_w_up, expert_w_down, shared_w_gate, shared_w_up,
           shared_w_down, final_norm):
    b, n, d = x.shape
    n_ctx = ctx.shape[1]
    depth = ada_w.shape[0]
    assert d == 2 * ROW_CH * LANES
    assert depth == 2 and n_ctx == TM and n % (NA_ROWS * GRID_W) == 0 and d % LANES == 0
    assert (n + n_ctx) % FLASH_TK == 0 and n % FLASH_TQ == 0 and n % DISPATCH_TT == 0 and n_ctx % DISPATCH_TT == 0

    cvec = jnp.concatenate([c, c_ctx[None], jnp.zeros((8 - b - 1, d), F32)], axis=0)
    mods = _mods(cvec, ada_w, ada_b)[:, :b + 1].reshape(depth, (b + 1) * 6, 1, d)
    cosf, sins = _rope_tables(n, n_ctx)
    gm = jnp.asarray(np.kron(np.eye(LANES // HEAD_DIM), np.full((HEAD_DIM, HEAD_DIM), 1.0 / HEAD_DIM)), BF16)
    ones = jnp.ones((1, LANES), F32)
    fn = final_norm.reshape(1, d)

    xa = jnp.concatenate([x, ctx], axis=1)

    w = ab_w_in[0]
    kb = [w[:, 2048 + HEAD_DIM * g: 2048 + HEAD_DIM * (g + 1)] for g in range(2)]
    vb = [w[:, 2176 + HEAD_DIM * g: 2176 + HEAD_DIM * (g + 1)] for g in range(2)]
    w0 = jnp.concatenate([w[:, :2048], kb[0], kb[0], kb[1], kb[1], vb[0], vb[0], vb[1], vb[1]],
                         axis=1).astype(BF16)
    plan0 = ([(None, False, True)] * 4 + [(None, False, False)] * 8 + [("q", True, True)] * 4
             + [("k", True, False)] * 2 + [(None, False, False)] * 2)
    qg = jnp.tile(gqa_q_gain[0].reshape(1, HEAD_DIM), (1, LANES // HEAD_DIM))
    kg = jnp.tile(gqa_k_gain[0].reshape(1, HEAD_DIM), (1, LANES // HEAD_DIM))
    qkv = _proj(xa, norm_mix[0].reshape(1, d), mods[0], 0, 1, w0, cosf, sins, gm, qg, kg, plan0, n)

    tl, tr = _na_bias_tiles(na_rpb[0])
    o_na = _na(qkv, tl, tr, n_lat=n, n_ctx=n_ctx, npairs=4, qc0=0, kc0=4, vc0=8)
    tbl_g = jnp.asarray([[12, 13, 14, 15], [16, 16, 17, 17], [18, 18, 19, 19]], jnp.int32)
    o_gqa = _flash(qkv, tbl_g, n_q=n, q_row0=0, n_keys=n + n_ctx, key_row0=0, tq=FLASH_TQ, tk=FLASH_TK)
    tbl_c = jnp.asarray([[0, 1, 2, 3, 12, 13, 14, 15], [4, 5, 6, 7, 16, 16, 17, 17],
                         [8, 9, 10, 11, 18, 18, 19, 19]], jnp.int32)
    o_ctx = _flash(qkv, tbl_c, n_q=n_ctx, q_row0=n, n_keys=n_ctx, key_row0=n, tq=n_ctx, tk=n_ctx)

    s_all = n + n_ctx
    x_new, h2p, lg = _outproj([o_na, o_gqa], o_ctx, ab_w_out[0].astype(BF16), xa, mods[0],
                              norm_ffn[0].reshape(1, d), router_w[0].T, n_lat=n, n_rows=s_all)
    xa = _moe_block(lg, h2p.reshape(-1, LANES), x_new.reshape(b * s_all, d), mods[0], router_bias[0],
                    expert_w_gate, expert_w_up, expert_w_down, shared_w_gate[0], shared_w_up[0],
                    shared_w_down[0], fn, layer=0, n_batch=b, n_rows=s_all, n_lat=n, final=False).reshape(b, s_all, d)

    lam_init = 0.8 - 0.6 * math.exp(-0.3 * 1)
    plan1 = [(None, True, True)] * 8 + [(None, True, False)] * 8 + [(None, False, False)] * 8
    qkv = _proj(xa, norm_mix[1].reshape(1, d), mods[1], 0, 1, diff_w_in[0].astype(BF16), cosf, sins, gm,
                ones, ones, plan1, n)
    tbl_d = jnp.asarray([list(range(0, 8)), list(range(8, 16)), list(range(16, 24))], jnp.int32)
    dp = [diff_lq1[0].reshape(1, HEAD_DIM), diff_lk1[0].reshape(1, HEAD_DIM),
          diff_lq2[0].reshape(1, HEAD_DIM), diff_lk2[0].reshape(1, HEAD_DIM),
          diff_sub_gain[0].reshape(1, LANES)]
    o_diff = _flash(qkv, tbl_d, n_q=n, q_row0=0, n_keys=n + n_ctx, key_row0=0, tq=FLASH_TQ, tk=FLASH_TK,
                    mode="diff", diff_params=dp, lam_init=lam_init)
    x_new, h2p, lg = _outproj([o_diff], None, diff_w_out[0].astype(BF16), xa, mods[1],
                              norm_ffn[1].reshape(1, d), router_w[1].T, n_lat=n, n_rows=n)
    return _moe_block(lg, h2p.reshape(-1, LANES), x_new.reshape(b * n, d), mods[1], router_bias[1],
                      expert_w_gate, expert_w_up, expert_w_down, shared_w_gate[1], shared_w_up[1],
                      shared_w_down[1], fn, layer=1, n_batch=b, n_rows=n, n_lat=n, final=True).reshape(b, n, d)
```

```python
import functools
import math

import jax
import jax.numpy as jnp
import numpy as np
from jax import lax
from jax.experimental import pallas as pl
from jax.experimental.pallas import tpu as pltpu

F32 = jnp.float32
BF16 = jnp.bfloat16
U32 = jnp.uint32

LANES = 128
HEAD_DIM = 64
GRID_W = 64
WIN_ROWS = 8
WIN_COLS = 16
ROPE_THETA = 10000.0
EPS = 1e-6
N_EXPERTS = 64
TOP_K = 8
ROUTED_SCALE = 2.5
NEG = -1e30
LOG2E = math.log2(math.e)
Q_SCALE = HEAD_DIM ** -0.5 * LOG2E
VMEM_LIMIT = 56 * 1024 * 1024

TM = 256
NA_ROWS = 8
NA_SPAN = 16
FLASH_TQ = 512
FLASH_TK = 2816
MOE_BLOCK = 512
MOE_TT = 128
DISPATCH_TT = 256
DMA_BATCH = 2
HI_MASK = 0xFFFF0000


def _cparams(sem):
    return pltpu.CompilerParams(dimension_semantics=sem, vmem_limit_bytes=VMEM_LIMIT)


def _pack_bf16_pairs(x):
    w = x.shape[1] // 2
    bits = lax.bitcast_convert_type(x.astype(BF16).astype(F32), U32)
    return (bits[:, w:] & jnp.uint32(HI_MASK)) | (bits[:, :w] >> 16)


def _unpack_bf16_pairs(p):
    lo = lax.bitcast_convert_type(p << 16, F32)
    hi = lax.bitcast_convert_type(p & jnp.uint32(HI_MASK), F32)
    return lo, hi


ROW_CH = 4


def _store_rowchunks(ref, idx, packed):
    m = packed.shape[0]
    for c in range(ROW_CH):
        ref[idx + (pl.ds(c, m, stride=ROW_CH), slice(None))] = packed[:, c * LANES:(c + 1) * LANES]


def _load_rowchunks(ref, idx, m, dtype):
    planes = [_unpack_bf16_pairs(ref[idx + (pl.ds(c, m, stride=ROW_CH), slice(None))]) for c in range(ROW_CH)]
    return jnp.concatenate([lo.astype(dtype) for lo, _ in planes] + [hi.astype(dtype) for _, hi in planes], axis=1)


def _mods_kernel(c_ref, w_ref, b_ref, o_ref):
    c = c_ref[...]
    s = c * jax.nn.sigmoid(c)
    o_ref[0] = jnp.dot(s.astype(BF16), w_ref[0].astype(BF16), preferred_element_type=F32) + b_ref[0]


def _mods(cvec, ada_w, ada_b):
    depth, d, d6 = ada_w.shape
    tn = 1536
    return pl.pallas_call(
        _mods_kernel,
        grid=(depth, d6 // tn),
        in_specs=[pl.BlockSpec((8, d), lambda l, j: (0, 0)),
                  pl.BlockSpec((1, d, tn), lambda l, j: (l, 0, j)),
                  pl.BlockSpec((1, 1, tn), lambda l, j: (l, 0, j))],
        out_specs=pl.BlockSpec((1, 8, tn), lambda l, j: (l, 0, j)),
        out_shape=jax.ShapeDtypeStruct((depth, 8, d6), F32),
        compiler_params=_cparams(("arbitrary", "arbitrary")),
        name="adaln_mods",
    )(cvec, ada_w, ada_b.reshape(depth, 1, d6))


def _proj_kernel(x_ref, g_ref, sh_ref, sc_ref, w_ref, cos_ref, sin_ref, gm_ref, qg_ref, kg_ref,
                 o_ref, *, plan):
    x = x_ref[0]
    ms = jnp.mean(x * x, axis=-1, keepdims=True)
    h = x * lax.rsqrt(ms + EPS) * g_ref[...]
    h = h * (1.0 + sc_ref[0]) + sh_ref[0]
    y = jnp.dot(h.astype(BF16), w_ref[...], preferred_element_type=F32)
    cosf = cos_ref[...]
    sins = sin_ref[...]
    even = (lax.broadcasted_iota(jnp.int32, (1, LANES), 1) % 2) == 0
    for c, (norm, rope, scale) in enumerate(plan):
        yc = y[:, c * LANES:(c + 1) * LANES]
        if norm:
            ms2 = jnp.dot((yc * yc).astype(BF16), gm_ref[...], preferred_element_type=F32)
            gain = qg_ref[...] if norm == "q" else kg_ref[...]
            yc = yc * lax.rsqrt(ms2 + EPS) * gain
        if rope:
            sw = jnp.where(even, pltpu.roll(yc, LANES - 1, 1), pltpu.roll(yc, 1, 1))
            yc = yc * cosf + sw * sins
        if scale:
            yc = yc * Q_SCALE
        o_ref[0, :, c * LANES:(c + 1) * LANES] = yc.astype(BF16)


def _proj(xa, gain, mods, sh_idx, sc_idx, w, cosf, sins, gm, qg, kg, plan, n_lat):
    b, s, d = xa.shape
    wcols = w.shape[1]
    nt = s // TM
    lat_tiles = n_lat // TM

    def mod_map(chunk):
        return lambda bi, i: (jnp.where(i >= lat_tiles, b, bi) * 6 + chunk, 0, 0)

    return pl.pallas_call(
        functools.partial(_proj_kernel, plan=plan),
        grid=(b, nt),
        in_specs=[pl.BlockSpec((1, TM, d), lambda bi, i: (bi, i, 0)),
                  pl.BlockSpec((1, d), lambda bi, i: (0, 0)),
                  pl.BlockSpec((1, 1, d), mod_map(sh_idx)),
                  pl.BlockSpec((1, 1, d), mod_map(sc_idx)),
                  pl.BlockSpec((d, wcols), lambda bi, i: (0, 0)),
                  pl.BlockSpec((TM, LANES), lambda bi, i: (i, 0)),
                  pl.BlockSpec((TM, LANES), lambda bi, i: (i, 0)),
                  pl.BlockSpec((LANES, LANES), lambda bi, i: (0, 0)),
                  pl.BlockSpec((1, LANES), lambda bi, i: (0, 0)),
                  pl.BlockSpec((1, LANES), lambda bi, i: (0, 0))],
        out_specs=pl.BlockSpec((1, TM, wcols), lambda bi, i: (bi, i, 0)),
        out_shape=jax.ShapeDtypeStruct((b, s, wcols), BF16),
        compiler_params=_cparams(("parallel", "arbitrary")),
        name="norm_inproj",
    )(xa, gain, mods, mods, w, cosf, sins, gm, qg, kg)


def _flash_kernel(tbl_ref, *refs, tq, tk, n_keys, n_tiles, mode, lam_init):
    del tbl_ref
    refs = list(refs)
    q_ref, k_ref, v_ref = refs[:3]
    refs = refs[3:]
    if mode == "diff":
        lq1, lk1, lq2, lk2, sg_ref = refs[:5]
        refs = refs[5:]
    o_ref, qs_ref, s_ref, m_ref, acc_ref = refs

    lane = lax.broadcasted_iota(jnp.int32, (1, LANES), 1)
    lo = lane < HEAD_DIM
    for t in range(n_tiles):
        q = q_ref[0, t * tq:(t + 1) * tq, :]
        zero = jnp.zeros_like(q)
        qs_ref[t, 0:tq, :] = jnp.where(lo, q, zero)
        qs_ref[t, tq:2 * tq, :] = jnp.where(lo, zero, q)
    nb = tk // LANES
    nchunks = n_keys // tk
    ones = jnp.ones((tk, LANES), BF16)

    def qk(slot, t, j):
        off = pl.multiple_of(j * tk, tk)
        s_ref[slot] = lax.dot_general(qs_ref[t], k_ref[0, pl.ds(off, tk), :], (((1,), (1,)), ((), ())),
                                      preferred_element_type=F32)

    def softmax_pv(slot, j):
        off = pl.multiple_of(j * tk, tk)
        s = s_ref[slot]
        m_prev = m_ref[...]
        m_next = jnp.maximum(m_prev, jnp.max(s, axis=1, keepdims=True))
        alpha = jnp.exp2(m_prev - m_next)
        p = jnp.exp2(s - jnp.concatenate([m_next] * nb, axis=1))
        v1 = jnp.concatenate([v_ref[0, pl.ds(off, tk), :], ones], axis=1)
        acc_ref[...] = (jnp.concatenate([alpha, alpha], axis=1) * acc_ref[...]
                        + jnp.dot(p.astype(BF16), v1, preferred_element_type=F32))
        m_ref[...] = m_next

    def finalize(t):
        acc = acc_ref[...]
        o = acc[:, 0:LANES] / acc[:, LANES:2 * LANES]
        if mode == "pair":
            out = jnp.where(lo, o[0:tq], o[tq:2 * tq])
        else:
            lam = (jnp.exp(jnp.sum(lq1[...] * lk1[...], axis=1, keepdims=True))
                   - jnp.exp(jnp.sum(lq2[...] * lk2[...], axis=1, keepdims=True)) + lam_init)
            dlt = o[0:tq] - lam * o[tq:2 * tq]
            ms = jnp.mean(dlt * dlt, axis=-1, keepdims=True)
            out = dlt * lax.rsqrt(ms + EPS) * sg_ref[...] * (1.0 - lam_init)
        o_ref[0, pl.ds(pl.multiple_of(t * tq, tq), tq), :] = out.astype(o_ref.dtype)

    def tile(t, par, t_next):
        m_ref[...] = jnp.full(m_ref.shape, NEG, F32)
        acc_ref[...] = jnp.zeros(acc_ref.shape, F32)

        def body(jj, carry):
            j = 2 * jj
            qk(1 - par, t, j + 1)
            softmax_pv(par, j)
            qk(par, t, j + 2)
            softmax_pv(1 - par, j + 1)
            return carry

        lax.fori_loop(0, (nchunks - 1) // 2, body, 0)
        if nchunks % 2 == 0:
            qk(1 - par, t, nchunks - 1)
            softmax_pv(par, nchunks - 2)
            if t_next is not None:
                qk(par, t_next, 0)
            softmax_pv(1 - par, nchunks - 1)
            nxt = par
        else:
            if t_next is not None:
                qk(1 - par, t_next, 0)
            softmax_pv(par, nchunks - 1)
            nxt = 1 - par
        finalize(t)
        return nxt

    qk(0, 0, 0)
    if n_tiles == 1:
        tile(0, 0, None)
    else:
        def outer(ii, carry):
            ta = 2 * ii
            par = tile(ta, 0, ta + 1)
            par = tile(ta + 1, par, jnp.minimum(ta + 2, n_tiles - 1))
            assert par == 0
            return carry

        lax.fori_loop(0, n_tiles // 2, outer, 0)


def _flash(qkv, tbl, *, n_q, q_row0, n_keys, key_row0, tq, tk, mode="pair", diff_params=None, lam_init=0.0):
    b = qkv.shape[0]
    ncols = tbl.shape[1]
    n_tiles = n_q // tq
    assert n_tiles == 1 or n_tiles % 2 == 0
    qb0 = q_row0 // n_q
    kb0 = key_row0 // n_keys
    in_specs = [pl.BlockSpec((1, n_q, LANES), lambda bi, c, t: (bi, qb0, t[0, c])),
                pl.BlockSpec((1, n_keys, LANES), lambda bi, c, t: (bi, kb0, t[1, c])),
                pl.BlockSpec((1, n_keys, LANES), lambda bi, c, t: (bi, kb0, t[2, c]))]
    args = [qkv, qkv, qkv]
    if mode == "diff":
        in_specs += [pl.BlockSpec((1, HEAD_DIM), lambda bi, c, t: (0, 0))] * 4
        in_specs += [pl.BlockSpec((1, LANES), lambda bi, c, t: (0, 0))]
        args += list(diff_params)
    grid_spec = pltpu.PrefetchScalarGridSpec(
        num_scalar_prefetch=1,
        grid=(b, ncols),
        in_specs=in_specs,
        out_specs=pl.BlockSpec((1, n_q, LANES), lambda bi, c, t: (bi, 0, c)),
        scratch_shapes=[pltpu.VMEM((n_tiles, 2 * tq, LANES), BF16),
                        pltpu.VMEM((2, 2 * tq, tk), F32),
                        pltpu.VMEM((2 * tq, LANES), F32),
                        pltpu.VMEM((2 * tq, 2 * LANES), F32)])
    return pl.pallas_call(
        functools.partial(_flash_kernel, tq=tq, tk=tk, n_keys=n_keys, n_tiles=n_tiles, mode=mode,
                          lam_init=lam_init),
        grid_spec=grid_spec,
        out_shape=jax.ShapeDtypeStruct((b, n_q, ncols * LANES), BF16),
        compiler_params=_cparams(("parallel", "parallel")),
        name="flash_%s_%d" % (mode, n_keys),
    )(tbl, *args)


def _na_case_geometry(case, rows):
    r0 = {0: 0, 1: NA_ROWS, 2: rows - NA_ROWS}[case]
    start = min(max(r0 - WIN_ROWS // 2, 0), rows - NA_SPAN)
    return r0, start


def _na_tile_index(case, dr, dk, rows):
    r0, start = _na_case_geometry(case, rows)
    r, kr = r0 + dr, start + dk
    rs = min(max(r - WIN_ROWS // 2, 0), rows - WIN_ROWS)
    if rs <= kr < rs + WIN_ROWS:
        return kr - r + WIN_ROWS
    return 0


def _na_kernel(q_ref, k_ref, v_ref, kc_ref, vc_ref, tl_ref, tr_ref, o_ref, bias_ref, *, rows):
    nblk = rows // NA_ROWS
    bq = NA_ROWS * GRID_W
    bk = NA_SPAN * GRID_W
    for hh in range(2):
        for case in range(3):
            for dr in range(NA_ROWS):
                for dkp in range(NA_SPAN // 2):
                    ia = _na_tile_index(case, dr, 2 * dkp, rows)
                    ib = _na_tile_index(case, dr, 2 * dkp + 1, rows)
                    bias_ref[hh, case, dr * GRID_W:(dr + 1) * GRID_W, dkp * LANES:(dkp + 1) * LANES] = (
                        tl_ref[hh, ia] + tr_ref[hh, ib])

    lane = lax.broadcasted_iota(jnp.int32, (1, LANES), 1)
    lo = lane < HEAD_DIM
    kctx = kc_ref[0]
    vctx = vc_ref[0]
    nt = (((1,), (1,)), ((), ()))

    def body(i, carry):
        r0 = i * NA_ROWS
        start = jnp.clip(r0 - WIN_ROWS // 2, 0, rows - NA_SPAN)
        case = jnp.where(i == 0, 0, jnp.where(i == nblk - 1, 2, 1))
        qoff = pl.multiple_of(i * bq, bq)
        koff = pl.multiple_of(start * GRID_W, GRID_W)
        qb = q_ref[0, pl.ds(qoff, bq), :]
        ks = k_ref[0, pl.ds(koff, bk), :]
        vs = v_ref[0, pl.ds(koff, bk), :]
        zero = jnp.zeros_like(qb)
        kcat = jnp.concatenate([ks, kctx], axis=0)
        vcat = jnp.concatenate([jnp.concatenate([vs, vctx], axis=0), jnp.ones((bk + kctx.shape[0], LANES), BF16)],
                               axis=1)
        outs = []
        for hh in range(2):
            qm = jnp.where(lo, qb, zero) if hh == 0 else jnp.where(lo, zero, qb)
            s = lax.dot_general(qm, kcat, nt, preferred_element_type=F32)
            s = jnp.concatenate([s[:, 0:bk] + bias_ref[hh, case], s[:, bk:]], axis=1)
            p = jnp.exp2(s - jnp.max(s, axis=1, keepdims=True))
            acc = jnp.dot(p.astype(BF16), vcat, preferred_element_type=F32)
            outs.append(acc[:, 0:LANES] / acc[:, LANES:2 * LANES])
        o_ref[0, pl.ds(qoff, bq), :] = jnp.where(lo, outs[0], outs[1]).astype(o_ref.dtype)
        return carry

    lax.fori_loop(0, nblk, body, 0)


def _na(qkv, tl, tr, *, n_lat, n_ctx, npairs, qc0, kc0, vc0):
    b = qkv.shape[0]
    rows = n_lat // GRID_W
    cb0 = n_lat // n_ctx
    nt = tl.shape[1]
    return pl.pallas_call(
        functools.partial(_na_kernel, rows=rows),
        grid=(b, npairs),
        in_specs=[pl.BlockSpec((1, n_lat, LANES), lambda bi, j: (bi, 0, qc0 + j)),
                  pl.BlockSpec((1, n_lat, LANES), lambda bi, j: (bi, 0, kc0 + j)),
                  pl.BlockSpec((1, n_lat, LANES), lambda bi, j: (bi, 0, vc0 + j)),
                  pl.BlockSpec((1, n_ctx, LANES), lambda bi, j: (bi, cb0, kc0 + j)),
                  pl.BlockSpec((1, n_ctx, LANES), lambda bi, j: (bi, cb0, vc0 + j)),
                  pl.BlockSpec((2, nt, GRID_W, LANES), lambda bi, j: (j, 0, 0, 0)),
                  pl.BlockSpec((2, nt, GRID_W, LANES), lambda bi, j: (j, 0, 0, 0))],
        out_specs=pl.BlockSpec((1, n_lat, LANES), lambda bi, j: (bi, 0, j)),
        out_shape=jax.ShapeDtypeStruct((b, n_lat, npairs * LANES), BF16),
        scratch_shapes=[pltpu.VMEM((2, 3, NA_ROWS * GRID_W, NA_SPAN * GRID_W), F32)],
        compiler_params=_cparams(("parallel", "arbitrary")),
        name="neighbourhood_attn",
    )(qkv, qkv, qkv, qkv, qkv, tl, tr)


def _na_bias_tiles(rpb):
    h = rpb.shape[0]
    cols = np.arange(GRID_W)
    cs = np.clip(cols - WIN_COLS // 2, 0, GRID_W - WIN_COLS)
    kc = cols[None, :]
    valid = (kc >= cs[:, None]) & (kc < cs[:, None] + WIN_COLS)
    ci = np.clip(kc - cols[:, None] + WIN_COLS - 1, 0, 2 * WIN_COLS - 2)
    onehot = jnp.asarray((ci[None] == np.arange(2 * WIN_COLS - 1)[:, None, None]).astype(np.float32))
    vals = jnp.einsum("hrj,jck->hrck", rpb.astype(F32), onehot, precision=lax.Precision.HIGHEST)
    t = jnp.where(jnp.asarray(valid)[None, None], vals * LOG2E, NEG)
    t = jnp.concatenate([jnp.full((h, 1, GRID_W, GRID_W), NEG, F32), t], axis=1)
    z = jnp.zeros_like(t)
    return jnp.concatenate([t, z], axis=-1), jnp.concatenate([z, t], axis=-1)


def _outproj_kernel(*refs, n_parts, has_ctx, lat_tiles):
    refs = list(refs)
    parts = [refs.pop(0) for _ in range(n_parts)]
    octx_ref = refs.pop(0) if has_ctx else None
    w_ref, x_ref, gm_ref, ng_ref, sh_ref, sc_ref, rw_ref, xo_ref, hp_ref, lg_ref, proj_ref = refs

    def lat():
        acc = None
        off = 0
        for p in parts:
            wdt = p.shape[-1]
            t = jnp.dot(p[0], w_ref[off:off + wdt, :], preferred_element_type=F32)
            acc = t if acc is None else acc + t
            off += wdt
        proj_ref[...] = acc

    if has_ctx:
        is_ctx = pl.program_id(1) >= lat_tiles
        pl.when(jnp.logical_not(is_ctx))(lat)

        @pl.when(is_ctx)
        def _():
            proj_ref[...] = jnp.dot(octx_ref[0], w_ref[...], preferred_element_type=F32)
    else:
        lat()

    x = x_ref[0] + gm_ref[0] * proj_ref[...]
    xo_ref[0] = x
    ms = jnp.mean(x * x, axis=-1, keepdims=True)
    h = x * lax.rsqrt(ms + EPS) * ng_ref[...]
    h = h * (1.0 + sc_ref[0]) + sh_ref[0]
    _store_rowchunks(hp_ref, (0,), _pack_bf16_pairs(h))
    lg_ref[...] = lax.dot_general(rw_ref[...], h, (((1,), (1,)), ((), ())), preferred_element_type=F32,
                                  precision=lax.Precision.HIGHEST)


def _outproj(parts, octx, w, xa, mods, gain, rw_t, *, n_lat, n_rows):
    b, _, d = xa.shape
    lat_tiles = n_lat // TM
    nt = n_rows // TM
    has_ctx = octx is not None

    def mod_map(chunk):
        return lambda bi, i: (jnp.where(i >= lat_tiles, b, bi) * 6 + chunk, 0, 0)

    in_specs = [pl.BlockSpec((1, TM, p.shape[-1]), lambda bi, i: (bi, jnp.minimum(i, lat_tiles - 1), 0))
                for p in parts]
    args = list(parts)
    if has_ctx:
        in_specs.append(pl.BlockSpec((1, TM, d), lambda bi, i: (bi, 0, 0)))
        args.append(octx)
    in_specs += [pl.BlockSpec((d, d), lambda bi, i: (0, 0)),
                 pl.BlockSpec((1, TM, d), lambda bi, i: (bi, i, 0)),
                 pl.BlockSpec((1, 1, d), mod_map(2)),
                 pl.BlockSpec((1, d), lambda bi, i: (0, 0)),
                 pl.BlockSpec((1, 1, d), mod_map(3)),
                 pl.BlockSpec((1, 1, d), mod_map(4)),
                 pl.BlockSpec((N_EXPERTS, d), lambda bi, i: (0, 0))]
    args += [w, xa, mods, gain, mods, mods, rw_t]
    return pl.pallas_call(
        functools.partial(_outproj_kernel, n_parts=len(parts), has_ctx=has_ctx, lat_tiles=lat_tiles),
        grid=(b, nt),
        in_specs=in_specs,
        out_specs=[pl.BlockSpec((1, TM, d), lambda bi, i: (bi, i, 0)),
                   pl.BlockSpec((1, TM * ROW_CH, LANES), lambda bi, i: (bi, i, 0)),
                   pl.BlockSpec((N_EXPERTS, TM), lambda bi, i: (0, bi * nt + i))],
        out_shape=[jax.ShapeDtypeStruct((b, n_rows, d), F32),
                   jax.ShapeDtypeStruct((b, n_rows * ROW_CH, LANES), U32),
                   jax.ShapeDtypeStruct((N_EXPERTS, b * n_rows), F32)],
        scratch_shapes=[pltpu.VMEM((TM, d), F32)],
        compiler_params=_cparams(("parallel", "arbitrary")),
        name="outproj_ffnnorm",
    )(*args)


def _router_kernel(lg_ref, rb_ref, tri_ref, idx_ref, gate_ref, pos_ref, cnt_ref, run_ref):
    @pl.when(pl.program_id(0) == 0)
    def _():
        run_ref[...] = jnp.zeros(run_ref.shape, F32)

    scores = jax.nn.sigmoid(lg_ref[...])
    work = scores + rb_ref[...]
    eidx = lax.broadcasted_iota(jnp.int32, work.shape, 0)
    hits, idx_rows, sel_rows = [], [], []
    for _ in range(TOP_K):
        mx = jnp.max(work, axis=0, keepdims=True)
        first = jnp.min(jnp.where(work == mx, eidx, N_EXPERTS), axis=0, keepdims=True)
        hit = eidx == first
        hits.append(hit)
        idx_rows.append(first)
        sel_rows.append(jnp.sum(jnp.where(hit, scores, 0.0), axis=0, keepdims=True))
        work = jnp.where(hit, NEG, work)
    mask = jnp.zeros(work.shape, F32)
    for hit in hits:
        mask = mask + hit.astype(F32)
    denom = sel_rows[0]
    for r in sel_rows[1:]:
        denom = denom + r
    csum = jnp.dot(mask.astype(BF16), tri_ref[...], preferred_element_type=F32)
    tm = mask.shape[1]
    posall = run_ref[:, 0:1] + csum - mask
    pos_rows = [jnp.sum(jnp.where(hit, posall, 0.0), axis=0, keepdims=True) for hit in hits]
    run_ref[...] = run_ref[...] + csum[:, tm - 1:tm]
    idx_ref[...] = jnp.concatenate(idx_rows, axis=0)
    gate_ref[...] = jnp.concatenate(sel_rows, axis=0) / denom * ROUTED_SCALE
    pos_ref[...] = jnp.concatenate(pos_rows, axis=0).astype(jnp.int32)
    cnt_ref[...] = run_ref[...]


def _router(logits_t, rb):
    e, t = logits_t.shape
    tri = jnp.asarray(np.triu(np.ones((TM, TM), np.float32)), BF16)
    return pl.pallas_call(
        _router_kernel,
        grid=(t // TM,),
        in_specs=[pl.BlockSpec((e, TM), lambda i: (0, i)),
                  pl.BlockSpec((e, 1), lambda i: (0, 0)),
                  pl.BlockSpec((TM, TM), lambda i: (0, 0))],
        out_specs=[pl.BlockSpec((TOP_K, TM), lambda i: (0, i)),
                   pl.BlockSpec((TOP_K, TM), lambda i: (0, i)),
                   pl.BlockSpec((TOP_K, TM), lambda i: (0, i)),
                   pl.BlockSpec((e, LANES), lambda i: (0, 0))],
        out_shape=[jax.ShapeDtypeStruct((TOP_K, t), jnp.int32),
                   jax.ShapeDtypeStruct((TOP_K, t), F32),
                   jax.ShapeDtypeStruct((TOP_K, t), jnp.int32),
                   jax.ShapeDtypeStruct((e, LANES), F32)],
        scratch_shapes=[pltpu.VMEM((e, LANES), F32)],
        compiler_params=_cparams(("arbitrary",)),
        name="router_topk",
    )(logits_t, rb, tri)


def _dispatch_kernel(fill_ref, dest_ref, h_ref, xs_hbm, zbuf, fsem, sem):
    i = pl.program_id(0)

    def fill_copy(e):
        start = pl.multiple_of(fill_ref[0, e] * ROW_CH, MOE_BLOCK * ROW_CH)
        return pltpu.make_async_copy(zbuf, xs_hbm.at[pl.ds(start, MOE_BLOCK * ROW_CH), :], fsem)

    @pl.when(i == 0)
    def _():
        zbuf[...] = jnp.zeros(zbuf.shape, zbuf.dtype)
        for e in range(fill_ref.shape[1]):
            pl.when(fill_ref[1, e] != 0)(lambda e=e: fill_copy(e).start())
        for e in range(fill_ref.shape[1]):
            pl.when(fill_ref[1, e] != 0)(lambda e=e: fill_copy(e).wait())

    for tl0 in range(0, DISPATCH_TT, DMA_BATCH):
        ds = [[dest_ref[0, 0, k * DISPATCH_TT + tl0 + u] for k in range(TOP_K)] for u in range(DMA_BATCH)]
        for u in range(DMA_BATCH):
            src = h_ref.at[pl.ds(ROW_CH * (tl0 + u), ROW_CH), :]
            for k in range(TOP_K):
                d = pl.multiple_of(ds[u][k] * ROW_CH, ROW_CH)
                pltpu.make_async_copy(src, xs_hbm.at[pl.ds(d, ROW_CH), :], sem).start(priority=k % 2)
    for k in range(TOP_K):
        pltpu.make_async_copy(h_ref, xs_hbm.at[pl.ds(0, DISPATCH_TT * ROW_CH), :], sem).wait()


def _dispatch(h2p, dest_tiles, fill, n_pad):
    grid_spec = pltpu.PrefetchScalarGridSpec(
        num_scalar_prefetch=1,
        grid=(h2p.shape[0] // (DISPATCH_TT * ROW_CH),),
        in_specs=[pl.BlockSpec((1, 1, TOP_K * DISPATCH_TT), lambda i, f: (i, 0, 0), memory_space=pltpu.SMEM),
                  pl.BlockSpec((DISPATCH_TT * ROW_CH, LANES), lambda i, f: (i, 0))],
        out_specs=pl.BlockSpec(memory_space=pl.ANY),
        scratch_shapes=[pltpu.VMEM((MOE_BLOCK * ROW_CH, LANES), U32),
                        pltpu.SemaphoreType.DMA,
                        pltpu.SemaphoreType.DMA])
    return pl.pallas_call(
        _dispatch_kernel,
        grid_spec=grid_spec,
        out_shape=jax.ShapeDtypeStruct((n_pad * ROW_CH, LANES), U32),
        compiler_params=_cparams(("arbitrary",)),
        name="moe_dispatch",
    )(fill, dest_tiles, h2p)


def _expert_kernel(be_ref, nu_ref, xs_ref, wg_ref, wu_ref, wd_ref, ys_ref, wgb, wub, wdb):
    i = pl.program_id(0)

    @pl.when(i >= nu_ref[0])
    def _():
        ys_ref[...] = jnp.zeros(ys_ref.shape, ys_ref.dtype)

    @pl.when(i < nu_ref[0])
    def _():
        e = be_ref[i]
        prev = be_ref[jnp.maximum(i - 1, 0)]

        @pl.when(jnp.logical_or(i == 0, e != prev))
        def _():
            wgb[...] = wg_ref[0, 0].astype(BF16)
            wub[...] = wu_ref[0, 0].astype(BF16)
            wdb[...] = wd_ref[0, 0].astype(BF16)

        xb = _load_rowchunks(xs_ref, (), MOE_BLOCK, BF16)
        g = jnp.dot(xb, wgb[...], preferred_element_type=F32)
        u = jnp.dot(xb, wub[...], preferred_element_type=F32)
        hid = (g * jax.nn.sigmoid(g)) * u
        y = jnp.dot(hid.astype(BF16), wdb[...], preferred_element_type=F32)
        _store_rowchunks(ys_ref, (), _pack_bf16_pairs(y))


def _experts(xs, blk_expert, n_used, wg, wu, wd, layer):
    rows = xs.shape[0]
    _, _, d, f = wg.shape
    nblk = rows // (MOE_BLOCK * ROW_CH)

    def row_map(i, be, nu):
        return (jnp.minimum(i, nu[0] - 1), 0)

    grid_spec = pltpu.PrefetchScalarGridSpec(
        num_scalar_prefetch=2,
        grid=(nblk,),
        in_specs=[pl.BlockSpec((MOE_BLOCK * ROW_CH, LANES), row_map),
                  pl.BlockSpec((1, 1, d, f), lambda i, be, nu: (layer, be[i], 0, 0)),
                  pl.BlockSpec((1, 1, d, f), lambda i, be, nu: (layer, be[i], 0, 0)),
                  pl.BlockSpec((1, 1, f, d), lambda i, be, nu: (layer, be[i], 0, 0))],
        out_specs=pl.BlockSpec((MOE_BLOCK * ROW_CH, LANES), lambda i, be, nu: (i, 0)),
        scratch_shapes=[pltpu.VMEM((d, f), BF16), pltpu.VMEM((d, f), BF16), pltpu.VMEM((f, d), BF16)])
    return pl.pallas_call(
        _expert_kernel,
        grid_spec=grid_spec,
        out_shape=jax.ShapeDtypeStruct((rows, LANES), U32),
        compiler_params=_cparams(("arbitrary",)),
        name="moe_experts",
    )(blk_expert, n_used, xs, wg, wu, wd)


def _combine_kernel(dest_ref, dnext_ref, ys_hbm, gate_ref, hp_ref, x_ref, gf_ref, sg_ref, su_ref, sd_ref, fn_ref,
                    o_ref, buf, ysum, grep, sem, *, final):
    i = pl.program_id(0)
    slot = i % 2

    def issue(dref, s):
        for tl0 in range(0, MOE_TT, DMA_BATCH):
            ds = [[dref[0, 0, k * MOE_TT + tl0 + u] for k in range(TOP_K)] for u in range(DMA_BATCH)]
            for u in range(DMA_BATCH):
                for k in range(TOP_K):
                    d = pl.multiple_of(ds[u][k] * ROW_CH, ROW_CH)
                    pltpu.make_async_copy(ys_hbm.at[pl.ds(d, ROW_CH), :],
                                          buf.at[s, k, pl.ds(ROW_CH * (tl0 + u), ROW_CH), :],
                                          sem.at[s]).start(priority=k % 2)

    pl.when(i == 0)(lambda: issue(dest_ref, 0))
    pl.when(i + 1 < pl.num_programs(0))(lambda: issue(dnext_ref, 1 - slot))

    xb = _load_rowchunks(hp_ref, (), MOE_TT, BF16)
    g = jnp.dot(xb, sg_ref[...], preferred_element_type=F32)
    u = jnp.dot(xb, su_ref[...], preferred_element_type=F32)
    y = jnp.dot(((g * jax.nn.sigmoid(g)) * u).astype(BF16), sd_ref[...], preferred_element_type=F32)

    for k in range(TOP_K):
        pltpu.make_async_copy(ys_hbm.at[pl.ds(0, MOE_TT * ROW_CH), :], buf.at[slot, k], sem.at[slot]).wait()
    for c in range(ROW_CH):
        grep[pl.ds(c, MOE_TT, stride=ROW_CH), :] = gate_ref[...]
    gates = grep[...]
    lo_sum = None
    hi_sum = None
    for k in range(TOP_K):
        lo, hi = _unpack_bf16_pairs(buf[slot, k])
        gk = gates[:, k:k + 1]
        lo_sum = gk * lo if lo_sum is None else lo_sum + gk * lo
        hi_sum = gk * hi if hi_sum is None else hi_sum + gk * hi
    ysum[0] = lo_sum
    ysum[1] = hi_sum
    routed = jnp.concatenate([ysum[half, pl.ds(c, MOE_TT, stride=ROW_CH), :]
                              for half in range(2) for c in range(ROW_CH)], axis=1)
    x = x_ref[...] + gf_ref[0] * (y + routed)
    if final:
        ms = jnp.mean(x * x, axis=-1, keepdims=True)
        x = x * lax.rsqrt(ms + EPS) * fn_ref[...]
    o_ref[...] = x


def _combine(ys, dest_tiles, gates_t, h2p, x_new, mods, sg, su, sd, fn, *, n_batch, n_rows, n_lat, final):
    t, d = x_new.shape
    f = sg.shape[1]
    tiles_b = n_rows // MOE_TT
    lat_tiles = n_lat // MOE_TT
    n_tiles = t // MOE_TT

    def mod_map(i):
        return (jnp.where(i % tiles_b >= lat_tiles, n_batch, i // tiles_b) * 6 + 5, 0, 0)

    grid_spec = pltpu.PrefetchScalarGridSpec(
        num_scalar_prefetch=0,
        grid=(t // MOE_TT,),
        in_specs=[pl.BlockSpec((1, 1, TOP_K * MOE_TT), lambda i: (i, 0, 0), memory_space=pltpu.SMEM),
                  pl.BlockSpec((1, 1, TOP_K * MOE_TT), lambda i: (jnp.minimum(i + 1, n_tiles - 1), 0, 0),
                               memory_space=pltpu.SMEM),
                  pl.BlockSpec(memory_space=pl.ANY),
                  pl.BlockSpec((MOE_TT, TOP_K), lambda i: (i, 0)),
                  pl.BlockSpec((MOE_TT * ROW_CH, LANES), lambda i: (i, 0)),
                  pl.BlockSpec((MOE_TT, d), lambda i: (i, 0)),
                  pl.BlockSpec((1, 1, d), mod_map),
                  pl.BlockSpec((d, f), lambda i: (0, 0)),
                  pl.BlockSpec((d, f), lambda i: (0, 0)),
                  pl.BlockSpec((f, d), lambda i: (0, 0)),
                  pl.BlockSpec((1, d), lambda i: (0, 0))],
        out_specs=pl.BlockSpec((MOE_TT, d), lambda i: (i, 0)),
        scratch_shapes=[pltpu.VMEM((2, TOP_K, MOE_TT * ROW_CH, LANES), U32),
                        pltpu.VMEM((2, MOE_TT * ROW_CH, LANES), F32),
                        pltpu.VMEM((MOE_TT * ROW_CH, TOP_K), F32),
                        pltpu.SemaphoreType.DMA((2,))])
    return pl.pallas_call(
        functools.partial(_combine_kernel, final=final),
        grid_spec=grid_spec,
        out_shape=jax.ShapeDtypeStruct((t, d), F32),
        compiler_params=_cparams(("arbitrary",)),
        name="moe_combine",
    )(dest_tiles, dest_tiles, ys, gates_t, h2p, x_new, mods, sg, su, sd, fn)


def _rope_tables(n_lat, n_ctx):
    t = np.arange(n_lat)
    row = (t // GRID_W).astype(np.float32)
    col = (t % GRID_W).astype(np.float32)
    npairs = HEAD_DIM // 4
    inv_freq = jnp.asarray(ROPE_THETA, F32) ** (-jnp.arange(npairs, dtype=F32) / npairs)
    ang = jnp.concatenate([jnp.asarray(row)[:, None] * inv_freq, jnp.asarray(col)[:, None] * inv_freq], axis=-1)
    cos = jnp.repeat(jnp.cos(ang), 2, axis=-1)
    sin = jnp.repeat(jnp.sin(ang), 2, axis=-1)
    sign = jnp.asarray(np.tile(np.array([-1.0, 1.0], np.float32), HEAD_DIM // 2))
    cosf = jnp.tile(cos, (1, LANES // HEAD_DIM))
    sins = jnp.tile(sin * sign, (1, LANES // HEAD_DIM))
    cosf = jnp.concatenate([cosf, jnp.ones((n_ctx, LANES), F32)], axis=0)
    sins = jnp.concatenate([sins, jnp.zeros((n_ctx, LANES), F32)], axis=0)
    return cosf, sins


def _moe_block(logits_t, h2p, x_new, mods_l, rb, wg, wu, wd, sg, su, sd, fn, *, layer, n_batch, n_rows, n_lat,
               final):
    t = x_new.shape[0]
    idx, gates, pos, cnt = _router(logits_t, rb.reshape(N_EXPERTS, 1))
    counts = cnt[:, 0].astype(jnp.int32)
    padded = (counts + MOE_BLOCK - 1) // MOE_BLOCK * MOE_BLOCK
    pad_end = jnp.cumsum(padded)
    pad_start = pad_end - padded
    n_blocks = (t * TOP_K + N_EXPERTS * (MOE_BLOCK - 1) + MOE_BLOCK - 1) // MOE_BLOCK
    n_pad = n_blocks * MOE_BLOCK
    experts = jnp.arange(N_EXPERTS, dtype=jnp.int32)
    dest = jnp.sum(jnp.where(idx[:, :, None] == experts, pad_start, 0), axis=-1) + pos

    def tiles(tt):
        return dest.reshape(TOP_K, t // tt, tt).transpose(1, 0, 2).reshape(t // tt, 1, TOP_K * tt)

    blk_start = jnp.arange(n_blocks, dtype=jnp.int32) * MOE_BLOCK
    blk_expert = jnp.minimum(jnp.sum((pad_end[None, :] <= blk_start[:, None]).astype(jnp.int32), axis=1),
                             N_EXPERTS - 1)
    n_used = (pad_end[-1:] // MOE_BLOCK).astype(jnp.int32)
    tail = n_used + jnp.arange(N_EXPERTS + 1, dtype=jnp.int32)
    fill = jnp.stack([jnp.concatenate([jnp.maximum(pad_end - MOE_BLOCK, 0), jnp.minimum(tail, n_blocks - 1) * MOE_BLOCK]),
                      jnp.concatenate([counts > 0, tail < n_blocks]).astype(jnp.int32)]).astype(jnp.int32)
    xs = _dispatch(h2p, tiles(DISPATCH_TT), fill, n_pad)
    ys = _experts(xs, blk_expert, n_used, wg, wu, wd, layer)
    return _combine(ys, tiles(MOE_TT), gates.T, h2p, x_new, mods_l, sg.astype(BF16), su.astype(BF16),
                    sd.astype(BF16), fn, n_batch=n_batch, n_rows=n_rows, n_lat=n_lat, final=final)


def kernel(x, c, ctx, c_ctx, ada_w, ada_b, norm_mix, norm_ffn, ab_w_in, ab_w_out, na_rpb, gqa_q_gain,
           gqa_k_gain, diff_w_in, diff_w_out, diff_lq1, diff_lk1, diff_lq2, diff_lk2, diff_sub_gain,
           router_w, router_bias, expert_w_gate, expert_w_up, expert_w_down, shared_w_gate, shared_w_up,
           shared_w_down, final_norm):
    b, n, d = x.shape
    n_ctx = ctx.shape[1]
    depth = ada_w.shape[0]
    assert d == 2 * ROW_CH * LANES
    assert depth == 2 and n_ctx == TM and n % (NA_ROWS * GRID_W) == 0 and d % LANES == 0
    assert (n + n_ctx) % FLASH_TK == 0 and n % FLASH_TQ == 0 and n % DISPATCH_TT == 0 and n_ctx % DISPATCH_TT == 0

    cvec = jnp.concatenate([c, c_ctx[None], jnp.zeros((8 - b - 1, d), F32)], axis=0)
    mods = _mods(cvec, ada_w, ada_b)[:, :b + 1].reshape(depth, (b + 1) * 6, 1, d)
    cosf, sins = _rope_tables(n, n_ctx)
    gm = jnp.asarray(np.kron(np.eye(LANES // HEAD_DIM), np.full((HEAD_DIM, HEAD_DIM), 1.0 / HEAD_DIM)), BF16)
    ones = jnp.ones((1, LANES), F32)
    fn = final_norm.reshape(1, d)

    xa = jnp.concatenate([x, ctx], axis=1)

    w = ab_w_in[0]
    kb = [w[:, 2048 + HEAD_DIM * g: 2048 + HEAD_DIM * (g + 1)] for g in range(2)]
    vb = [w[:, 2176 + HEAD_DIM * g: 2176 + HEAD_DIM * (g + 1)] for g in range(2)]
    w0 = jnp.concatenate([w[:, :2048], kb[0], kb[0], kb[1], kb[1], vb[0], vb[0], vb[1], vb[1]],
                         axis=1).astype(BF16)
    plan0 = ([(None, False, True)] * 4 + [(None, False, False)] * 8 + [("q", True, True)] * 4
             + [("k", True, False)] * 2 + [(None, False, False)] * 2)
    qg = jnp.tile(gqa_q_gain[0].reshape(1, HEAD_DIM), (1, LANES // HEAD_DIM))
    kg = jnp.tile(gqa_k_gain[0].reshape(1, HEAD_DIM), (1, LANES // HEAD_DIM))
    qkv = _proj(xa, norm_mix[0].reshape(1, d), mods[0], 0, 1, w0, cosf, sins, gm, qg, kg, plan0, n)

    tl, tr = _na_bias_tiles(na_rpb[0])
    o_na = _na(qkv, tl, tr, n_lat=n, n_ctx=n_ctx, npairs=4, qc0=0, kc0=4, vc0=8)
    tbl_g = jnp.asarray([[12, 13, 14, 15], [16, 16, 17, 17], [18, 18, 19, 19]], jnp.int32)
    o_gqa = _flash(qkv, tbl_g, n_q=n, q_row0=0, n_keys=n + n_ctx, key_row0=0, tq=FLASH_TQ, tk=FLASH_TK)
    tbl_c = jnp.asarray([[0, 1, 2, 3, 12, 13, 14, 15], [4, 5, 6, 7, 16, 16, 17, 17],
                         [8, 9, 10, 11, 18, 18, 19, 19]], jnp.int32)
    o_ctx = _flash(qkv, tbl_c, n_q=n_ctx, q_row0=n, n_keys=n_ctx, key_row0=n, tq=n_ctx, tk=n_ctx)

    s_all = n + n_ctx
    x_new, h2p, lg = _outproj([o_na, o_gqa], o_ctx, ab_w_out[0].astype(BF16), xa, mods[0],
                              norm_ffn[0].reshape(1, d), router_w[0].T, n_lat=n, n_rows=s_all)
    xa = _moe_block(lg, h2p.reshape(-1, LANES), x_new.reshape(b * s_all, d), mods[0], router_bias[0],
                    expert_w_gate, expert_w_up, expert_w_down, shared_w_gate[0], shared_w_up[0],
                    shared_w_down[0], fn, layer=0, n_batch=b, n_rows=s_all, n_lat=n, final=False).reshape(b, s_all, d)

    lam_init = 0.8 - 0.6 * math.exp(-0.3 * 1)
    plan1 = [(None, True, True)] * 8 + [(None, True, False)] * 8 + [(None, False, False)] * 8
    qkv = _proj(xa, norm_mix[1].reshape(1, d), mods[1], 0, 1, diff_w_in[0].astype(BF16), cosf, sins, gm,
                ones, ones, plan1, n)
    tbl_d = jnp.asarray([list(range(0, 8)), list(range(8, 16)), list(range(16, 24))], jnp.int32)
    dp = [diff_lq1[0].reshape(1, HEAD_DIM), diff_lk1[0].reshape(1, HEAD_DIM),
          diff_lq2[0].reshape(1, HEAD_DIM), diff_lk2[0].reshape(1, HEAD_DIM),
          diff_sub_gain[0].reshape(1, LANES)]
    o_diff = _flash(qkv, tbl_d, n_q=n, q_row0=0, n_keys=n + n_ctx, key_row0=0, tq=FLASH_TQ, tk=FLASH_TK,
                    mode="diff", diff_params=dp, lam_init=lam_init)
    x_new, h2p, lg = _outproj([o_diff], None, diff_w_out[0].astype(BF16), xa, mods[1],
                              norm_ffn[1].reshape(1, d), router_w[1].T, n_lat=n, n_rows=n)
    return _moe_block(lg, h2p.reshape(-1, LANES), x_new.reshape(b * n, d), mods[1], router_bias[1],
                      expert_w_gate, expert_w_up, expert_w_down, shared_w_gate[1], shared_w_up[1],
                      shared_w_down[1], fn, layer=1, n_batch=b, n_rows=n, n_lat=n, final=True).reshape(b, n, d)
```
